```python
import math
import jax
import jax.numpy as jnp
from jax import lax
import numpy as np

D_MODEL = 2048
BATCH = 2
SEQ = 4096
DEPTH = 2

GRID_W = 64
CTX_LEN = 256
N_BRANCH = 3
DA_HEADS = 8
DA_HEAD_DIM = 64
DA_V_DIM = 2 * DA_HEAD_DIM
DA_WIDTH = DA_HEADS * DA_V_DIM
Q_BLOCK = 128
ROPE_THETA = 10000.0
SSD_HEADS = 16
SSD_HEAD_DIM = 64
SSD_GROUPS = 2
SSD_HPG = SSD_HEADS // SSD_GROUPS
SSD_STATE = 128
SSD_WIDTH = SSD_HEADS * SSD_HEAD_DIM
SSD_XBC = SSD_WIDTH + 2 * SSD_GROUPS * SSD_STATE
SSD_CONV = 5
SSD_CHUNK = 128
S5_GROUP = 16
S5_GROUPS = 64
S5_WIDTH = S5_GROUPS * S5_GROUP
S5_STATE = 64
D_FF = int(math.ceil(8 * D_MODEL / 3 / 256)) * 256
Q_OFF = 0
K_OFF = Q_OFF + DA_WIDTH
V_OFF = K_OFF + DA_WIDTH
Z_OFF = V_OFF + DA_WIDTH
XBC_OFF = Z_OFF + SSD_WIDTH
DT_OFF = XBC_OFF + SSD_XBC
U_OFF = DT_OFF + 2 * SSD_HEADS
GATE_OFF = U_OFF + S5_WIDTH
N_IN = GATE_OFF + N_BRANCH * D_MODEL
RMS_EPS = 1e-6

kernel_name = 'hybrid_diffattn_ssd_s5_block'


def rms_norm(x, g):
    xf = x.astype(jnp.float32)
    y = xf * lax.rsqrt(jnp.mean(xf * xf, axis=-1, keepdims=True) + RMS_EPS)
    return (y * g.astype(jnp.float32)).astype(x.dtype)


def modulate(h, shift, scale):
    return h * (1 + scale) + shift


def swiglu(h, w_gate, w_up, w_down):
    return (jax.nn.silu(h @ w_gate) * (h @ w_up)) @ w_down


def flip_seq(t):
    return jnp.flip(t, axis=1)


def axial_rope_tables(n_tokens):
    n_rows = n_tokens // GRID_W
    row = jnp.repeat(jnp.arange(n_rows, dtype=jnp.float32), GRID_W)
    col = jnp.tile(jnp.arange(GRID_W, dtype=jnp.float32), n_rows)
    half = DA_HEAD_DIM // 2
    inv_freq = ROPE_THETA ** (-jnp.arange(0, half, 2, dtype=jnp.float32) / half)

    def axis_angles(pos):
        a = pos[:, None] * inv_freq[None, :]
        return jnp.concatenate([a, a], axis=-1)

    ang = jnp.concatenate([axis_angles(row), axis_angles(col)], axis=-1)
    return jnp.cos(ang), jnp.sin(ang)


def rotate_half(x):
    x1, x2 = jnp.split(x, 2, axis=-1)
    return jnp.concatenate([-x2, x1], axis=-1)


def apply_axial_rope(x, cos, sin):
    half = DA_HEAD_DIM // 2
    rot = jnp.concatenate([rotate_half(x[..., :half]), rotate_half(x[..., half:])], axis=-1)
    c = cos[None, :, None, None, :]
    s = sin[None, :, None, None, :]
    return (x.astype(jnp.float32) * c + rot.astype(jnp.float32) * s).astype(x.dtype)


def diff_attn_block(q, k, v, lam):
    s = jnp.einsum('bqhcd,bkhcd->bhcqk', q, k).astype(jnp.float32) * (DA_HEAD_DIM ** -0.5)
    p = jax.nn.softmax(s, axis=-1)
    a = p[:, :, 0] - lam * p[:, :, 1]
    return jnp.einsum('bhqk,bkhe->bqhe', a.astype(v.dtype), v)


def diff_attention_sweep(q, k, v, lam):
    b, n_q = q.shape[:2]
    n_blk = n_q // Q_BLOCK
    qb = jnp.moveaxis(q.reshape(b, n_blk, Q_BLOCK, DA_HEADS, 2, DA_HEAD_DIM), 1, 0)
    ob = lax.map(lambda qq: diff_attn_block(qq, k, v, lam), qb)
    return jnp.moveaxis(ob, 0, 1).reshape(b, n_q, DA_HEADS, DA_V_DIM)


def depthwise_conv_centred(x, w, b):
    pad = SSD_CONV // 2
    y = lax.conv_general_dilated(x, w[:, None, :], window_strides=(1,), padding=[(pad, pad)],
                                 dimension_numbers=('NWC', 'WIO', 'NWC'),
                                 feature_group_count=x.shape[-1])
    return y + b


def ssd_inputs(xbc_raw, dt_raw, conv_w, conv_b, dt_bias):
    b, n = xbc_raw.shape[:2]
    xbc = jax.nn.silu(depthwise_conv_centred(xbc_raw, conv_w, conv_b)).astype(jnp.float32)
    gn = SSD_GROUPS * SSD_STATE
    xs = xbc[..., :SSD_WIDTH].reshape(b, n, SSD_GROUPS, SSD_HPG, SSD_HEAD_DIM)
    bm = xbc[..., SSD_WIDTH:SSD_WIDTH + gn].reshape(b, n, SSD_GROUPS, SSD_STATE)
    cm = xbc[..., SSD_WIDTH + gn:].reshape(b, n, SSD_GROUPS, SSD_STATE)
    dt = jax.nn.softplus(dt_raw.astype(jnp.float32).reshape(b, n, 2, SSD_HEADS)
                         + dt_bias.astype(jnp.float32))
    return xs, bm, cm, dt.reshape(b, n, 2, SSD_GROUPS, SSD_HPG)


def ssd_chunked_scan(xs, dt, a, bm, cm, h0):
    b, n = xs.shape[:2]
    nc = n // SSD_CHUNK
    q = SSD_CHUNK
    xs = xs.reshape(b, nc, q, SSD_GROUPS, SSD_HPG, SSD_HEAD_DIM)
    dt = dt.reshape(b, nc, q, SSD_GROUPS, SSD_HPG)
    bm = bm.reshape(b, nc, q, SSD_GROUPS, SSD_STATE)
    cm = cm.reshape(b, nc, q, SSD_GROUPS, SSD_STATE)
    acum = jnp.cumsum(dt * a, axis=2)
    xdt = xs * dt[..., None]
    ac = jnp.moveaxis(acum, 2, -1)
    seg = ac[..., :, None] - ac[..., None, :]
    lower = jnp.tril(jnp.ones((q, q), dtype=bool))
    decay = jnp.exp(jnp.where(lower, seg, -jnp.inf))
    cb = jnp.einsum('bcign,bcjgn->bcgij', cm, bm)
    y_diag = jnp.einsum('bcgkij,bcjgkp->bcigkp', cb[:, :, :, None] * decay, xdt)
    decay_end = jnp.exp(acum[:, :, -1:] - acum)
    states = jnp.einsum('bcjgn,bcjgkp->bcgkpn', bm, xdt * decay_end[..., None])
    chunk_decay = jnp.exp(acum[:, :, -1])

    def step(h, inp):
        st, dec = inp
        return h * dec[..., None, None] + st, h

    h_final, h_prev = lax.scan(step, h0, (jnp.moveaxis(states, 1, 0), jnp.moveaxis(chunk_decay, 1, 0)))
    h_prev = jnp.moveaxis(h_prev, 0, 1)
    y_off = jnp.einsum('bcign,bcgkpn->bcigkp', cm, h_prev) * jnp.exp(acum)[..., None]
    return (y_diag + y_off).reshape(b, n, SSD_GROUPS, SSD_HPG, SSD_HEAD_DIM), h_final


def ssd_bidir(xs, bm, cm, dt, a, h0_f, h0_b):
    y_f, h_f = ssd_chunked_scan(xs, dt[:, :, 0], a[0], bm, cm, h0_f)
    y_b, h_b = ssd_chunked_scan(flip_seq(xs), flip_seq(dt[:, :, 1]), a[1],
                                flip_seq(bm), flip_seq(cm), h0_b)
    return y_f + flip_seq(y_b), h_f, h_b


def s5_discretise(lam_re, lam_im, log_step, b_re, b_im):
    lam = lax.complex(lam_re.astype(jnp.float32), lam_im.astype(jnp.float32))
    delta = jnp.exp(log_step.astype(jnp.float32))[..., None]
    lam_bar = jnp.exp(lam * delta)
    bmat = lax.complex(b_re.astype(jnp.float32), b_im.astype(jnp.float32))
    b_bar = ((lam_bar - 1.0) / lam)[..., None] * bmat[None]
    return lam_bar, b_bar


def _linear_recurrence(left, right):
    a_l, b_l = left
    a_r, b_r = right
    return a_l * a_r, a_r * b_l + b_r


def s5_scan(u, lam_bar, b_bar, h0):
    bu = jnp.einsum('gpe,blge->blgp', b_bar, u.astype(jnp.float32).astype(jnp.complex64))
    bu = bu.at[:, 0].add(lam_bar * h0)
    a = jnp.broadcast_to(lam_bar, bu.shape)
    _, h = lax.associative_scan(_linear_recurrence, (a, bu), axis=1)
    return h, h[:, -1]


def s5_bidir(u, lam_bar, b_bar, h0_f, h0_b):
    h_f, last_f = s5_scan(u, lam_bar[0], b_bar[0], h0_f)
    h_b, last_b = s5_scan(flip_seq(u), lam_bar[1], b_bar[1], h0_b)
    return h_f + flip_seq(h_b), last_f, last_b


def token_mixing(h_lat, h_ctx, need_ctx, lam_init, w_in, da_lambda, da_subln,
                 conv_w, conv_b, dt_bias, a_log, ssd_d, ssd_norm,
                 lam_re, lam_im, log_step, b_re, b_im, c_re, c_im, s5_d, glu_w, glu_b,
                 w_branch, w_out):
    dtype = h_lat.dtype
    b, n_lat = h_lat.shape[:2]
    p_lat = h_lat @ w_in
    p_ctx = h_ctx @ w_in

    lf = da_lambda.astype(jnp.float32)
    lam = jnp.exp(jnp.sum(lf[0] * lf[1])) - jnp.exp(jnp.sum(lf[2] * lf[3])) + lam_init

    def qkv(p):
        m = p.shape[1]
        q = p[..., Q_OFF:K_OFF].reshape(b, m, DA_HEADS, 2, DA_HEAD_DIM)
        k = p[..., K_OFF:V_OFF].reshape(b, m, DA_HEADS, 2, DA_HEAD_DIM)
        v = p[..., V_OFF:Z_OFF].reshape(b, m, DA_HEADS, DA_V_DIM)
        return q, k, v

    def attn_post(o):
        return (rms_norm(o, da_subln) * (1.0 - lam_init)).reshape(o.shape[0], o.shape[1], DA_WIDTH)

    cos, sin = axial_rope_tables(n_lat)
    q_l, k_l, v_l = qkv(p_lat)
    q_l = apply_axial_rope(q_l, cos, sin)
    k_l = apply_axial_rope(k_l, cos, sin)
    q_c, k_c, v_c = qkv(p_ctx)
    y_attn_lat = attn_post(diff_attention_sweep(q_l, jnp.concatenate([k_c, k_l], axis=1),
                                                jnp.concatenate([v_c, v_l], axis=1), lam))

    a = -jnp.exp(a_log.astype(jnp.float32)).reshape(2, SSD_GROUPS, SSD_HPG)
    d_ssd = ssd_d.astype(jnp.float32).reshape(SSD_GROUPS, SSD_HPG, 1)

    def ssd_post(y, xs, p):
        y = (y + d_ssd * xs).reshape(b, y.shape[1], SSD_WIDTH).astype(dtype)
        return rms_norm(y * jax.nn.silu(p[..., Z_OFF:XBC_OFF]), ssd_norm)

    xs_c, bm_c, cm_c, dt_c = ssd_inputs(p_ctx[..., XBC_OFF:DT_OFF], p_ctx[..., DT_OFF:U_OFF],
                                        conv_w, conv_b, dt_bias)
    h_zero = jnp.zeros((b, SSD_GROUPS, SSD_HPG, SSD_HEAD_DIM, SSD_STATE), jnp.float32)
    y_ssd_c, hf_c, hb_c = ssd_bidir(xs_c, bm_c, cm_c, dt_c, a, h_zero, h_zero)
    xs_l, bm_l, cm_l, dt_l = ssd_inputs(p_lat[..., XBC_OFF:DT_OFF], p_lat[..., DT_OFF:U_OFF],
                                        conv_w, conv_b, dt_bias)
    y_ssd_l, _, _ = ssd_bidir(xs_l, bm_l, cm_l, dt_l, a, hf_c, hb_c)
    y_ssd_lat = ssd_post(y_ssd_l, xs_l, p_lat)

    lam_bar, b_bar = s5_discretise(lam_re, lam_im, log_step, b_re, b_im)
    cmat = lax.complex(c_re.astype(jnp.float32), c_im.astype(jnp.float32))
    d_s5 = s5_d.astype(jnp.float32).reshape(S5_GROUPS, S5_GROUP)

    def s5_post(hs, u):
        y = jnp.real(jnp.einsum('gep,blgp->blge', cmat, hs)) + d_s5 * u.astype(jnp.float32)
        y = jax.nn.gelu(y.reshape(b, u.shape[1], S5_WIDTH).astype(dtype))
        ab = y @ glu_w + glu_b
        return ab[..., :S5_WIDTH] * jax.nn.sigmoid(ab[..., S5_WIDTH:])

    u_c = p_ctx[..., U_OFF:GATE_OFF].reshape(b, h_ctx.shape[1], S5_GROUPS, S5_GROUP)
    s_zero = jnp.zeros((b, S5_GROUPS, S5_STATE), jnp.complex64)
    hs_c, lf_c, lb_c = s5_bidir(u_c, lam_bar, b_bar, s_zero, s_zero)
    u_l = p_lat[..., U_OFF:GATE_OFF].reshape(b, n_lat, S5_GROUPS, S5_GROUP)
    hs_l, _, _ = s5_bidir(u_l, lam_bar, b_bar, lf_c, lb_c)
    y_s5_lat = s5_post(hs_l, u_l)

    def merge(p, y_a, y_b, y_c):
        m = p.shape[1]
        ys = jnp.stack([y_a, y_b, y_c], axis=2)
        branches = jnp.einsum('blnw,nwd->blnd', ys, w_branch)
        gates = jax.nn.sigmoid(p[..., GATE_OFF:].reshape(b, m, N_BRANCH, D_MODEL))
        return jnp.sum(gates * branches, axis=2) @ w_out

    out_lat = merge(p_lat, y_attn_lat, y_ssd_lat, y_s5_lat)
    out_ctx = None
    if need_ctx:
        y_attn_c = attn_post(diff_attn_block(q_c, k_c, v_c, lam))
        out_ctx = merge(p_ctx, y_attn_c, ssd_post(y_ssd_c, xs_c, p_ctx), s5_post(hs_c, u_c))
    return out_lat, out_ctx


def setup_inputs(seed: int = 0) -> dict:
    key = jax.random.key(seed)
    ks = iter(jax.random.split(key, 40))

    def nrm(shape, scale):
        return jax.random.normal(next(ks), shape, jnp.float32) * scale

    def gain(shape):
        return 1.0 + nrm(shape, 0.02)

    def unif(shape, lo, hi):
        return jax.random.uniform(next(ks), shape, jnp.float32, lo, hi)

    L = DEPTH
    x = nrm((BATCH, SEQ, D_MODEL), 1.0)
    c = nrm((BATCH, D_MODEL), 1.0)
    ctx = nrm((BATCH, CTX_LEN, D_MODEL), 1.0)
    c_ctx = nrm((D_MODEL,), 1.0)
    ada_w = nrm((L, D_MODEL, 6 * D_MODEL), 0.5 * D_MODEL ** -0.5)
    ada_b = nrm((L, 6 * D_MODEL), 0.02)
    norm_mix_pre = gain((L, D_MODEL))
    norm_mix_post = gain((L, D_MODEL))
    norm_ffn_pre = gain((L, D_MODEL))
    norm_ffn_post = gain((L, D_MODEL))
    w_in = nrm((L, D_MODEL, N_IN), D_MODEL ** -0.5)
    da_lambda = nrm((L, 4, DA_HEAD_DIM), 0.1)
    da_subln = gain((L, DA_V_DIM))
    ssd_conv_w = nrm((L, SSD_CONV, SSD_XBC), SSD_CONV ** -0.5)
    ssd_conv_b = nrm((L, SSD_XBC), 0.01)
    dt0 = jnp.exp(unif((L, 2, SSD_HEADS), math.log(1e-3), math.log(1e-1)))
    ssd_dt_bias = dt0 + jnp.log(-jnp.expm1(-dt0))
    ssd_a_log = jnp.log(unif((L, 2, SSD_HEADS), 1.0, 16.0))
    ssd_d = gain((L, SSD_HEADS))
    ssd_norm = gain((L, SSD_WIDTH))
    s5_lam_re = -0.5 + nrm((L, 2, S5_GROUPS, S5_STATE), 0.01)
    s5_lam_im = (math.pi * jnp.arange(S5_STATE, dtype=jnp.float32)) + nrm((L, 2, S5_GROUPS, S5_STATE), 0.01)
    s5_log_step = unif((L, 2, S5_GROUPS), math.log(1e-3), math.log(1e-1))
    s5_b_re = nrm((L, S5_GROUPS, S5_STATE, S5_GROUP), (2.0 * S5_GROUP) ** -0.5)
    s5_b_im = nrm((L, S5_GROUPS, S5_STATE, S5_GROUP), (2.0 * S5_GROUP) ** -0.5)
    s5_c_re = nrm((L, S5_GROUPS, S5_GROUP, S5_STATE), (2.0 * S5_STATE) ** -0.5)
    s5_c_im = nrm((L, S5_GROUPS, S5_GROUP, S5_STATE), (2.0 * S5_STATE) ** -0.5)
    s5_d = nrm((L, S5_WIDTH), 1.0)
    s5_glu_w = nrm((L, S5_WIDTH, 2 * S5_WIDTH), S5_WIDTH ** -0.5)
    s5_glu_b = nrm((L, 2 * S5_WIDTH), 0.01)
    w_branch = nrm((L, N_BRANCH, DA_WIDTH, D_MODEL), DA_WIDTH ** -0.5)
    w_out = nrm((L, D_MODEL, D_MODEL), D_MODEL ** -0.5)
    ffn_w_gate = nrm((L, D_MODEL, D_FF), D_MODEL ** -0.5)
    ffn_w_up = nrm((L, D_MODEL, D_FF), D_MODEL ** -0.5)
    ffn_w_down = nrm((L, D_FF, D_MODEL), D_FF ** -0.5)
    return {'x': x, 'c': c, 'ctx': ctx, 'c_ctx': c_ctx, 'ada_w': ada_w, 'ada_b': ada_b,
            'norm_mix_pre': norm_mix_pre, 'norm_mix_post': norm_mix_post,
            'norm_ffn_pre': norm_ffn_pre, 'norm_ffn_post': norm_ffn_post,
            'w_in': w_in, 'da_lambda': da_lambda, 'da_subln': da_subln,
            'ssd_conv_w': ssd_conv_w, 'ssd_conv_b': ssd_conv_b, 'ssd_dt_bias': ssd_dt_bias,
            'ssd_a_log': ssd_a_log, 'ssd_d': ssd_d, 'ssd_norm': ssd_norm,
            's5_lam_re': s5_lam_re, 's5_lam_im': s5_lam_im, 's5_log_step': s5_log_step,
            's5_b_re': s5_b_re, 's5_b_im': s5_b_im, 's5_c_re': s5_c_re, 's5_c_im': s5_c_im,
            's5_d': s5_d, 's5_glu_w': s5_glu_w, 's5_glu_b': s5_glu_b,
            'w_branch': w_branch, 'w_out': w_out,
            'ffn_w_gate': ffn_w_gate, 'ffn_w_up': ffn_w_up, 'ffn_w_down': ffn_w_down}


def reference(x, c, ctx, c_ctx, ada_w, ada_b, norm_mix_pre, norm_mix_post, norm_ffn_pre, norm_ffn_post,
              w_in, da_lambda, da_subln, ssd_conv_w, ssd_conv_b, ssd_dt_bias, ssd_a_log, ssd_d, ssd_norm,
              s5_lam_re, s5_lam_im, s5_log_step, s5_b_re, s5_b_im, s5_c_re, s5_c_im, s5_d,
              s5_glu_w, s5_glu_b, w_branch, w_out, ffn_w_gate, ffn_w_up, ffn_w_down):
    xc = ctx
    silu_c = jax.nn.silu(c)
    silu_cc = jax.nn.silu(c_ctx)
    for l in range(DEPTH):
        last = l == DEPTH - 1
        lam_init = 0.8 - 0.6 * math.exp(-0.3 * l)
        mod_l = (silu_c @ ada_w[l] + ada_b[l]).reshape(x.shape[0], 1, 6, D_MODEL)
        mod_c = (silu_cc @ ada_w[l] + ada_b[l]).reshape(6, D_MODEL)
        sh1, sc1, g1, sh2, sc2, g2 = [mod_l[:, :, i] for i in range(6)]
        csh1, csc1, cg1, csh2, csc2, cg2 = [mod_c[i] for i in range(6)]

        h = modulate(rms_norm(x, norm_mix_pre[l]), sh1, sc1)
        hc = modulate(rms_norm(xc, norm_mix_pre[l]), csh1, csc1)
        o_lat, o_ctx = token_mixing(h, hc, not last, lam_init, w_in[l], da_lambda[l], da_subln[l],
                                    ssd_conv_w[l], ssd_conv_b[l], ssd_dt_bias[l], ssd_a_log[l],
                                    ssd_d[l], ssd_norm[l],
                                    s5_lam_re[l], s5_lam_im[l], s5_log_step[l], s5_b_re[l], s5_b_im[l],
                                    s5_c_re[l], s5_c_im[l], s5_d[l], s5_glu_w[l], s5_glu_b[l],
                                    w_branch[l], w_out[l])
        x = x + g1 * rms_norm(o_lat, norm_mix_post[l])
        h = modulate(rms_norm(x, norm_ffn_pre[l]), sh2, sc2)
        x = x + g2 * rms_norm(swiglu(h, ffn_w_gate[l], ffn_w_up[l], ffn_w_down[l]), norm_ffn_post[l])

        if not last:
            xc = xc + cg1 * rms_norm(o_ctx, norm_mix_post[l])
            hc = modulate(rms_norm(xc, norm_ffn_pre[l]), csh2, csc2)
            xc = xc + cg2 * rms_norm(swiglu(hc, ffn_w_gate[l], ffn_w_up[l], ffn_w_down[l]),
                                     norm_ffn_post[l])
    return x
```

```python
import functools
import math

import jax
import jax.numpy as jnp
from jax import lax
from jax.experimental import pallas as pl
from jax.experimental.pallas import tpu as pltpu

F32 = jnp.float32
BF16 = jnp.bfloat16
HIGHEST = lax.Precision.HIGHEST

GRID_W = 64
N_BRANCH = 3
DA_HEADS = 8
DA_HEAD_DIM = 64
DA_V_DIM = 2 * DA_HEAD_DIM
DA_WIDTH = DA_HEADS * DA_V_DIM
ROPE_THETA = 10000.0
SSD_HEADS = 16
SSD_HEAD_DIM = 64
SSD_GROUPS = 2
SSD_HPG = SSD_HEADS // SSD_GROUPS
SSD_STATE = 128
SSD_WIDTH = SSD_HEADS * SSD_HEAD_DIM
SSD_XBC = SSD_WIDTH + 2 * SSD_GROUPS * SSD_STATE
SSD_CONV = 5
SSD_CHUNK = 128
S5_GROUP = 16
S5_GROUPS = 64
S5_WIDTH = S5_GROUPS * S5_GROUP
S5_STATE = 64
S5_CHUNK = 16
S5_PAIRS = S5_GROUPS // 2
S5_PAIRS_PER_STEP = 4
RMS_EPS = 1e-6

REF_XBC_OFF = 4 * DA_WIDTH
REF_DT_OFF = REF_XBC_OFF + SSD_XBC
REF_U_OFF = REF_DT_OFF + 2 * SSD_HEADS
Q_OFF = 0
K_OFF = DA_WIDTH
V_OFF = 2 * DA_WIDTH
Z_OFF = 3 * DA_WIDTH
XBC_OFF = 4 * DA_WIDTH
U_OFF = XBC_OFF + SSD_XBC
GATE_OFF = U_OFF + S5_WIDTH

LANES = 128
SUBLANES = 8
VMEM_LIMIT_BYTES = 52 * 1024 * 1024
MOD_ROWS = 8

TOK_TILE = 256


def _cparams(*sem):
    return pltpu.CompilerParams(dimension_semantics=sem, vmem_limit_bytes=VMEM_LIMIT_BYTES)


def _rms(x):
    return x * lax.rsqrt(jnp.mean(x * x, axis=-1, keepdims=True) + RMS_EPS)


def _sigmoid(x):
    return 1.0 / (1.0 + jnp.exp(-x))


def _silu(x):
    return x * _sigmoid(x)


def _seg_of_tile(i, tm, n_lat, seq, batch):
    return jnp.where(i < n_lat // tm, i // (seq // tm), batch)


def _ada_kernel(c_ref, w_ref, b_ref, o_ref):
    c = c_ref[...]
    o_ref[0] = jnp.dot(_silu(c), w_ref[0], precision=HIGHEST, preferred_element_type=F32) + b_ref[0]


def _ada(cc, ada_w, ada_b):
    depth, d, n = ada_w.shape
    tn = 1024
    return pl.pallas_call(
        _ada_kernel,
        out_shape=jax.ShapeDtypeStruct((depth, MOD_ROWS, n), F32),
        grid=(depth, n // tn),
        in_specs=[pl.BlockSpec((MOD_ROWS, d), lambda l, j: (0, 0)),
                  pl.BlockSpec((1, d, tn), lambda l, j: (l, 0, j)),
                  pl.BlockSpec((1, 1, tn), lambda l, j: (l, 0, j))],
        out_specs=pl.BlockSpec((1, MOD_ROWS, tn), lambda l, j: (l, 0, j)),
        compiler_params=_cparams("parallel", "parallel"),
        name="ada_mod",
    )(cc, ada_w, ada_b.reshape(depth, 1, n))


def _norm_mod_kernel(x_ref, g_ref, mod_ref, o_ref, *, shift_idx):
    y = _rms(x_ref[...]) * g_ref[...]
    shift = mod_ref[0, shift_idx:shift_idx + 1, :]
    scale = mod_ref[0, shift_idx + 1:shift_idx + 2, :]
    o_ref[...] = (y * (1.0 + scale) + shift).astype(BF16)


def _norm_mod(x, g, mod, shift_idx, dims):
    t, d = x.shape
    tm = TOK_TILE
    seg = functools.partial(_seg_of_tile, tm=tm, n_lat=dims["nl"], seq=dims["seq"], batch=dims["batch"])
    return pl.pallas_call(
        functools.partial(_norm_mod_kernel, shift_idx=shift_idx),
        out_shape=jax.ShapeDtypeStruct((t, d), BF16),
        grid=(t // tm,),
        in_specs=[pl.BlockSpec((tm, d), lambda i: (i, 0)),
                  pl.BlockSpec((1, d), lambda i: (0, 0)),
                  pl.BlockSpec((1, 6, d), lambda i: (seg(i), 0, 0))],
        out_specs=pl.BlockSpec((tm, d), lambda i: (i, 0)),
        compiler_params=_cparams("parallel"),
        name="norm_mod",
    )(x, g.reshape(1, d), mod)


def _mm_kernel(x_ref, w_ref, o_ref):
    o_ref[...] = jnp.dot(x_ref[...], w_ref[...], preferred_element_type=F32).astype(o_ref.dtype)


def _matmul(x, w, out_dtype, tm, tn, name):
    m, k = x.shape
    n = w.shape[1]
    return pl.pallas_call(
        _mm_kernel,
        out_shape=jax.ShapeDtypeStruct((m, n), out_dtype),
        grid=(m // tm, n // tn),
        in_specs=[pl.BlockSpec((tm, k), lambda i, j: (i, 0)),
                  pl.BlockSpec((k, tn), lambda i, j: (0, j))],
        out_specs=pl.BlockSpec((tm, tn), lambda i, j: (i, j)),
        compiler_params=_cparams("parallel", "parallel"),
        name=name,
    )(x, w)


def _rope_tables(seq, ctx_len):
    n_rows = seq // GRID_W
    row = jnp.repeat(jnp.arange(n_rows, dtype=F32), GRID_W)
    col = jnp.tile(jnp.arange(GRID_W, dtype=F32), n_rows)
    half = DA_HEAD_DIM // 2
    inv_freq = ROPE_THETA ** (-jnp.arange(0, half, 2, dtype=F32) / half)
    ar = row[:, None] * inv_freq[None, :]
    ac = col[:, None] * inv_freq[None, :]
    ang = jnp.concatenate([ar, ar, ac, ac], axis=-1)
    ang = jnp.concatenate([ang, jnp.zeros((ctx_len, DA_HEAD_DIM), F32)], axis=0)
    cos = jnp.tile(jnp.cos(ang), (1, 2))
    sin = jnp.tile(jnp.sin(ang), (1, 2))
    first = (jnp.arange(LANES) % half) < (half // 2)
    sin_a = jnp.where(first[None, :], -sin, 0.0)
    sin_b = jnp.where(first[None, :], 0.0, sin)
    return cos, sin_a, sin_b


def _rope_kernel(p_ref, cos_ref, sa_ref, sb_ref, o_ref):
    cos = cos_ref[...]
    sa = sa_ref[...]
    sb = sb_ref[...]
    quarter = DA_HEAD_DIM // 4
    for h in range(2 * DA_HEADS):
        sl = slice(h * LANES, (h + 1) * LANES)
        x = p_ref[:, sl].astype(F32)
        r = x * cos + pltpu.roll(x, LANES - quarter, 1) * sa + pltpu.roll(x, quarter, 1) * sb
        if h < DA_HEADS:
            r = r * (DA_HEAD_DIM ** -0.5)
        o_ref[:, sl] = r.astype(BF16)


def _rope(p, tables, dims):
    t = p.shape[0]
    tm = TOK_TILE
    nl, seq, ctx_len = dims["nl"], dims["seq"], dims["ctx"]

    def tab(i):
        return (jnp.where(i < nl // tm, i % (seq // tm), seq // tm + (i - nl // tm) % (ctx_len // tm)), 0)

    w = 2 * DA_WIDTH
    tspec = pl.BlockSpec((tm, LANES), tab)
    return pl.pallas_call(
        _rope_kernel,
        out_shape=jax.ShapeDtypeStruct((t, w), BF16),
        grid=(t // tm,),
        in_specs=[pl.BlockSpec((tm, w), lambda i: (i, 0)), tspec, tspec, tspec],
        out_specs=pl.BlockSpec((tm, w), lambda i: (i, 0)),
        compiler_params=_cparams("parallel"),
        name="rope_qk",
    )(p, *tables)


def _attn_kernel(lam_ref, subln_ref, q_ref, kl_ref, kc_ref, vl_ref, vc_ref, o_ref, *, tq, tk, n_lat_q, seq, lam_init):
    qi = pl.program_id(2)
    lf = lam_ref[...]
    lam = (jnp.exp(jnp.sum(lf[0:1] * lf[1:2], axis=-1, keepdims=True))
           - jnp.exp(jnp.sum(lf[2:3] * lf[3:4], axis=-1, keepdims=True)) + lam_init)
    q = q_ref[...]
    lane = lax.broadcasted_iota(jnp.int32, q.shape, 1)
    zero = jnp.zeros_like(q)
    qs = jnp.concatenate([jnp.where(lane < DA_HEAD_DIM, q, zero), jnp.where(lane >= DA_HEAD_DIM, q, zero)], axis=0)

    def step(k, v, carry):
        m, l, acc = carry
        s = lax.dot_general(qs, k, (((1,), (1,)), ((), ())), preferred_element_type=F32)
        m_new = jnp.maximum(m, jnp.max(s, axis=-1, keepdims=True))
        alpha = jnp.exp(m - m_new)
        p = jnp.exp(s - m_new)
        l = alpha * l + jnp.sum(p, axis=-1, keepdims=True)
        acc = alpha * acc + jnp.dot(p.astype(BF16), v, preferred_element_type=F32)
        return m_new, l, acc

    init = (jnp.full((2 * tq, 1), -1e30, F32), jnp.zeros((2 * tq, 1), F32), jnp.zeros((2 * tq, DA_V_DIM), F32))
    carry = step(kc_ref[...], vc_ref[...], init)
    n_lat_k = jnp.where(qi < n_lat_q, seq // tk, 0)

    def body(j, c):
        off = pl.multiple_of(j * tk, tk)
        return step(kl_ref[pl.ds(off, tk), :], vl_ref[pl.ds(off, tk), :], c)

    _, l, acc = lax.fori_loop(0, n_lat_k, body, carry)
    o = acc / l
    out = o[:tq] - lam * o[tq:]
    y = _rms(out) * subln_ref[...] * (1.0 - lam_init)
    o_ref[...] = y.astype(BF16)


def _attention(qk, p, da_lambda, da_subln, lam_init, with_ctx, dims):
    batch, seq, ctx_len, nl = dims["batch"], dims["seq"], dims["ctx"], dims["nl"]
    tq = TOK_TILE
    tk = 512
    n_lat_q = seq // tq
    n_ctx_q = ctx_len // tq if with_ctx else 0
    rows = nl + (batch * ctx_len if with_ctx else 0)

    def q_map(b, h, i):
        r = jnp.where(i < n_lat_q, b * n_lat_q + i, nl // tq + b * (ctx_len // tq) + (i - n_lat_q))
        return (r, h)

    kv_lat = lambda off: pl.BlockSpec((seq, LANES), lambda b, h, i: (b, off + h))
    kv_ctx = lambda off: pl.BlockSpec((ctx_len, LANES), lambda b, h, i: (nl // ctx_len + b, off + h))
    return pl.pallas_call(
        functools.partial(_attn_kernel, tq=tq, tk=tk, n_lat_q=n_lat_q, seq=seq, lam_init=lam_init),
        out_shape=jax.ShapeDtypeStruct((rows, DA_WIDTH), BF16),
        grid=(batch, DA_HEADS, n_lat_q + n_ctx_q),
        in_specs=[pl.BlockSpec((4, DA_HEAD_DIM), lambda b, h, i: (0, 0)),
                  pl.BlockSpec((1, DA_V_DIM), lambda b, h, i: (0, 0)),
                  pl.BlockSpec((tq, LANES), q_map),
                  kv_lat(DA_HEADS), kv_ctx(DA_HEADS),
                  kv_lat(V_OFF // LANES), kv_ctx(V_OFF // LANES)],
        out_specs=pl.BlockSpec((tq, LANES), q_map),
        compiler_params=_cparams("parallel", "parallel", "arbitrary"),
        name="diff_attn",
    )(da_lambda, da_subln.reshape(1, DA_V_DIM), qk, qk, qk, p, p)


def _softplus(x):
    return jnp.maximum(x, 0.0) + jnp.log(1.0 + jnp.exp(-jnp.abs(x)))


def _ssd_dt_kernel(h_ref, w_ref, b_ref, alog_ref, dd_ref, ddt_ref):
    raw = jnp.dot(h_ref[...], w_ref[...], preferred_element_type=F32)
    dt = _softplus(raw + b_ref[...])
    lane = lax.broadcasted_iota(jnp.int32, dt.shape, 1)
    n = 2 * SSD_HEADS
    mult = jnp.where(lane < n, 1.0, jnp.where(lane < 2 * n, -jnp.exp(alog_ref[...]), 0.0))
    dd = dt * mult
    dd_ref[...] = dd
    ddt_ref[...] = dd.T


def _ssd_dt(h, w_dt2, bias2, alog2):
    t, d = h.shape
    tm = TOK_TILE
    return pl.pallas_call(
        _ssd_dt_kernel,
        out_shape=(jax.ShapeDtypeStruct((t, LANES), F32), jax.ShapeDtypeStruct((LANES, t), F32)),
        grid=(t // tm,),
        in_specs=[pl.BlockSpec((tm, d), lambda i: (i, 0)),
                  pl.BlockSpec((d, LANES), lambda i: (0, 0)),
                  pl.BlockSpec((1, LANES), lambda i: (0, 0)),
                  pl.BlockSpec((1, LANES), lambda i: (0, 0))],
        out_specs=(pl.BlockSpec((tm, LANES), lambda i: (i, 0)), pl.BlockSpec((LANES, tm), lambda i: (0, i))),
        compiler_params=_cparams("parallel"),
        name="ssd_dt",
    )(h, w_dt2, bias2, alog2)


def _conv_kernel(prev_ref, x_ref, next_ref, w_ref, b_ref, o_ref, *, tm, n_lat_tiles, lat_per_seq, ctx_per_seq):
    i = pl.program_id(0)
    is_lat = i < n_lat_tiles
    pos = jnp.where(is_lat, i % lat_per_seq, (i - n_lat_tiles) % ctx_per_seq)
    per = jnp.where(is_lat, lat_per_seq, ctx_per_seq)
    keep_prev = (pos > 0).astype(F32)
    keep_next = (pos < per - 1).astype(F32)
    ext = jnp.concatenate([prev_ref[...].astype(F32) * keep_prev, x_ref[...].astype(F32),
                           next_ref[...].astype(F32) * keep_next], axis=0)
    n = tm + 2 * SUBLANES
    w = w_ref[...]
    acc = jnp.zeros((tm, ext.shape[1]), F32) + b_ref[...]
    for k in range(SSD_CONV):
        shift = (SSD_CONV // 2 - k) % n
        rolled = ext if shift == 0 else pltpu.roll(ext, shift, 0)
        acc = acc + rolled[SUBLANES:SUBLANES + tm] * w[k:k + 1, :]
    o_ref[...] = _silu(acc).astype(BF16)


def _ssd_conv(p, conv_w, conv_b, dims):
    t = p.shape[0]
    tm = TOK_TILE
    cb = 512
    col0 = XBC_OFF // cb
    r8 = tm // SUBLANES
    last8 = t // SUBLANES - 1
    kern = functools.partial(_conv_kernel, tm=tm, n_lat_tiles=dims["nl"] // tm, lat_per_seq=dims["seq"] // tm,
                             ctx_per_seq=dims["ctx"] // tm)
    return pl.pallas_call(
        kern,
        out_shape=jax.ShapeDtypeStruct((t, SSD_XBC), BF16),
        grid=(t // tm, SSD_XBC // cb),
        in_specs=[pl.BlockSpec((SUBLANES, cb), lambda i, j: (jnp.maximum(i * r8 - 1, 0), col0 + j)),
                  pl.BlockSpec((tm, cb), lambda i, j: (i, col0 + j)),
                  pl.BlockSpec((SUBLANES, cb), lambda i, j: (jnp.minimum((i + 1) * r8, last8), col0 + j)),
                  pl.BlockSpec((SSD_CONV, cb), lambda i, j: (0, j)),
                  pl.BlockSpec((1, cb), lambda i, j: (0, j))],
        out_specs=pl.BlockSpec((tm, cb), lambda i, j: (i, j)),
        compiler_params=_cparams("parallel", "parallel"),
        name="ssd_conv",
    )(p, p, p, conv_w, conv_b.reshape(1, SSD_XBC))


def _ssd_scan_kernel(*refs, direction, final, ncc, write_ctx):
    if final:
        xbc_ref, dd_ref, ddt_ref, yf_ref, z_ref, dskip_ref, norm_ref, o_ref, h_ref = refs
    else:
        xbc_ref, dd_ref, ddt_ref, o_ref, h_ref = refs
    j = pl.program_id(1)
    q = SSD_CHUNK
    gw = SSD_HPG * SSD_HEAD_DIM

    @pl.when(j == 0)
    def _():
        h_ref[...] = jnp.zeros_like(h_ref)

    xbc = xbc_ref[...]
    xs = xbc[:, :SSD_WIDTH].astype(F32)
    bm = xbc[:, SSD_WIDTH:SSD_WIDTH + SSD_GROUPS * SSD_STATE]
    cm = xbc[:, SSD_WIDTH + SSD_GROUPS * SSD_STATE:]
    dd = dd_ref[...]
    ddt = ddt_ref[...]

    ii = lax.broadcasted_iota(jnp.int32, (q, q), 0)
    jj = lax.broadcasted_iota(jnp.int32, (q, q), 1)
    if direction == 0:
        mask = jj <= ii
        last = q - 1
    else:
        mask = jj >= ii
        last = 0
    tri = mask.astype(F32)
    tri_t = (ii <= jj).astype(F32) if direction == 0 else (ii >= jj).astype(F32)
    ac = jnp.dot(tri, dd, precision=HIGHEST, preferred_element_type=F32)
    ac_t = jnp.dot(ddt, tri_t, precision=HIGHEST, preferred_element_type=F32)

    er = lax.broadcasted_iota(jnp.int32, (LANES, SSD_WIDTH), 0)
    ec = lax.broadcasted_iota(jnp.int32, (LANES, SSD_WIDTH), 1) // SSD_HEAD_DIM
    dt_col = direction * SSD_HEADS
    ac_col = 2 * SSD_HEADS + direction * SSD_HEADS
    dt_exp = jnp.dot(dd, (er == ec + dt_col).astype(F32), precision=HIGHEST, preferred_element_type=F32)
    ac_exp = jnp.dot(ac, (er == ec + ac_col).astype(F32), precision=HIGHEST, preferred_element_type=F32)
    ac_last = ac_exp[last:last + 1, :]
    eac = jnp.exp(ac_exp)
    dec_end = jnp.exp(ac_last - ac_exp)
    chunk_dec = jnp.exp(ac_last)
    xdt = xs * dt_exp
    xdt_b = xdt.astype(BF16)
    xde_b = (xdt * dec_end).astype(BF16)

    lane = lax.broadcasted_iota(jnp.int32, (q, LANES), 1)
    left = lane < SSD_HEAD_DIM
    zero_b = jnp.zeros((q, LANES), BF16)
    pieces = []
    for g in range(SSD_GROUPS):
        bg = bm[:, g * SSD_STATE:(g + 1) * SSD_STATE]
        cg = cm[:, g * SSD_STATE:(g + 1) * SSD_STATE]
        cb = lax.dot_general(cg, bg, (((1,), (1,)), ((), ())), preferred_element_type=F32)
        h_t = h_ref[g]
        y_off = jnp.dot(cg, h_t.astype(BF16), preferred_element_type=F32) * eac[:, g * gw:(g + 1) * gw]
        bg_t = bg.astype(F32).T.astype(BF16)
        s_t = jnp.dot(bg_t, xde_b[:, g * gw:(g + 1) * gw], preferred_element_type=F32)
        h_ref[g] = h_t * chunk_dec[:, g * gw:(g + 1) * gw] + s_t
        for kp in range(SSD_HPG // 2):
            k0 = g * SSD_HPG + 2 * kp
            ms = []
            for k in (k0, k0 + 1):
                c = ac_col + k
                seg = ac[:, c:c + 1] - ac_t[c:c + 1, :]
                ms.append((cb * jnp.exp(jnp.where(mask, seg, -1e30))).astype(BF16))
            xp = xdt_b[:, k0 * SSD_HEAD_DIM:k0 * SSD_HEAD_DIM + LANES]
            y_diag = (jnp.dot(ms[0], jnp.where(left, xp, zero_b), preferred_element_type=F32)
                      + jnp.dot(ms[1], jnp.where(left, zero_b, xp), preferred_element_type=F32))
            pieces.append(y_diag + y_off[:, kp * LANES:(kp + 1) * LANES])
    y = jnp.concatenate(pieces, axis=1)

    def emit():
        if final:
            yt = y + yf_ref[...] + dskip_ref[...] * xs
            yt = yt * _silu(z_ref[...].astype(F32))
            o_ref[...] = (_rms(yt) * norm_ref[...]).astype(BF16)
        else:
            o_ref[...] = y

    if write_ctx:
        emit()
    else:
        pl.when(j >= ncc)(emit)


def _ssd_scan(xbc, dd, ddt, direction, dims, with_ctx, final_args=None):
    batch, seq, ctx_len, nl = dims["batch"], dims["seq"], dims["ctx"], dims["nl"]
    q = SSD_CHUNK
    ncl, ncc = seq // q, ctx_len // q
    final = final_args is not None
    rows = nl + (batch * ctx_len if with_ctx else 0)

    def chunk(b, j):
        jc = j if direction == 0 else ncc - 1 - j
        jl = (j - ncc) if direction == 0 else ncl - 1 - (j - ncc)
        return jnp.where(j < ncc, nl // q + b * ncc + jc, b * ncl + jl)

    def out_chunk(b, j):
        if with_ctx:
            return chunk(b, j)
        return chunk(b, jnp.maximum(j, ncc))

    in_specs = [pl.BlockSpec((q, SSD_XBC), lambda b, j: (chunk(b, j), 0)),
                pl.BlockSpec((q, LANES), lambda b, j: (chunk(b, j), 0)),
                pl.BlockSpec((LANES, q), lambda b, j: (0, chunk(b, j)))]
    args = [xbc, dd, ddt]
    if final:
        yf, p, dskip, norm = final_args
        in_specs += [pl.BlockSpec((q, SSD_WIDTH), lambda b, j: (out_chunk(b, j), 0)),
                     pl.BlockSpec((q, SSD_WIDTH), lambda b, j: (chunk(b, j), Z_OFF // SSD_WIDTH)),
                     pl.BlockSpec((1, SSD_WIDTH), lambda b, j: (0, 0)),
                     pl.BlockSpec((1, SSD_WIDTH), lambda b, j: (0, 0))]
        args += [yf, p, dskip, norm]
    return pl.pallas_call(
        functools.partial(_ssd_scan_kernel, direction=direction, final=final, ncc=ncc, write_ctx=with_ctx),
        out_shape=jax.ShapeDtypeStruct((rows, SSD_WIDTH), BF16 if final else F32),
        grid=(batch, ncc + ncl),
        in_specs=in_specs,
        out_specs=pl.BlockSpec((q, SSD_WIDTH), lambda b, j: (out_chunk(b, j), 0)),
        scratch_shapes=[pltpu.VMEM((SSD_GROUPS, SSD_STATE, SSD_HPG * SSD_HEAD_DIM), F32)],
        compiler_params=_cparams("parallel", "arbitrary"),
        name="ssd_scan_bwd" if direction else "ssd_scan_fwd",
    )(*args)


def _s5_matrices(lam_re, lam_im, log_step, b_re, b_im, c_re, c_im, s5_d):
    tc = S5_CHUNK
    delta = jnp.exp(log_step.astype(F32))[..., None]
    lr = lam_re.astype(F32) * delta
    li = lam_im.astype(F32) * delta
    k = jnp.arange(tc + 1, dtype=F32)[:, None, None, None]
    mag = jnp.exp(k * lr[None])
    pw_re = mag * jnp.cos(k * li[None])
    pw_im = mag * jnp.sin(k * li[None])
    x = jnp.expm1(lr) * jnp.cos(li) - 2.0 * jnp.sin(0.5 * li) ** 2
    y = jnp.exp(lr) * jnp.sin(li)
    a, b = lam_re.astype(F32), lam_im.astype(F32)
    den = a * a + b * b
    co_re = (x * a + y * b) / den
    co_im = (y * a - x * b) / den
    bb_re = co_re[..., None] * b_re[None] - co_im[..., None] * b_im[None]
    bb_im = co_re[..., None] * b_im[None] + co_im[..., None] * b_re[None]

    def cmul(ar, ai, br, bi):
        return ar * br - ai * bi, ar * bi + ai * br

    w_re, w_im = cmul(pw_re[..., None], pw_im[..., None], bb_re[None], bb_im[None])
    taps = (jnp.einsum("gfp,kdgpe->kdgfe", c_re, w_re[:tc], precision=HIGHEST)
            - jnp.einsum("gfp,kdgpe->kdgfe", c_im, w_im[:tc], precision=HIGHEST))
    s_idx = jnp.arange(tc)[:, None]
    t_idx = jnp.arange(tc)[None, :]
    lag_f = jnp.clip(t_idx - s_idx, 0, tc - 1)
    lag_b = jnp.clip(s_idx - t_idx, 0, tc - 1)
    kf = jnp.where((t_idx >= s_idx)[..., None, None, None], taps[lag_f, 0], 0.0)
    kb = jnp.where((s_idx >= t_idx)[..., None, None, None], taps[lag_b, 1], 0.0)
    eye_e = jnp.eye(S5_GROUP, dtype=F32)
    skip = (s_idx == t_idx)[..., None, None, None] * (s5_d.reshape(S5_GROUPS, S5_GROUP)[None, None, :, :, None]
                                                      * eye_e[None, None, None])
    m = (kf + kb + skip).transpose(2, 0, 4, 1, 3).reshape(S5_GROUPS, tc * S5_GROUP, tc * S5_GROUP)

    bf_re = w_re[tc - 1 - jnp.arange(tc), 0]
    bf_im = w_im[tc - 1 - jnp.arange(tc), 0]
    bb2_re = w_re[jnp.arange(tc), 1]
    bb2_im = w_im[jnp.arange(tc), 1]
    bmat = jnp.stack([bf_re, bf_im, bb2_re, bb2_im], axis=0)
    bmat = bmat.transpose(2, 1, 4, 0, 3).reshape(S5_GROUPS, tc * S5_GROUP, 4, S5_STATE)

    def cout(pr, pi):
        o_re = c_re[None] * pr[:, :, None, :] - c_im[None] * pi[:, :, None, :]
        o_im = c_re[None] * pi[:, :, None, :] + c_im[None] * pr[:, :, None, :]
        return o_re, -o_im
    cf_re, cf_im = cout(pw_re[1:tc + 1, 0], pw_im[1:tc + 1, 0])
    cb_re, cb_im = cout(pw_re[tc - jnp.arange(tc), 1], pw_im[tc - jnp.arange(tc), 1])
    cmat = jnp.stack([cf_re, cf_im, cb_re, cb_im], axis=0)
    cmat = cmat.transpose(2, 0, 4, 1, 3).reshape(S5_GROUPS, 4, S5_STATE, tc * S5_GROUP)

    ge = tc * S5_GROUP
    zm = jnp.zeros((S5_PAIRS, ge, ge), F32)
    m_p = jnp.concatenate([jnp.concatenate([m[0::2], zm], axis=2), jnp.concatenate([zm, m[1::2]], axis=2)], axis=1)
    zb = jnp.zeros((S5_PAIRS, ge, 4, S5_STATE), F32)
    b_p = jnp.concatenate([jnp.concatenate([bmat[0::2], zb], axis=3), jnp.concatenate([zb, bmat[1::2]], axis=3)],
                          axis=1).reshape(S5_PAIRS, 2 * ge, 4 * 2 * S5_STATE)
    zc = jnp.zeros((S5_PAIRS, 4, S5_STATE, ge), F32)
    c_p = jnp.concatenate([jnp.concatenate([cmat[0::2], zc], axis=3), jnp.concatenate([zc, cmat[1::2]], axis=3)],
                          axis=2).reshape(S5_PAIRS, 4 * 2 * S5_STATE, 2 * ge)
    a16 = jnp.stack([pw_re[tc, 0], pw_im[tc, 0], pw_re[tc, 1], pw_im[tc, 1]], axis=0)
    a16 = a16.reshape(4, S5_PAIRS // S5_PAIRS_PER_STEP, S5_PAIRS_PER_STEP * 2 * S5_STATE).transpose(1, 0, 2)
    return m_p.astype(BF16), b_p.astype(BF16), c_p.astype(BF16), a16


def _gelu_tanh(x):
    return 0.5 * x * (1.0 + jnp.tanh(math.sqrt(2.0 / math.pi) * (x + 0.044715 * x * x * x)))


def _s5_kernel(u_ref, m_ref, b_ref, c_ref, a_ref, o_ref, s_ref, *, batch, ncl, ncc):
    npp = S5_PAIRS_PER_STEP
    sw = npp * LANES
    for jp in range(npp):
        s = jnp.dot(u_ref[jp], b_ref[jp], preferred_element_type=F32)
        for qn in range(4):
            s_ref[:, qn * sw + jp * LANES:qn * sw + (jp + 1) * LANES] = s[:, qn * LANES:(qn + 1) * LANES]

    a = a_ref[0]
    af_re, af_im, ab_re, ab_im = a[0:1], a[1:2], a[2:3], a[3:4]

    def advance(row, h, dir_off, ar, ai):
        h_re, h_im = h
        s_re = s_ref[pl.ds(row, 1), dir_off:dir_off + sw]
        s_im = s_ref[pl.ds(row, 1), dir_off + sw:dir_off + 2 * sw]
        s_ref[pl.ds(row, 1), dir_off:dir_off + sw] = h_re
        s_ref[pl.ds(row, 1), dir_off + sw:dir_off + 2 * sw] = h_im
        return ar * h_re - ai * h_im + s_re, ar * h_im + ai * h_re + s_im

    zero = jnp.zeros((1, sw), F32)
    for b in range(batch):
        ctx0 = batch * ncl + b * ncc
        lat0 = b * ncl

        def ctx_body(i, c):
            hf, hb = c
            return (advance(ctx0 + i, hf, 0, af_re, af_im), advance(ctx0 + ncc - 1 - i, hb, 2 * sw, ab_re, ab_im))

        def lat_body(i, c):
            hf, hb = c
            return (advance(lat0 + i, hf, 0, af_re, af_im), advance(lat0 + ncl - 1 - i, hb, 2 * sw, ab_re, ab_im))

        c = lax.fori_loop(0, ncc, ctx_body, ((zero, zero), (zero, zero)))
        lax.fori_loop(0, ncl, lat_body, c)

    for jp in range(npp):
        h = jnp.concatenate([s_ref[:, qn * sw + jp * LANES:qn * sw + (jp + 1) * LANES] for qn in range(4)], axis=1)
        y = (jnp.dot(u_ref[jp], m_ref[jp], preferred_element_type=F32)
             + jnp.dot(h.astype(BF16), c_ref[jp], preferred_element_type=F32))
        o_ref[jp] = _gelu_tanh(y).astype(BF16)


def _s5(u_r, mats, dims):
    m_p, b_p, c_p, a16 = mats
    npairs, nch, w = u_r.shape
    npp = S5_PAIRS_PER_STEP
    ncl, ncc = dims["seq"] // S5_CHUNK, dims["ctx"] // S5_CHUNK
    wspec = pl.BlockSpec((npp, w, w), lambda i: (i, 0, 0))
    return pl.pallas_call(
        functools.partial(_s5_kernel, batch=dims["batch"], ncl=ncl, ncc=ncc),
        out_shape=jax.ShapeDtypeStruct((npairs, nch, w), BF16),
        grid=(npairs // npp,),
        in_specs=[pl.BlockSpec((npp, nch, w), lambda i: (i, 0, 0)), wspec, wspec, wspec,
                  pl.BlockSpec((1, 4, npp * LANES), lambda i: (i, 0, 0))],
        out_specs=pl.BlockSpec((npp, nch, w), lambda i: (i, 0, 0)),
        scratch_shapes=[pltpu.VMEM((nch, 4 * npp * LANES), F32)],
        compiler_params=_cparams("parallel"),
        name="s5_scan",
    )(u_r, m_p, b_p, c_p, a16)


def _glu_kernel(x_ref, wa_ref, wb_ref, ba_ref, bb_ref, o_ref):
    x = x_ref[...]
    a = jnp.dot(x, wa_ref[...], preferred_element_type=F32) + ba_ref[...]
    b = jnp.dot(x, wb_ref[...], preferred_element_type=F32) + bb_ref[...]
    o_ref[...] = (a * _sigmoid(b)).astype(BF16)


def _glu(x, w, bias, tm):
    r, k = x.shape
    tn = 512
    nb = S5_WIDTH // tn
    return pl.pallas_call(
        _glu_kernel,
        out_shape=jax.ShapeDtypeStruct((r, S5_WIDTH), BF16),
        grid=(r // tm, nb),
        in_specs=[pl.BlockSpec((tm, k), lambda i, j: (i, 0)),
                  pl.BlockSpec((k, tn), lambda i, j: (0, j)),
                  pl.BlockSpec((k, tn), lambda i, j: (0, nb + j)),
                  pl.BlockSpec((1, tn), lambda i, j: (0, j)),
                  pl.BlockSpec((1, tn), lambda i, j: (0, nb + j))],
        out_specs=pl.BlockSpec((tm, tn), lambda i, j: (i, j)),
        compiler_params=_cparams("parallel", "parallel"),
        name="s5_glu",
    )(x, w, w, bias, bias)


def _merge_kernel(ya_ref, yb_ref, yc_ref, wa_ref, wb_ref, wc_ref, ga_ref, gb_ref, gc_ref, o_ref):
    acc = None
    for y_ref, w_ref, g_ref in ((ya_ref, wa_ref, ga_ref), (yb_ref, wb_ref, gb_ref), (yc_ref, wc_ref, gc_ref)):
        br = jnp.dot(y_ref[...], w_ref[0], preferred_element_type=F32)
        term = _sigmoid(g_ref[...].astype(F32)) * br
        acc = term if acc is None else acc + term
    o_ref[...] = acc.astype(BF16)


def _merge(ya, yb, yc, w_branch, p, tm):
    r, k = ya.shape
    d = w_branch.shape[2]
    tn = 512
    g0 = GATE_OFF // tn
    gstep = d // tn
    yspec = pl.BlockSpec((tm, k), lambda i, j: (i, 0))
    wspec = lambda n: pl.BlockSpec((1, k, tn), lambda i, j: (n, 0, j))
    gspec = lambda n: pl.BlockSpec((tm, tn), lambda i, j: (i, g0 + n * gstep + j))
    return pl.pallas_call(
        _merge_kernel,
        out_shape=jax.ShapeDtypeStruct((r, d), BF16),
        grid=(r // tm, d // tn),
        in_specs=[yspec, yspec, yspec, wspec(0), wspec(1), wspec(2), gspec(0), gspec(1), gspec(2)],
        out_specs=pl.BlockSpec((tm, tn), lambda i, j: (i, j)),
        compiler_params=_cparams("parallel", "parallel"),
        name="branch_merge",
    )(ya, yb, yc, w_branch, w_branch, w_branch, p, p, p)


def _out_proj_kernel(g_ref, w_ref, x_ref, mod_ref, npost_ref, npre_ref, xo_ref, ho_ref):
    o = jnp.dot(g_ref[...], w_ref[...], preferred_element_type=F32)
    gate = mod_ref[0, 2:3, :]
    xn = x_ref[...] + gate * (_rms(o) * npost_ref[...])
    xo_ref[...] = xn
    shift = mod_ref[0, 3:4, :]
    scale = mod_ref[0, 4:5, :]
    ho_ref[...] = (_rms(xn) * npre_ref[...] * (1.0 + scale) + shift).astype(BF16)


def _out_proj(g, w_out, x, mod, npost, npre, dims):
    r, d = g.shape
    tm = TOK_TILE
    seg = functools.partial(_seg_of_tile, tm=tm, n_lat=dims["nl"], seq=dims["seq"], batch=dims["batch"])
    row = pl.BlockSpec((tm, d), lambda i: (i, 0))
    vec = pl.BlockSpec((1, d), lambda i: (0, 0))
    return pl.pallas_call(
        _out_proj_kernel,
        out_shape=(jax.ShapeDtypeStruct((r, d), F32), jax.ShapeDtypeStruct((r, d), BF16)),
        grid=(r // tm,),
        in_specs=[row, pl.BlockSpec((d, d), lambda i: (0, 0)), row,
                  pl.BlockSpec((1, 6, d), lambda i: (seg(i), 0, 0)), vec, vec],
        out_specs=(row, row),
        compiler_params=_cparams("parallel"),
        name="out_proj",
    )(g, w_out, x, mod, npost.reshape(1, d), npre.reshape(1, d))


def _ffn_up_kernel(h_ref, wg_ref, wu_ref, o_ref):
    h = h_ref[...]
    a = jnp.dot(h, wg_ref[...], preferred_element_type=F32)
    b = jnp.dot(h, wu_ref[...], preferred_element_type=F32)
    o_ref[...] = (_silu(a) * b).astype(BF16)


def _ffn_up(h, wg, wu, tm):
    r, d = h.shape
    f = wg.shape[1]
    tn = 512
    wspec = pl.BlockSpec((d, tn), lambda i, j: (0, j))
    return pl.pallas_call(
        _ffn_up_kernel,
        out_shape=jax.ShapeDtypeStruct((r, f), BF16),
        grid=(r // tm, f // tn),
        in_specs=[pl.BlockSpec((tm, d), lambda i, j: (i, 0)), wspec, wspec],
        out_specs=pl.BlockSpec((tm, tn), lambda i, j: (i, j)),
        compiler_params=_cparams("parallel", "parallel"),
        name="ffn_up",
    )(h, wg, wu)


def _ffn_down_kernel(a_ref, w_ref, x_ref, mod_ref, npost_ref, o_ref, acc_ref):
    k = pl.program_id(1)

    @pl.when(k == 0)
    def _():
        acc_ref[...] = jnp.zeros_like(acc_ref)

    acc_ref[...] += jnp.dot(a_ref[...], w_ref[...], preferred_element_type=F32)

    @pl.when(k == pl.num_programs(1) - 1)
    def _():
        gate = mod_ref[0, 5:6, :]
        o_ref[...] = x_ref[...] + gate * (_rms(acc_ref[...]) * npost_ref[...])


def _ffn_down(act, wd, x, mod, npost, dims):
    r, f = act.shape
    d = wd.shape[1]
    tm = TOK_TILE * 2
    nk = 4
    tk = f // nk
    seg = functools.partial(_seg_of_tile, tm=tm, n_lat=dims["nl"], seq=dims["seq"], batch=dims["batch"])
    row = pl.BlockSpec((tm, d), lambda i, k: (i, 0))
    return pl.pallas_call(
        _ffn_down_kernel,
        out_shape=jax.ShapeDtypeStruct((r, d), F32),
        grid=(r // tm, nk),
        in_specs=[pl.BlockSpec((tm, tk), lambda i, k: (i, k)),
                  pl.BlockSpec((tk, d), lambda i, k: (k, 0)),
                  row,
                  pl.BlockSpec((1, 6, d), lambda i, k: (seg(i), 0, 0)),
                  pl.BlockSpec((1, d), lambda i, k: (0, 0))],
        out_specs=row,
        scratch_shapes=[pltpu.VMEM((tm, d), F32)],
        compiler_params=_cparams("parallel", "arbitrary"),
        name="ffn_down",
    )(act, wd, x, mod, npost.reshape(1, d))


def _pick_tile(rows, cands):
    for c in cands:
        if rows % c == 0:
            return c
    raise ValueError(f"no tile in {cands} divides {rows}")


def kernel(x, c, ctx, c_ctx, ada_w, ada_b, norm_mix_pre, norm_mix_post, norm_ffn_pre, norm_ffn_post, w_in, da_lambda, da_subln, ssd_conv_w, ssd_conv_b, ssd_dt_bias, ssd_a_log, ssd_d, ssd_norm, s5_lam_re, s5_lam_im, s5_log_step, s5_b_re, s5_b_im, s5_c_re, s5_c_im, s5_d, s5_glu_w, s5_glu_b, w_branch, w_out, ffn_w_gate, ffn_w_up, ffn_w_down):
    batch, seq, d = x.shape
    ctx_len = ctx.shape[1]
    depth = ada_w.shape[0]
    nl, nc = batch * seq, batch * ctx_len
    dims = dict(batch=batch, seq=seq, ctx=ctx_len, nl=nl, nc=nc)
    assert batch < MOD_ROWS and seq % TOK_TILE == 0 and ctx_len % TOK_TILE == 0 and seq % ctx_len == 0
    assert nc % (2 * TOK_TILE) == 0 and seq % GRID_W == 0

    xt = jnp.concatenate([x.reshape(nl, d), ctx.reshape(nc, d)], axis=0)
    cc = jnp.concatenate([c, c_ctx[None], jnp.zeros((MOD_ROWS - batch - 1, d), F32)], axis=0)
    mod_all = _ada(cc, ada_w, ada_b).reshape(depth, MOD_ROWS, 6, d)
    rope_tabs = _rope_tables(seq, ctx_len)
    nch = (nl + nc) // S5_CHUNK

    for l in range(depth):
        last = l == depth - 1
        with_ctx = not last
        lam_init = 0.8 - 0.6 * math.exp(-0.3 * l)
        mod = mod_all[l]
        rows = nl + nc if with_ctx else nl

        wl = w_in[l]
        w_main = jnp.concatenate([wl[:, :REF_DT_OFF], wl[:, REF_U_OFF:]], axis=1).astype(BF16)
        w_dt = wl[:, REF_DT_OFF:REF_U_OFF]
        n_dt = 2 * SSD_HEADS
        w_dt2 = jnp.concatenate([w_dt, w_dt, jnp.zeros((d, LANES - 2 * n_dt), F32)], axis=1).astype(BF16)
        bias = ssd_dt_bias[l].reshape(1, n_dt)
        bias2 = jnp.concatenate([bias, bias, jnp.zeros((1, LANES - 2 * n_dt), F32)], axis=1)
        alog = ssd_a_log[l].reshape(1, n_dt)
        alog2 = jnp.concatenate([alog, alog, jnp.zeros((1, LANES - 2 * n_dt), F32)], axis=1)

        h = _norm_mod(xt, norm_mix_pre[l], mod, 0, dims)
        p = _matmul(h, w_main, BF16, _pick_tile(nl + nc, (2176, 1088, 512, 256)), 512, "in_proj")

        qk = _rope(p, rope_tabs, dims)
        y_attn = _attention(qk, p, da_lambda[l], da_subln[l], lam_init, with_ctx, dims)

        dd, ddt = _ssd_dt(h, w_dt2, bias2, alog2)
        xbc = _ssd_conv(p, ssd_conv_w[l], ssd_conv_b[l], dims)
        y_f = _ssd_scan(xbc, dd, ddt, 0, dims, with_ctx)
        dskip = jnp.repeat(ssd_d[l], SSD_HEAD_DIM).reshape(1, SSD_WIDTH)
        y_ssd = _ssd_scan(xbc, dd, ddt, 1, dims, with_ctx,
                          final_args=(y_f, p, dskip, ssd_norm[l].reshape(1, SSD_WIDTH)))

        mats = _s5_matrices(s5_lam_re[l], s5_lam_im[l], s5_log_step[l], s5_b_re[l], s5_b_im[l],
                            s5_c_re[l], s5_c_im[l], s5_d[l])
        u = p[:, U_OFF:U_OFF + S5_WIDTH]
        u_r = (u.reshape(nch, S5_CHUNK, S5_PAIRS, 2, S5_GROUP).transpose(2, 0, 3, 1, 4)
               .reshape(S5_PAIRS, nch, 2 * S5_CHUNK * S5_GROUP))
        yg_r = _s5(u_r, mats, dims)
        yg = (yg_r.reshape(S5_PAIRS, nch, 2, S5_CHUNK, S5_GROUP).transpose(1, 3, 0, 2, 4)
              .reshape(nl + nc, S5_WIDTH))[:rows]
        tm = _pick_tile(rows, (1024, 512)) if not with_ctx else 2 * TOK_TILE
        y_s5 = _glu(yg, s5_glu_w[l].astype(BF16), s5_glu_b[l].reshape(1, 2 * S5_WIDTH), tm)

        g = _merge(y_attn, y_ssd, y_s5, w_branch[l].astype(BF16), p, tm)
        xt, h2 = _out_proj(g, w_out[l].astype(BF16), xt[:rows], mod, norm_mix_post[l], norm_ffn_pre[l], dims)
        act = _ffn_up(h2, ffn_w_gate[l].astype(BF16), ffn_w_up[l].astype(BF16), tm)
        xt = _ffn_down(act, ffn_w_down[l].astype(BF16), xt, mod, norm_ffn_post[l], dims)

    return xt[:nl].reshape(batch, seq, d)
```

```python
import functools
import math

import jax
import jax.numpy as jnp
from jax import lax
from jax.experimental import pallas as pl
from jax.experimental.pallas import tpu as pltpu

F32 = jnp.float32
BF16 = jnp.bfloat16
HIGHEST = lax.Precision.HIGHEST

GRID_W = 64
N_BRANCH = 3
DA_HEADS = 8
DA_HEAD_DIM = 64
DA_V_DIM = 2 * DA_HEAD_DIM
DA_WIDTH = DA_HEADS * DA_V_DIM
ROPE_THETA = 10000.0
SSD_HEADS = 16
SSD_HEAD_DIM = 64
SSD_GROUPS = 2
SSD_HPG = SSD_HEADS // SSD_GROUPS
SSD_STATE = 128
SSD_WIDTH = SSD_HEADS * SSD_HEAD_DIM
SSD_XBC = SSD_WIDTH + 2 * SSD_GROUPS * SSD_STATE
SSD_CONV = 5
SSD_CHUNK = 128
S5_GROUP = 16
S5_GROUPS = 64
S5_WIDTH = S5_GROUPS * S5_GROUP
S5_STATE = 64
S5_CHUNK = 16
S5_PAIRS = S5_GROUPS // 2
S5_PAIRS_PER_STEP = 4
RMS_EPS = 1e-6

REF_XBC_OFF = 4 * DA_WIDTH
REF_DT_OFF = REF_XBC_OFF + SSD_XBC
REF_U_OFF = REF_DT_OFF + 2 * SSD_HEADS
Q_OFF = 0
K_OFF = DA_WIDTH
V_OFF = 2 * DA_WIDTH
Z_OFF = 3 * DA_WIDTH
XBC_OFF = 4 * DA_WIDTH
U_OFF = XBC_OFF + SSD_XBC
GATE_OFF = U_OFF + S5_WIDTH

LANES = 128
SUBLANES = 8
VMEM_LIMIT_BYTES = 52 * 1024 * 1024
MOD_ROWS = 8

TOK_TILE = 256


def _cparams(*sem):
    return pltpu.CompilerParams(dimension_semantics=sem, vmem_limit_bytes=VMEM_LIMIT_BYTES)


def _rms(x):
    return x * lax.rsqrt(jnp.mean(x * x, axis=-1, keepdims=True) + RMS_EPS)


def _sigmoid(x):
    return 1.0 / (1.0 + jnp.exp(-x))


def _silu(x):
    return x * _sigmoid(x)


def _seg_of_tile(i, tm, n_lat, seq, batch):
    return jnp.where(i < n_lat // tm, i // (seq // tm), batch)


def _ada_kernel(c_ref, w_ref, b_ref, o_ref):
    c = c_ref[...]
    o_ref[0] = jnp.dot(_silu(c), w_ref[0], precision=HIGHEST, preferred_element_type=F32) + b_ref[0]


def _ada(cc, ada_w, ada_b):
    depth, d, n = ada_w.shape
    tn = 1024
    return pl.pallas_call(
        _ada_kernel,
        out_shape=jax.ShapeDtypeStruct((depth, MOD_ROWS, n), F32),
        grid=(depth, n // tn),
        in_specs=[pl.BlockSpec((MOD_ROWS, d), lambda l, j: (0, 0)),
                  pl.BlockSpec((1, d, tn), lambda l, j: (l, 0, j)),
                  pl.BlockSpec((1, 1, tn), lambda l, j: (l, 0, j))],
        out_specs=pl.BlockSpec((1, MOD_ROWS, tn), lambda l, j: (l, 0, j)),
        compiler_params=_cparams("parallel", "parallel"),
        name="ada_mod",
    )(cc, ada_w, ada_b.reshape(depth, 1, n))


def _norm_mod_kernel(x_ref, g_ref, mod_ref, o_ref, *, shift_idx):
    y = _rms(x_ref[...]) * g_ref[...]
    shift = mod_ref[0, shift_idx:shift_idx + 1, :]
    scale = mod_ref[0, shift_idx + 1:shift_idx + 2, :]
    o_ref[...] = (y * (1.0 + scale) + shift).astype(BF16)


def _norm_mod(x, g, mod, shift_idx, dims):
    t, d = x.shape
    tm = TOK_TILE
    seg = functools.partial(_seg_of_tile, tm=tm, n_lat=dims["nl"], seq=dims["seq"], batch=dims["batch"])
    return pl.pallas_call(
        functools.partial(_norm_mod_kernel, shift_idx=shift_idx),
        out_shape=jax.ShapeDtypeStruct((t, d), BF16),
        grid=(t // tm,),
        in_specs=[pl.BlockSpec((tm, d), lambda i: (i, 0)),
                  pl.BlockSpec((1, d), lambda i: (0, 0)),
                  pl.BlockSpec((1, 6, d), lambda i: (seg(i), 0, 0))],
        out_specs=pl.BlockSpec((tm, d), lambda i: (i, 0)),
        compiler_params=_cparams("parallel"),
        name="norm_mod",
    )(x, g.reshape(1, d), mod)


def _mm_kernel(x_ref, w_ref, o_ref):
    o_ref[...] = jnp.dot(x_ref[...], w_ref[...], preferred_element_type=F32).astype(o_ref.dtype)


def _matmul(x, w, out_dtype, tm, tn, name):
    m, k = x.shape
    n = w.shape[1]
    return pl.pallas_call(
        _mm_kernel,
        out_shape=jax.ShapeDtypeStruct((m, n), out_dtype),
        grid=(m // tm, n // tn),
        in_specs=[pl.BlockSpec((tm, k), lambda i, j: (i, 0)),
                  pl.BlockSpec((k, tn), lambda i, j: (0, j))],
        out_specs=pl.BlockSpec((tm, tn), lambda i, j: (i, j)),
        compiler_params=_cparams("parallel", "parallel"),
        name=name,
    )(x, w)


def _rope_tables(seq, ctx_len):
    n_rows = seq // GRID_W
    row = jnp.repeat(jnp.arange(n_rows, dtype=F32), GRID_W)
    col = jnp.tile(jnp.arange(GRID_W, dtype=F32), n_rows)
    half = DA_HEAD_DIM // 2
    inv_freq = ROPE_THETA ** (-jnp.arange(0, half, 2, dtype=F32) / half)
    ar = row[:, None] * inv_freq[None, :]
    ac = col[:, None] * inv_freq[None, :]
    ang = jnp.concatenate([ar, ar, ac, ac], axis=-1)
    ang = jnp.concatenate([ang, jnp.zeros((ctx_len, DA_HEAD_DIM), F32)], axis=0)
    cos = jnp.tile(jnp.cos(ang), (1, 2))
    sin = jnp.tile(jnp.sin(ang), (1, 2))
    first = (jnp.arange(LANES) % half) < (half // 2)
    sin_a = jnp.where(first[None, :], -sin, 0.0)
    sin_b = jnp.where(first[None, :], 0.0, sin)
    return cos, sin_a, sin_b


def _rope_kernel(p_ref, cos_ref, sa_ref, sb_ref, o_ref):
    cos = cos_ref[...]
    sa = sa_ref[...]
    sb = sb_ref[...]
    quarter = DA_HEAD_DIM // 4
    for h in range(2 * DA_HEADS):
        sl = slice(h * LANES, (h + 1) * LANES)
        x = p_ref[:, sl].astype(F32)
        r = x * cos + pltpu.roll(x, LANES - quarter, 1) * sa + pltpu.roll(x, quarter, 1) * sb
        if h < DA_HEADS:
            r = r * (DA_HEAD_DIM ** -0.5)
        o_ref[:, sl] = r.astype(BF16)


def _rope(p, tables, dims):
    t = p.shape[0]
    tm = TOK_TILE
    nl, seq, ctx_len = dims["nl"], dims["seq"], dims["ctx"]

    def tab(i):
        return (jnp.where(i < nl // tm, i % (seq // tm), seq // tm + (i - nl // tm) % (ctx_len // tm)), 0)

    w = 2 * DA_WIDTH
    tspec = pl.BlockSpec((tm, LANES), tab)
    return pl.pallas_call(
        _rope_kernel,
        out_shape=jax.ShapeDtypeStruct((t, w), BF16),
        grid=(t // tm,),
        in_specs=[pl.BlockSpec((tm, w), lambda i: (i, 0)), tspec, tspec, tspec],
        out_specs=pl.BlockSpec((tm, w), lambda i: (i, 0)),
        compiler_params=_cparams("parallel"),
        name="rope_qk",
    )(p, *tables)


ATTN_ROW_BLOCK = 64


def _attn_kernel(*refs, tq, tk, n_lat_k, lam_init):
    if n_lat_k:
        lam_ref, subln_ref, q_ref, kc_ref, vc_ref, kl_ref, vl_ref, o_ref = refs[:8]
    else:
        lam_ref, subln_ref, q_ref, kc_ref, vc_ref = refs[:5]
        o_ref = refs[-8]
    qs_ref, s_ref, p_ref, m_ref, a_ref, acc_ref, vx_ref = refs[-7:]
    rb = ATTN_ROW_BLOCK
    n_ctx = kc_ref.shape[0]

    n_lat = n_lat_k * tk

    @pl.when(pl.program_id(2) == 0)
    def _():
        if n_lat_k:
            vx_ref[0:n_lat, 0:LANES] = vl_ref[...]
        vx_ref[n_lat:, 0:LANES] = vc_ref[...]
        vx_ref[:, LANES:2 * LANES] = jnp.ones((vx_ref.shape[0], LANES), BF16)

    q = q_ref[...]
    lane = lax.broadcasted_iota(jnp.int32, q.shape, 1)
    zero = jnp.zeros_like(q)
    qs_ref[0:tq, :] = jnp.where(lane < DA_HEAD_DIM, q, zero)
    qs_ref[tq:2 * tq, :] = jnp.where(lane >= DA_HEAD_DIM, q, zero)
    m_ref[...] = jnp.full(m_ref.shape, -1e30, F32)
    acc_ref[...] = jnp.zeros(acc_ref.shape, F32)

    def key_rows(t, nk):
        return pl.ds(t * tk if isinstance(t, int) else pl.multiple_of(t * tk, tk), nk)

    def keys(t):
        if isinstance(t, int) and t == n_lat_k:
            return kc_ref[...], n_ctx
        return kl_ref[key_rows(t, tk), :], tk

    def scores(t, buf):
        k, nk = keys(t)
        s_ref[buf, :, 0:nk] = lax.dot_general(qs_ref[...], k, (((1,), (1,)), ((), ())),
                                              preferred_element_type=F32)

    def softmax(buf, nk):
        reps = nk // LANES
        for r in range(2 * tq // rb):
            rows = slice(r * rb, (r + 1) * rb)
            m_prev = m_ref[rows, :]
            m_new = jnp.maximum(m_prev, jnp.max(s_ref[buf, rows, 0:nk], axis=-1, keepdims=True))
            a_ref[buf, rows, :] = jnp.exp(m_prev - m_new)
            m_ref[rows, :] = m_new
            p = jnp.exp(s_ref[buf, rows, 0:nk] - jnp.concatenate([m_new] * reps, axis=1))
            p_ref[buf, rows, 0:nk] = p.astype(BF16)

    def weighted(t, buf):
        nk = n_ctx if isinstance(t, int) and t == n_lat_k else tk
        vx = vx_ref[key_rows(t, nk), :]
        alpha = a_ref[buf]
        pv = jnp.dot(p_ref[buf, :, 0:nk], vx, preferred_element_type=F32)
        acc_ref[...] = jnp.concatenate([alpha, alpha], axis=1) * acc_ref[...] + pv

    def stage(t, buf):
        scores(t + 1, 1 - buf)
        softmax(buf, tk)
        weighted(t - 1, 1 - buf)

    n_static = n_lat_k + 1
    loop_pairs = (n_lat_k - 2) // 2 if (n_lat_k >= 4 and n_lat_k % 2 == 0) else 0
    scores(0, 0)
    t = 0
    while t < n_static:
        if t == 1 and loop_pairs:
            def body(i, carry):
                stage(1 + 2 * i, 1)
                stage(2 + 2 * i, 0)
                return carry
            lax.fori_loop(0, loop_pairs, body, 0)
            t += 2 * loop_pairs
            continue
        if t + 1 < n_static:
            scores(t + 1, (t + 1) % 2)
        softmax(t % 2, n_ctx if t == n_lat_k else tk)
        if t >= 1:
            weighted(t - 1, (t - 1) % 2)
        t += 1
    weighted(n_lat_k, n_lat_k % 2)

    lf = lam_ref[...]
    lam = (jnp.exp(jnp.sum(lf[0:1] * lf[1:2], axis=-1, keepdims=True))
           - jnp.exp(jnp.sum(lf[2:3] * lf[3:4], axis=-1, keepdims=True)) + lam_init)
    o = acc_ref[:, 0:LANES] / acc_ref[:, LANES:2 * LANES]
    out = o[:tq] - lam * o[tq:]
    y = _rms(out) * subln_ref[...] * (1.0 - lam_init)
    o_ref[...] = y.astype(BF16)


def _attn_call(qk, p, da_lambda, da_subln, lam_init, dims, *, tq, q_row0, n_q, rows, with_lat, prev=None):
    batch, seq, ctx_len, nl = dims["batch"], dims["seq"], dims["ctx"], dims["nl"]
    tk = 512
    q_map = lambda b, h, i: (q_row0 // tq + b * n_q + i, h)
    kv_lat = lambda off: pl.BlockSpec((seq, LANES), lambda b, h, i: (b, off + h))
    kv_ctx = lambda off: pl.BlockSpec((ctx_len, LANES), lambda b, h, i: (nl // ctx_len + b, off + h))
    in_specs = [pl.BlockSpec((4, DA_HEAD_DIM), lambda b, h, i: (0, 0)),
                pl.BlockSpec((1, DA_V_DIM), lambda b, h, i: (0, 0)),
                pl.BlockSpec((tq, LANES), q_map),
                kv_ctx(DA_HEADS), kv_ctx(V_OFF // LANES)]
    args = [da_lambda, da_subln.reshape(1, DA_V_DIM), qk, qk, p]
    if with_lat:
        in_specs += [kv_lat(DA_HEADS), kv_lat(V_OFF // LANES)]
        args += [qk, p]
    aliases = {}
    if prev is not None:
        in_specs.append(pl.BlockSpec(memory_space=pl.ANY))
        args.append(prev)
        aliases = {len(args) - 1: 0}
    n_keys = ctx_len + (seq if with_lat else 0)
    scratch = [pltpu.VMEM((2 * tq, LANES), BF16), pltpu.VMEM((2, 2 * tq, tk), F32), pltpu.VMEM((2, 2 * tq, tk), BF16),
               pltpu.VMEM((2 * tq, LANES), F32), pltpu.VMEM((2, 2 * tq, LANES), F32),
               pltpu.VMEM((2 * tq, 2 * LANES), F32), pltpu.VMEM((n_keys, 2 * LANES), BF16)]
    kern = functools.partial(_attn_kernel, tq=tq, tk=tk, n_lat_k=seq // tk if with_lat else 0, lam_init=lam_init)
    return pl.pallas_call(
        kern,
        out_shape=jax.ShapeDtypeStruct((rows, DA_WIDTH), BF16),
        grid=(batch, DA_HEADS, n_q),
        in_specs=in_specs,
        out_specs=pl.BlockSpec((tq, LANES), q_map),
        scratch_shapes=scratch,
        input_output_aliases=aliases,
        compiler_params=_cparams("parallel", "parallel", "arbitrary"),
        name="diff_attn" if with_lat else "diff_attn_ctx",
    )(*args)


def _attention(qk, p, da_lambda, da_subln, lam_init, with_ctx, dims):
    batch, seq, ctx_len, nl = dims["batch"], dims["seq"], dims["ctx"], dims["nl"]
    rows = nl + (batch * ctx_len if with_ctx else 0)
    tq = 512 if seq % 512 == 0 else TOK_TILE
    y = _attn_call(qk, p, da_lambda, da_subln, lam_init, dims, tq=tq, q_row0=0, n_q=seq // tq, rows=rows,
                   with_lat=True)
    if with_ctx:
        tqc = TOK_TILE
        y = _attn_call(qk, p, da_lambda, da_subln, lam_init, dims, tq=tqc, q_row0=nl, n_q=ctx_len // tqc,
                       rows=rows, with_lat=False, prev=y)
    return y


def _softplus(x):
    return jnp.maximum(x, 0.0) + jnp.log(1.0 + jnp.exp(-jnp.abs(x)))


def _ssd_dt_kernel(h_ref, w_ref, b_ref, alog_ref, dd_ref, ddt_ref):
    raw = jnp.dot(h_ref[...], w_ref[...], preferred_element_type=F32)
    dt = _softplus(raw + b_ref[...])
    lane = lax.broadcasted_iota(jnp.int32, dt.shape, 1)
    n = 2 * SSD_HEADS
    mult = jnp.where(lane < n, 1.0, jnp.where(lane < 2 * n, -jnp.exp(alog_ref[...]), 0.0))
    dd = dt * mult
    dd_ref[...] = dd
    ddt_ref[...] = dd.T


def _ssd_dt(h, w_dt2, bias2, alog2):
    t, d = h.shape
    tm = TOK_TILE
    return pl.pallas_call(
        _ssd_dt_kernel,
        out_shape=(jax.ShapeDtypeStruct((t, LANES), F32), jax.ShapeDtypeStruct((LANES, t), F32)),
        grid=(t // tm,),
        in_specs=[pl.BlockSpec((tm, d), lambda i: (i, 0)),
                  pl.BlockSpec((d, LANES), lambda i: (0, 0)),
                  pl.BlockSpec((1, LANES), lambda i: (0, 0)),
                  pl.BlockSpec((1, LANES), lambda i: (0, 0))],
        out_specs=(pl.BlockSpec((tm, LANES), lambda i: (i, 0)), pl.BlockSpec((LANES, tm), lambda i: (0, i))),
        compiler_params=_cparams("parallel"),
        name="ssd_dt",
    )(h, w_dt2, bias2, alog2)


def _conv_kernel(prev_ref, x_ref, next_ref, w_ref, b_ref, o_ref, *, tm, n_lat_tiles, lat_per_seq, ctx_per_seq):
    i = pl.program_id(0)
    is_lat = i < n_lat_tiles
    pos = jnp.where(is_lat, i % lat_per_seq, (i - n_lat_tiles) % ctx_per_seq)
    per = jnp.where(is_lat, lat_per_seq, ctx_per_seq)
    keep_prev = (pos > 0).astype(F32)
    keep_next = (pos < per - 1).astype(F32)
    ext = jnp.concatenate([prev_ref[...].astype(F32) * keep_prev, x_ref[...].astype(F32),
                           next_ref[...].astype(F32) * keep_next], axis=0)
    n = tm + 2 * SUBLANES
    w = w_ref[...]
    acc = jnp.zeros((tm, ext.shape[1]), F32) + b_ref[...]
    for k in range(SSD_CONV):
        shift = (SSD_CONV // 2 - k) % n
        rolled = ext if shift == 0 else pltpu.roll(ext, shift, 0)
        acc = acc + rolled[SUBLANES:SUBLANES + tm] * w[k:k + 1, :]
    o_ref[...] = _silu(acc).astype(BF16)


def _ssd_conv(p, conv_w, conv_b, dims):
    t = p.shape[0]
    tm = TOK_TILE
    cb = 512
    col0 = XBC_OFF // cb
    r8 = tm // SUBLANES
    last8 = t // SUBLANES - 1
    kern = functools.partial(_conv_kernel, tm=tm, n_lat_tiles=dims["nl"] // tm, lat_per_seq=dims["seq"] // tm,
                             ctx_per_seq=dims["ctx"] // tm)
    return pl.pallas_call(
        kern,
        out_shape=jax.ShapeDtypeStruct((t, SSD_XBC), BF16),
        grid=(t // tm, SSD_XBC // cb),
        in_specs=[pl.BlockSpec((SUBLANES, cb), lambda i, j: (jnp.maximum(i * r8 - 1, 0), col0 + j)),
                  pl.BlockSpec((tm, cb), lambda i, j: (i, col0 + j)),
                  pl.BlockSpec((SUBLANES, cb), lambda i, j: (jnp.minimum((i + 1) * r8, last8), col0 + j)),
                  pl.BlockSpec((SSD_CONV, cb), lambda i, j: (0, j)),
                  pl.BlockSpec((1, cb), lambda i, j: (0, j))],
        out_specs=pl.BlockSpec((tm, cb), lambda i, j: (i, j)),
        compiler_params=_cparams("parallel", "parallel"),
        name="ssd_conv",
    )(p, p, p, conv_w, conv_b.reshape(1, SSD_XBC))


def _ssd_scan_kernel(*refs, direction, final, ncc, write_ctx):
    if final:
        xbc_ref, dd_ref, ddt_ref, yf_ref, z_ref, dskip_ref, norm_ref, o_ref, h_ref = refs
    else:
        xbc_ref, dd_ref, ddt_ref, o_ref, h_ref = refs
    j = pl.program_id(1)
    q = SSD_CHUNK
    gw = SSD_HPG * SSD_HEAD_DIM

    @pl.when(j == 0)
    def _():
        h_ref[...] = jnp.zeros_like(h_ref)

    xbc = xbc_ref[...]
    xs = xbc[:, :SSD_WIDTH].astype(F32)
    bm = xbc[:, SSD_WIDTH:SSD_WIDTH + SSD_GROUPS * SSD_STATE]
    cm = xbc[:, SSD_WIDTH + SSD_GROUPS * SSD_STATE:]
    dd = dd_ref[...]
    ddt = ddt_ref[...]

    ii = lax.broadcasted_iota(jnp.int32, (q, q), 0)
    jj = lax.broadcasted_iota(jnp.int32, (q, q), 1)
    if direction == 0:
        mask = jj <= ii
        last = q - 1
    else:
        mask = jj >= ii
        last = 0
    tri = mask.astype(F32)
    tri_t = (ii <= jj).astype(F32) if direction == 0 else (ii >= jj).astype(F32)
    ac = jnp.dot(tri, dd, precision=HIGHEST, preferred_element_type=F32)
    ac_t = jnp.dot(ddt, tri_t, precision=HIGHEST, preferred_element_type=F32)

    er = lax.broadcasted_iota(jnp.int32, (LANES, SSD_WIDTH), 0)
    ec = lax.broadcasted_iota(jnp.int32, (LANES, SSD_WIDTH), 1) // SSD_HEAD_DIM
    dt_col = direction * SSD_HEADS
    ac_col = 2 * SSD_HEADS + direction * SSD_HEADS
    dt_exp = jnp.dot(dd, (er == ec + dt_col).astype(F32), precision=HIGHEST, preferred_element_type=F32)
    ac_exp = jnp.dot(ac, (er == ec + ac_col).astype(F32), precision=HIGHEST, preferred_element_type=F32)
    ac_last = ac_exp[last:last + 1, :]
    eac = jnp.exp(ac_exp)
    dec_end = jnp.exp(ac_last - ac_exp)
    chunk_dec = jnp.exp(ac_last)
    xdt = xs * dt_exp
    xdt_b = xdt.astype(BF16)
    xde_b = (xdt * dec_end).astype(BF16)

    lane = lax.broadcasted_iota(jnp.int32, (q, LANES), 1)
    left = lane < SSD_HEAD_DIM
    zero_b = jnp.zeros((q, LANES), BF16)
    pieces = []
    for g in range(SSD_GROUPS):
        bg = bm[:, g * SSD_STATE:(g + 1) * SSD_STATE]
        cg = cm[:, g * SSD_STATE:(g + 1) * SSD_STATE]
        cb = lax.dot_general(cg, bg, (((1,), (1,)), ((), ())), preferred_element_type=F32)
        h_t = h_ref[g]
        y_off = jnp.dot(cg, h_t.astype(BF16), preferred_element_type=F32) * eac[:, g * gw:(g + 1) * gw]
        bg_t = bg.astype(F32).T.astype(BF16)
        s_t = jnp.dot(bg_t, xde_b[:, g * gw:(g + 1) * gw], preferred_element_type=F32)
        h_ref[g] = h_t * chunk_dec[:, g * gw:(g + 1) * gw] + s_t
        for kp in range(SSD_HPG // 2):
            k0 = g * SSD_HPG + 2 * kp
            ms = []
            for k in (k0, k0 + 1):
                c = ac_col + k
                seg = ac[:, c:c + 1] - ac_t[c:c + 1, :]
                ms.append((cb * jnp.exp(jnp.where(mask, seg, -1e30))).astype(BF16))
            xp = xdt_b[:, k0 * SSD_HEAD_DIM:k0 * SSD_HEAD_DIM + LANES]
            y_diag = (jnp.dot(ms[0], jnp.where(left, xp, zero_b), preferred_element_type=F32)
                      + jnp.dot(ms[1], jnp.where(left, zero_b, xp), preferred_element_type=F32))
            pieces.append(y_diag + y_off[:, kp * LANES:(kp + 1) * LANES])
    y = jnp.concatenate(pieces, axis=1)

    def emit():
        if final:
            yt = y + yf_ref[...] + dskip_ref[...] * xs
            yt = yt * _silu(z_ref[...].astype(F32))
            o_ref[...] = (_rms(yt) * norm_ref[...]).astype(BF16)
        else:
            o_ref[...] = y

    if write_ctx:
        emit()
    else:
        pl.when(j >= ncc)(emit)


def _ssd_scan(xbc, dd, ddt, direction, dims, with_ctx, final_args=None):
    batch, seq, ctx_len, nl = dims["batch"], dims["seq"], dims["ctx"], dims["nl"]
    q = SSD_CHUNK
    ncl, ncc = seq // q, ctx_len // q
    final = final_args is not None
    rows = nl + (batch * ctx_len if with_ctx else 0)

    def chunk(b, j):
        jc = j if direction == 0 else ncc - 1 - j
        jl = (j - ncc) if direction == 0 else ncl - 1 - (j - ncc)
        return jnp.where(j < ncc, nl // q + b * ncc + jc, b * ncl + jl)

    def out_chunk(b, j):
        if with_ctx:
            return chunk(b, j)
        return chunk(b, jnp.maximum(j, ncc))

    in_specs = [pl.BlockSpec((q, SSD_XBC), lambda b, j: (chunk(b, j), 0)),
                pl.BlockSpec((q, LANES), lambda b, j: (chunk(b, j), 0)),
                pl.BlockSpec((LANES, q), lambda b, j: (0, chunk(b, j)))]
    args = [xbc, dd, ddt]
    if final:
        yf, p, dskip, norm = final_args
        in_specs += [pl.BlockSpec((q, SSD_WIDTH), lambda b, j: (out_chunk(b, j), 0)),
                     pl.BlockSpec((q, SSD_WIDTH), lambda b, j: (chunk(b, j), Z_OFF // SSD_WIDTH)),
                     pl.BlockSpec((1, SSD_WIDTH), lambda b, j: (0, 0)),
                     pl.BlockSpec((1, SSD_WIDTH), lambda b, j: (0, 0))]
        args += [yf, p, dskip, norm]
    return pl.pallas_call(
        functools.partial(_ssd_scan_kernel, direction=direction, final=final, ncc=ncc, write_ctx=with_ctx),
        out_shape=jax.ShapeDtypeStruct((rows, SSD_WIDTH), BF16 if final else F32),
        grid=(batch, ncc + ncl),
        in_specs=in_specs,
        out_specs=pl.BlockSpec((q, SSD_WIDTH), lambda b, j: (out_chunk(b, j), 0)),
        scratch_shapes=[pltpu.VMEM((SSD_GROUPS, SSD_STATE, SSD_HPG * SSD_HEAD_DIM), F32)],
        compiler_params=_cparams("parallel", "arbitrary"),
        name="ssd_scan_bwd" if direction else "ssd_scan_fwd",
    )(*args)


def _s5_matrices(lam_re, lam_im, log_step, b_re, b_im, c_re, c_im, s5_d):
    tc = S5_CHUNK
    delta = jnp.exp(log_step.astype(F32))[..., None]
    lr = lam_re.astype(F32) * delta
    li = lam_im.astype(F32) * delta
    k = jnp.arange(tc + 1, dtype=F32)[:, None, None, None]
    mag = jnp.exp(k * lr[None])
    pw_re = mag * jnp.cos(k * li[None])
    pw_im = mag * jnp.sin(k * li[None])
    x = jnp.expm1(lr) * jnp.cos(li) - 2.0 * jnp.sin(0.5 * li) ** 2
    y = jnp.exp(lr) * jnp.sin(li)
    a, b = lam_re.astype(F32), lam_im.astype(F32)
    den = a * a + b * b
    co_re = (x * a + y * b) / den
    co_im = (y * a - x * b) / den
    bb_re = co_re[..., None] * b_re[None] - co_im[..., None] * b_im[None]
    bb_im = co_re[..., None] * b_im[None] + co_im[..., None] * b_re[None]

    def cmul(ar, ai, br, bi):
        return ar * br - ai * bi, ar * bi + ai * br

    w_re, w_im = cmul(pw_re[..., None], pw_im[..., None], bb_re[None], bb_im[None])
    taps = (jnp.einsum("gfp,kdgpe->kdgfe", c_re, w_re[:tc], precision=HIGHEST)
            - jnp.einsum("gfp,kdgpe->kdgfe", c_im, w_im[:tc], precision=HIGHEST))
    s_idx = jnp.arange(tc)[:, None]
    t_idx = jnp.arange(tc)[None, :]
    lag_f = jnp.clip(t_idx - s_idx, 0, tc - 1)
    lag_b = jnp.clip(s_idx - t_idx, 0, tc - 1)
    kf = jnp.where((t_idx >= s_idx)[..., None, None, None], taps[lag_f, 0], 0.0)
    kb = jnp.where((s_idx >= t_idx)[..., None, None, None], taps[lag_b, 1], 0.0)
    eye_e = jnp.eye(S5_GROUP, dtype=F32)
    skip = (s_idx == t_idx)[..., None, None, None] * (s5_d.reshape(S5_GROUPS, S5_GROUP)[None, None, :, :, None]
                                                      * eye_e[None, None, None])
    m = (kf + kb + skip).transpose(2, 0, 4, 1, 3).reshape(S5_GROUPS, tc * S5_GROUP, tc * S5_GROUP)

    bf_re = w_re[tc - 1 - jnp.arange(tc), 0]
    bf_im = w_im[tc - 1 - jnp.arange(tc), 0]
    bb2_re = w_re[jnp.arange(tc), 1]
    bb2_im = w_im[jnp.arange(tc), 1]
    bmat = jnp.stack([bf_re, bf_im, bb2_re, bb2_im], axis=0)
    bmat = bmat.transpose(2, 1, 4, 0, 3).reshape(S5_GROUPS, tc * S5_GROUP, 4, S5_STATE)

    def cout(pr, pi):
        o_re = c_re[None] * pr[:, :, None, :] - c_im[None] * pi[:, :, None, :]
        o_im = c_re[None] * pi[:, :, None, :] + c_im[None] * pr[:, :, None, :]
        return o_re, -o_im
    cf_re, cf_im = cout(pw_re[1:tc + 1, 0], pw_im[1:tc + 1, 0])
    cb_re, cb_im = cout(pw_re[tc - jnp.arange(tc), 1], pw_im[tc - jnp.arange(tc), 1])
    cmat = jnp.stack([cf_re, cf_im, cb_re, cb_im], axis=0)
    cmat = cmat.transpose(2, 0, 4, 1, 3).reshape(S5_GROUPS, 4, S5_STATE, tc * S5_GROUP)

    ge = tc * S5_GROUP
    zm = jnp.zeros((S5_PAIRS, ge, ge), F32)
    m_p = jnp.concatenate([jnp.concatenate([m[0::2], zm], axis=2), jnp.concatenate([zm, m[1::2]], axis=2)], axis=1)
    zb = jnp.zeros((S5_PAIRS, ge, 4, S5_STATE), F32)
    b_p = jnp.concatenate([jnp.concatenate([bmat[0::2], zb], axis=3), jnp.concatenate([zb, bmat[1::2]], axis=3)],
                          axis=1).reshape(S5_PAIRS, 2 * ge, 4 * 2 * S5_STATE)
    zc = jnp.zeros((S5_PAIRS, 4, S5_STATE, ge), F32)
    c_p = jnp.concatenate([jnp.concatenate([cmat[0::2], zc], axis=3), jnp.concatenate([zc, cmat[1::2]], axis=3)],
                          axis=2).reshape(S5_PAIRS, 4 * 2 * S5_STATE, 2 * ge)
    a16 = jnp.stack([pw_re[tc, 0], pw_im[tc, 0], pw_re[tc, 1], pw_im[tc, 1]], axis=0)
    a16 = a16.reshape(4, S5_PAIRS // S5_PAIRS_PER_STEP, S5_PAIRS_PER_STEP * 2 * S5_STATE).transpose(1, 0, 2)
    return m_p.astype(BF16), b_p.astype(BF16), c_p.astype(BF16), a16


def _gelu_tanh(x):
    return 0.5 * x * (1.0 + jnp.tanh(math.sqrt(2.0 / math.pi) * (x + 0.044715 * x * x * x)))


def _s5_kernel(u_ref, m_ref, b_ref, c_ref, a_ref, o_ref, s_ref, *, batch, ncl, ncc):
    npp = S5_PAIRS_PER_STEP
    sw = npp * LANES
    for jp in range(npp):
        s = jnp.dot(u_ref[jp], b_ref[jp], preferred_element_type=F32)
        for qn in range(4):
            s_ref[:, qn * sw + jp * LANES:qn * sw + (jp + 1) * LANES] = s[:, qn * LANES:(qn + 1) * LANES]

    a = a_ref[0]
    af_re, af_im, ab_re, ab_im = a[0:1], a[1:2], a[2:3], a[3:4]

    def advance(row, h, dir_off, ar, ai):
        h_re, h_im = h
        s_re = s_ref[pl.ds(row, 1), dir_off:dir_off + sw]
        s_im = s_ref[pl.ds(row, 1), dir_off + sw:dir_off + 2 * sw]
        s_ref[pl.ds(row, 1), dir_off:dir_off + sw] = h_re
        s_ref[pl.ds(row, 1), dir_off + sw:dir_off + 2 * sw] = h_im
        return ar * h_re - ai * h_im + s_re, ar * h_im + ai * h_re + s_im

    zero = jnp.zeros((1, sw), F32)
    for b in range(batch):
        ctx0 = batch * ncl + b * ncc
        lat0 = b * ncl

        def ctx_body(i, c):
            hf, hb = c
            return (advance(ctx0 + i, hf, 0, af_re, af_im), advance(ctx0 + ncc - 1 - i, hb, 2 * sw, ab_re, ab_im))

        def lat_body(i, c):
            hf, hb = c
            return (advance(lat0 + i, hf, 0, af_re, af_im), advance(lat0 + ncl - 1 - i, hb, 2 * sw, ab_re, ab_im))

        c = lax.fori_loop(0, ncc, ctx_body, ((zero, zero), (zero, zero)))
        lax.fori_loop(0, ncl, lat_body, c)

    for jp in range(npp):
        h = jnp.concatenate([s_ref[:, qn * sw + jp * LANES:qn * sw + (jp + 1) * LANES] for qn in range(4)], axis=1)
        y = (jnp.dot(u_ref[jp], m_ref[jp], preferred_element_type=F32)
             + jnp.dot(h.astype(BF16), c_ref[jp], preferred_element_type=F32))
        o_ref[jp] = _gelu_tanh(y).astype(BF16)


def _s5(u_r, mats, dims):
    m_p, b_p, c_p, a16 = mats
    npairs, nch, w = u_r.shape
    npp = S5_PAIRS_PER_STEP
    ncl, ncc = dims["seq"] // S5_CHUNK, dims["ctx"] // S5_CHUNK
    wspec = pl.BlockSpec((npp, w, w), lambda i: (i, 0, 0))
    return pl.pallas_call(
        functools.partial(_s5_kernel, batch=dims["batch"], ncl=ncl, ncc=ncc),
        out_shape=jax.ShapeDtypeStruct((npairs, nch, w), BF16),
        grid=(npairs // npp,),
        in_specs=[pl.BlockSpec((npp, nch, w), lambda i: (i, 0, 0)), wspec, wspec, wspec,
                  pl.BlockSpec((1, 4, npp * LANES), lambda i: (i, 0, 0))],
        out_specs=pl.BlockSpec((npp, nch, w), lambda i: (i, 0, 0)),
        scratch_shapes=[pltpu.VMEM((nch, 4 * npp * LANES), F32)],
        compiler_params=_cparams("parallel"),
        name="s5_scan",
    )(u_r, m_p, b_p, c_p, a16)


def _glu_kernel(x_ref, wa_ref, wb_ref, ba_ref, bb_ref, o_ref):
    x = x_ref[...]
    a = jnp.dot(x, wa_ref[...], preferred_element_type=F32) + ba_ref[...]
    b = jnp.dot(x, wb_ref[...], preferred_element_type=F32) + bb_ref[...]
    o_ref[...] = (a * _sigmoid(b)).astype(BF16)


def _glu(x, w, bias, tm):
    r, k = x.shape
    tn = 512
    nb = S5_WIDTH // tn
    return pl.pallas_call(
        _glu_kernel,
        out_shape=jax.ShapeDtypeStruct((r, S5_WIDTH), BF16),
        grid=(r // tm, nb),
        in_specs=[pl.BlockSpec((tm, k), lambda i, j: (i, 0)),
                  pl.BlockSpec((k, tn), lambda i, j: (0, j)),
                  pl.BlockSpec((k, tn), lambda i, j: (0, nb + j)),
                  pl.BlockSpec((1, tn), lambda i, j: (0, j)),
                  pl.BlockSpec((1, tn), lambda i, j: (0, nb + j))],
        out_specs=pl.BlockSpec((tm, tn), lambda i, j: (i, j)),
        compiler_params=_cparams("parallel", "parallel"),
        name="s5_glu",
    )(x, w, w, bias, bias)


def _merge_kernel(ya_ref, yb_ref, yc_ref, wa_ref, wb_ref, wc_ref, ga_ref, gb_ref, gc_ref, o_ref):
    acc = None
    for y_ref, w_ref, g_ref in ((ya_ref, wa_ref, ga_ref), (yb_ref, wb_ref, gb_ref), (yc_ref, wc_ref, gc_ref)):
        br = jnp.dot(y_ref[...], w_ref[0], preferred_element_type=F32)
        term = _sigmoid(g_ref[...].astype(F32)) * br
        acc = term if acc is None else acc + term
    o_ref[...] = acc.astype(BF16)


def _merge(ya, yb, yc, w_branch, p, tm):
    r, k = ya.shape
    d = w_branch.shape[2]
    tn = 512
    g0 = GATE_OFF // tn
    gstep = d // tn
    yspec = pl.BlockSpec((tm, k), lambda i, j: (i, 0))
    wspec = lambda n: pl.BlockSpec((1, k, tn), lambda i, j: (n, 0, j))
    gspec = lambda n: pl.BlockSpec((tm, tn), lambda i, j: (i, g0 + n * gstep + j))
    return pl.pallas_call(
        _merge_kernel,
        out_shape=jax.ShapeDtypeStruct((r, d), BF16),
        grid=(r // tm, d // tn),
        in_specs=[yspec, yspec, yspec, wspec(0), wspec(1), wspec(2), gspec(0), gspec(1), gspec(2)],
        out_specs=pl.BlockSpec((tm, tn), lambda i, j: (i, j)),
        compiler_params=_cparams("parallel", "parallel"),
        name="branch_merge",
    )(ya, yb, yc, w_branch, w_branch, w_branch, p, p, p)


def _out_proj_kernel(g_ref, w_ref, x_ref, mod_ref, npost_ref, npre_ref, xo_ref, ho_ref):
    o = jnp.dot(g_ref[...], w_ref[...], preferred_element_type=F32)
    gate = mod_ref[0, 2:3, :]
    xn = x_ref[...] + gate * (_rms(o) * npost_ref[...])
    xo_ref[...] = xn
    shift = mod_ref[0, 3:4, :]
    scale = mod_ref[0, 4:5, :]
    ho_ref[...] = (_rms(xn) * npre_ref[...] * (1.0 + scale) + shift).astype(BF16)


def _out_proj(g, w_out, x, mod, npost, npre, dims):
    r, d = g.shape
    tm = TOK_TILE
    seg = functools.partial(_seg_of_tile, tm=tm, n_lat=dims["nl"], seq=dims["seq"], batch=dims["batch"])
    row = pl.BlockSpec((tm, d), lambda i: (i, 0))
    vec = pl.BlockSpec((1, d), lambda i: (0, 0))
    return pl.pallas_call(
        _out_proj_kernel,
        out_shape=(jax.ShapeDtypeStruct((r, d), F32), jax.ShapeDtypeStruct((r, d), BF16)),
        grid=(r // tm,),
        in_specs=[row, pl.BlockSpec((d, d), lambda i: (0, 0)), row,
                  pl.BlockSpec((1, 6, d), lambda i: (seg(i), 0, 0)), vec, vec],
        out_specs=(row, row),
        compiler_params=_cparams("parallel"),
        name="out_proj",
    )(g, w_out, x, mod, npost.reshape(1, d), npre.reshape(1, d))


def _ffn_up_kernel(h_ref, wg_ref, wu_ref, o_ref):
    h = h_ref[...]
    a = jnp.dot(h, wg_ref[...], preferred_element_type=F32)
    b = jnp.dot(h, wu_ref[...], preferred_element_type=F32)
    o_ref[...] = (_silu(a) * b).astype(BF16)


def _ffn_up(h, wg, wu, tm):
    r, d = h.shape
    f = wg.shape[1]
    tn = 512
    wspec = pl.BlockSpec((d, tn), lambda i, j: (0, j))
    return pl.pallas_call(
        _ffn_up_kernel,
        out_shape=jax.ShapeDtypeStruct((r, f), BF16),
        grid=(r // tm, f // tn),
        in_specs=[pl.BlockSpec((tm, d), lambda i, j: (i, 0)), wspec, wspec],
        out_specs=pl.BlockSpec((tm, tn), lambda i, j: (i, j)),
        compiler_params=_cparams("parallel", "parallel"),
        name="ffn_up",
    )(h, wg, wu)


def _ffn_down_kernel(a_ref, w_ref, x_ref, mod_ref, npost_ref, o_ref, acc_ref):
    k = pl.program_id(1)

    @pl.when(k == 0)
    def _():
        acc_ref[...] = jnp.zeros_like(acc_ref)

    acc_ref[...] += jnp.dot(a_ref[...], w_ref[...], preferred_element_type=F32)

    @pl.when(k == pl.num_programs(1) - 1)
    def _():
        gate = mod_ref[0, 5:6, :]
        o_ref[...] = x_ref[...] + gate * (_rms(acc_ref[...]) * npost_ref[...])


def _ffn_down(act, wd, x, mod, npost, dims):
    r, f = act.shape
    d = wd.shape[1]
    tm = TOK_TILE * 2
    nk = 4
    tk = f // nk
    seg = functools.partial(_seg_of_tile, tm=tm, n_lat=dims["nl"], seq=dims["seq"], batch=dims["batch"])
    row = pl.BlockSpec((tm, d), lambda i, k: (i, 0))
    return pl.pallas_call(
        _ffn_down_kernel,
        out_shape=jax.ShapeDtypeStruct((r, d), F32),
        grid=(r // tm, nk),
        in_specs=[pl.BlockSpec((tm, tk), lambda i, k: (i, k)),
                  pl.BlockSpec((tk, d), lambda i, k: (k, 0)),
                  row,
                  pl.BlockSpec((1, 6, d), lambda i, k: (seg(i), 0, 0)),
                  pl.BlockSpec((1, d), lambda i, k: (0, 0))],
        out_specs=row,
        scratch_shapes=[pltpu.VMEM((tm, d), F32)],
        compiler_params=_cparams("parallel", "arbitrary"),
        name="ffn_down",
    )(act, wd, x, mod, npost.reshape(1, d))


def _pick_tile(rows, cands):
    for c in cands:
        if rows % c == 0:
            return c
    raise ValueError(f"no tile in {cands} divides {rows}")


def kernel(x, c, ctx, c_ctx, ada_w, ada_b, norm_mix_pre, norm_mix_post, norm_ffn_pre, norm_ffn_post, w_in, da_lambda, da_subln, ssd_conv_w, ssd_conv_b, ssd_dt_bias, ssd_a_log, ssd_d, ssd_norm, s5_lam_re, s5_lam_im, s5_log_step, s5_b_re, s5_b_im, s5_c_re, s5_c_im, s5_d, s5_glu_w, s5_glu_b, w_branch, w_out, ffn_w_gate, ffn_w_up, ffn_w_down):
    batch, seq, d = x.shape
    ctx_len = ctx.shape[1]
    depth = ada_w.shape[0]
    nl, nc = batch * seq, batch * ctx_len
    dims = dict(batch=batch, seq=seq, ctx=ctx_len, nl=nl, nc=nc)
    assert batch < MOD_ROWS and seq % TOK_TILE == 0 and ctx_len % TOK_TILE == 0 and seq % ctx_len == 0
    assert nc % (2 * TOK_TILE) == 0 and seq % GRID_W == 0

    xt = jnp.concatenate([x.reshape(nl, d), ctx.reshape(nc, d)], axis=0)
    cc = jnp.concatenate([c, c_ctx[None], jnp.zeros((MOD_ROWS - batch - 1, d), F32)], axis=0)
    mod_all = _ada(cc, ada_w, ada_b).reshape(depth, MOD_ROWS, 6, d)
    rope_tabs = _rope_tables(seq, ctx_len)
    nch = (nl + nc) // S5_CHUNK

    for l in range(depth):
        last = l == depth - 1
        with_ctx = not last
        lam_init = 0.8 - 0.6 * math.exp(-0.3 * l)
        mod = mod_all[l]
        rows = nl + nc if with_ctx else nl

        wl = w_in[l]
        w_main = jnp.concatenate([wl[:, :REF_DT_OFF], wl[:, REF_U_OFF:]], axis=1).astype(BF16)
        w_dt = wl[:, REF_DT_OFF:REF_U_OFF]
        n_dt = 2 * SSD_HEADS
        w_dt2 = jnp.concatenate([w_dt, w_dt, jnp.zeros((d, LANES - 2 * n_dt), F32)], axis=1).astype(BF16)
        bias = ssd_dt_bias[l].reshape(1, n_dt)
        bias2 = jnp.concatenate([bias, bias, jnp.zeros((1, LANES - 2 * n_dt), F32)], axis=1)
        alog = ssd_a_log[l].reshape(1, n_dt)
        alog2 = jnp.concatenate([alog, alog, jnp.zeros((1, LANES - 2 * n_dt), F32)], axis=1)

        h = _norm_mod(xt, norm_mix_pre[l], mod, 0, dims)
        p = _matmul(h, w_main, BF16, _pick_tile(nl + nc, (2176, 1088, 512, 256)), 512, "in_proj")

        qk = _rope(p, rope_tabs, dims)
        y_attn = _attention(qk, p, da_lambda[l], da_subln[l], lam_init, with_ctx, dims)

        dd, ddt = _ssd_dt(h, w_dt2, bias2, alog2)
        xbc = _ssd_conv(p, ssd_conv_w[l], ssd_conv_b[l], dims)
        y_f = _ssd_scan(xbc, dd, ddt, 0, dims, with_ctx)
        dskip = jnp.repeat(ssd_d[l], SSD_HEAD_DIM).reshape(1, SSD_WIDTH)
        y_ssd = _ssd_scan(xbc, dd, ddt, 1, dims, with_ctx,
                          final_args=(y_f, p, dskip, ssd_norm[l].reshape(1, SSD_WIDTH)))

        mats = _s5_matrices(s5_lam_re[l], s5_lam_im[l], s5_log_step[l], s5_b_re[l], s5_b_im[l],
                            s5_c_re[l], s5_c_im[l], s5_d[l])
        u = p[:, U_OFF:U_OFF + S5_WIDTH]
        u_r = (u.reshape(nch, S5_CHUNK, S5_PAIRS, 2, S5_GROUP).transpose(2, 0, 3, 1, 4)
               .reshape(S5_PAIRS, nch, 2 * S5_CHUNK * S5_GROUP))
        yg_r = _s5(u_r, mats, dims)
        yg = (yg_r.reshape(S5_PAIRS, nch, 2, S5_CHUNK, S5_GROUP).transpose(1, 3, 0, 2, 4)
              .reshape(nl + nc, S5_WIDTH))[:rows]
        tm = _pick_tile(rows, (1024, 512)) if not with_ctx else 2 * TOK_TILE
        y_s5 = _glu(yg, s5_glu_w[l].astype(BF16), s5_glu_b[l].reshape(1, 2 * S5_WIDTH), tm)

        g = _merge(y_attn, y_ssd, y_s5, w_branch[l].astype(BF16), p, tm)
        xt, h2 = _out_proj(g, w_out[l].astype(BF16), xt[:rows], mod, norm_mix_post[l], norm_ffn_pre[l], dims)
        act = _ffn_up(h2, ffn_w_gate[l].astype(BF16), ffn_w_up[l].astype(BF16), tm)
        xt = _ffn_down(act, ffn_w_down[l].astype(BF16), xt, mod, norm_ffn_post[l], dims)

    return xt[:nl].reshape(batch, seq, d)
```

```python
import functools
import math

import jax
import jax.numpy as jnp
from jax import lax
from jax.experimental import pallas as pl
from jax.experimental.pallas import tpu as pltpu

F32 = jnp.float32
BF16 = jnp.bfloat16
HIGHEST = lax.Precision.HIGHEST

GRID_W = 64
N_BRANCH = 3
DA_HEADS = 8
DA_HEAD_DIM = 64
DA_V_DIM = 2 * DA_HEAD_DIM
DA_WIDTH = DA_HEADS * DA_V_DIM
ROPE_THETA = 10000.0
SSD_HEADS = 16
SSD_HEAD_DIM = 64
SSD_GROUPS = 2
SSD_HPG = SSD_HEADS // SSD_GROUPS
SSD_STATE = 128
SSD_WIDTH = SSD_HEADS * SSD_HEAD_DIM
SSD_XBC = SSD_WIDTH + 2 * SSD_GROUPS * SSD_STATE
SSD_CONV = 5
SSD_CHUNK = 128
S5_GROUP = 16
S5_GROUPS = 64
S5_WIDTH = S5_GROUPS * S5_GROUP
S5_STATE = 64
S5_CHUNK = 16
S5_PAIRS = S5_GROUPS // 2
S5_PAIRS_PER_STEP = 4
RMS_EPS = 1e-6

REF_XBC_OFF = 4 * DA_WIDTH
REF_DT_OFF = REF_XBC_OFF + SSD_XBC
REF_U_OFF = REF_DT_OFF + 2 * SSD_HEADS
Q_OFF = 0
K_OFF = DA_WIDTH
V_OFF = 2 * DA_WIDTH
Z_OFF = 3 * DA_WIDTH
XBC_OFF = 4 * DA_WIDTH
U_OFF = XBC_OFF + SSD_XBC
GATE_OFF = U_OFF + S5_WIDTH

LANES = 128
SUBLANES = 8
VMEM_LIMIT_BYTES = 52 * 1024 * 1024
MOD_ROWS = 8

TOK_TILE = 256


def _cparams(*sem):
    return pltpu.CompilerParams(dimension_semantics=sem, vmem_limit_bytes=VMEM_LIMIT_BYTES)


def _rms(x):
    return x * lax.rsqrt(jnp.mean(x * x, axis=-1, keepdims=True) + RMS_EPS)


def _sigmoid(x):
    return 1.0 / (1.0 + jnp.exp(-x))


def _silu(x):
    return x * _sigmoid(x)


def _seg_of_tile(i, tm, n_lat, seq, batch):
    return jnp.where(i < n_lat // tm, i // (seq // tm), batch)


def _ada_kernel(c_ref, w_ref, b_ref, o_ref):
    c = c_ref[...]
    o_ref[0] = jnp.dot(_silu(c), w_ref[0], precision=HIGHEST, preferred_element_type=F32) + b_ref[0]


def _ada(cc, ada_w, ada_b):
    depth, d, n = ada_w.shape
    tn = 1024
    return pl.pallas_call(
        _ada_kernel,
        out_shape=jax.ShapeDtypeStruct((depth, MOD_ROWS, n), F32),
        grid=(depth, n // tn),
        in_specs=[pl.BlockSpec((MOD_ROWS, d), lambda l, j: (0, 0)),
                  pl.BlockSpec((1, d, tn), lambda l, j: (l, 0, j)),
                  pl.BlockSpec((1, 1, tn), lambda l, j: (l, 0, j))],
        out_specs=pl.BlockSpec((1, MOD_ROWS, tn), lambda l, j: (l, 0, j)),
        compiler_params=_cparams("parallel", "parallel"),
        name="ada_mod",
    )(cc, ada_w, ada_b.reshape(depth, 1, n))


def _norm_mod_kernel(x_ref, g_ref, mod_ref, o_ref, *, shift_idx):
    y = _rms(x_ref[...]) * g_ref[...]
    shift = mod_ref[0, shift_idx:shift_idx + 1, :]
    scale = mod_ref[0, shift_idx + 1:shift_idx + 2, :]
    o_ref[...] = (y * (1.0 + scale) + shift).astype(BF16)


def _norm_mod(x, g, mod, shift_idx, dims):
    t, d = x.shape
    tm = TOK_TILE
    seg = functools.partial(_seg_of_tile, tm=tm, n_lat=dims["nl"], seq=dims["seq"], batch=dims["batch"])
    return pl.pallas_call(
        functools.partial(_norm_mod_kernel, shift_idx=shift_idx),
        out_shape=jax.ShapeDtypeStruct((t, d), BF16),
        grid=(t // tm,),
        in_specs=[pl.BlockSpec((tm, d), lambda i: (i, 0)),
                  pl.BlockSpec((1, d), lambda i: (0, 0)),
                  pl.BlockSpec((1, 6, d), lambda i: (seg(i), 0, 0))],
        out_specs=pl.BlockSpec((tm, d), lambda i: (i, 0)),
        compiler_params=_cparams("parallel"),
        name="norm_mod",
    )(x, g.reshape(1, d), mod)


W_TILE = 512


def _shifted_cols(w0, w1, shift):
    width = w0.shape[1]
    lane = lax.broadcasted_iota(jnp.int32, w0.shape, 1)
    return jnp.where(lane < width - shift, pltpu.roll(w0, width - shift, 1), pltpu.roll(w1, width - shift, 1))


def _proj_kernel(*refs, shift):
    if shift:
        x_ref, w0_ref, w1_ref, o_ref, wb_ref = refs
    else:
        x_ref, w0_ref, o_ref, wb_ref = refs

    @pl.when(pl.program_id(1) == 0)
    def _():
        w = _shifted_cols(w0_ref[...], w1_ref[...], shift) if shift else w0_ref[...]
        wb_ref[...] = w.astype(BF16)

    o_ref[...] = jnp.dot(x_ref[...], wb_ref[...], preferred_element_type=F32).astype(o_ref.dtype)


def _proj(x, w, layer, col0, n_cols, out_dtype, tm, name):
    m, k = x.shape
    tn = W_TILE
    cb0, shift = col0 // tn, col0 % tn
    w_specs = [pl.BlockSpec((None, k, tn), lambda j, i: (layer, 0, cb0 + j))]
    if shift:
        w_specs.append(pl.BlockSpec((None, k, tn), lambda j, i: (layer, 0, cb0 + j + 1)))
    return pl.pallas_call(
        functools.partial(_proj_kernel, shift=shift),
        out_shape=jax.ShapeDtypeStruct((m, n_cols), out_dtype),
        grid=(n_cols // tn, m // tm),
        in_specs=[pl.BlockSpec((tm, k), lambda j, i: (i, 0))] + w_specs,
        out_specs=pl.BlockSpec((tm, tn), lambda j, i: (i, j)),
        scratch_shapes=[pltpu.VMEM((k, tn), BF16)],
        compiler_params=_cparams("parallel", "arbitrary"),
        name=name,
    )(x, *([w] * len(w_specs)))


def _cast_kernel(w_ref, o_ref):
    o_ref[...] = w_ref[...].astype(BF16)


def _cast_bf16(w, layer):
    _, r, n = w.shape
    tr = _pick_tile(r, (512, 256, 128, 8))
    return pl.pallas_call(
        _cast_kernel,
        out_shape=jax.ShapeDtypeStruct((r, n), BF16),
        grid=(r // tr,),
        in_specs=[pl.BlockSpec((None, tr, n), lambda i: (layer, i, 0))],
        out_specs=pl.BlockSpec((tr, n), lambda i: (i, 0)),
        compiler_params=_cparams("parallel"),
        name="cast_bf16",
    )(w)


def _rope_tables(seq, ctx_len):
    n_rows = seq // GRID_W
    row = jnp.repeat(jnp.arange(n_rows, dtype=F32), GRID_W)
    col = jnp.tile(jnp.arange(GRID_W, dtype=F32), n_rows)
    half = DA_HEAD_DIM // 2
    inv_freq = ROPE_THETA ** (-jnp.arange(0, half, 2, dtype=F32) / half)
    ar = row[:, None] * inv_freq[None, :]
    ac = col[:, None] * inv_freq[None, :]
    ang = jnp.concatenate([ar, ar, ac, ac], axis=-1)
    ang = jnp.concatenate([ang, jnp.zeros((ctx_len, DA_HEAD_DIM), F32)], axis=0)
    cos = jnp.tile(jnp.cos(ang), (1, 2))
    sin = jnp.tile(jnp.sin(ang), (1, 2))
    first = (jnp.arange(LANES) % half) < (half // 2)
    sin_a = jnp.where(first[None, :], -sin, 0.0)
    sin_b = jnp.where(first[None, :], 0.0, sin)
    return cos, sin_a, sin_b


def _rope_kernel(p_ref, cos_ref, sa_ref, sb_ref, o_ref):
    cos = cos_ref[...]
    sa = sa_ref[...]
    sb = sb_ref[...]
    quarter = DA_HEAD_DIM // 4
    for h in range(2 * DA_HEADS):
        sl = slice(h * LANES, (h + 1) * LANES)
        x = p_ref[:, sl].astype(F32)
        r = x * cos + pltpu.roll(x, LANES - quarter, 1) * sa + pltpu.roll(x, quarter, 1) * sb
        if h < DA_HEADS:
            r = r * (DA_HEAD_DIM ** -0.5)
        o_ref[:, sl] = r.astype(BF16)


def _rope(p, tables, dims):
    t = p.shape[0]
    tm = TOK_TILE
    nl, seq, ctx_len = dims["nl"], dims["seq"], dims["ctx"]

    def tab(i):
        return (jnp.where(i < nl // tm, i % (seq // tm), seq // tm + (i - nl // tm) % (ctx_len // tm)), 0)

    w = 2 * DA_WIDTH
    tspec = pl.BlockSpec((tm, LANES), tab)
    return pl.pallas_call(
        _rope_kernel,
        out_shape=jax.ShapeDtypeStruct((t, w), BF16),
        grid=(t // tm,),
        in_specs=[pl.BlockSpec((tm, w), lambda i: (i, 0)), tspec, tspec, tspec],
        out_specs=pl.BlockSpec((tm, w), lambda i: (i, 0)),
        compiler_params=_cparams("parallel"),
        name="rope_qk",
    )(p, *tables)


ATTN_ROW_BLOCK = 64


def _attn_kernel(*refs, tq, tk, n_lat_k, lam_init):
    if n_lat_k:
        lam_ref, subln_ref, q_ref, kc_ref, vc_ref, kl_ref, vl_ref, o_ref = refs[:8]
    else:
        lam_ref, subln_ref, q_ref, kc_ref, vc_ref = refs[:5]
        o_ref = refs[-8]
    qs_ref, s_ref, p_ref, m_ref, a_ref, acc_ref, vx_ref = refs[-7:]
    rb = ATTN_ROW_BLOCK
    n_ctx = kc_ref.shape[0]

    n_lat = n_lat_k * tk

    @pl.when(pl.program_id(2) == 0)
    def _():
        if n_lat_k:
            vx_ref[0:n_lat, 0:LANES] = vl_ref[...]
        vx_ref[n_lat:, 0:LANES] = vc_ref[...]
        vx_ref[:, LANES:2 * LANES] = jnp.ones((vx_ref.shape[0], LANES), BF16)

    q = q_ref[...]
    lane = lax.broadcasted_iota(jnp.int32, q.shape, 1)
    zero = jnp.zeros_like(q)
    qs_ref[0:tq, :] = jnp.where(lane < DA_HEAD_DIM, q, zero)
    qs_ref[tq:2 * tq, :] = jnp.where(lane >= DA_HEAD_DIM, q, zero)
    m_ref[...] = jnp.full(m_ref.shape, -1e30, F32)
    acc_ref[...] = jnp.zeros(acc_ref.shape, F32)

    def key_rows(t, nk):
        return pl.ds(t * tk if isinstance(t, int) else pl.multiple_of(t * tk, tk), nk)

    def keys(t):
        if isinstance(t, int) and t == n_lat_k:
            return kc_ref[...], n_ctx
        return kl_ref[key_rows(t, tk), :], tk

    def scores(t, buf):
        k, nk = keys(t)
        s_ref[buf, :, 0:nk] = lax.dot_general(qs_ref[...], k, (((1,), (1,)), ((), ())),
                                              preferred_element_type=F32)

    def softmax(buf, nk):
        reps = nk // LANES
        for r in range(2 * tq // rb):
            rows = slice(r * rb, (r + 1) * rb)
            m_prev = m_ref[rows, :]
            m_new = jnp.maximum(m_prev, jnp.max(s_ref[buf, rows, 0:nk], axis=-1, keepdims=True))
            a_ref[buf, rows, :] = jnp.exp(m_prev - m_new)
            m_ref[rows, :] = m_new
            p = jnp.exp(s_ref[buf, rows, 0:nk] - jnp.concatenate([m_new] * reps, axis=1))
            p_ref[buf, rows, 0:nk] = p.astype(BF16)

    def weighted(t, buf):
        nk = n_ctx if isinstance(t, int) and t == n_lat_k else tk
        vx = vx_ref[key_rows(t, nk), :]
        alpha = a_ref[buf]
        pv = jnp.dot(p_ref[buf, :, 0:nk], vx, preferred_element_type=F32)
        acc_ref[...] = jnp.concatenate([alpha, alpha], axis=1) * acc_ref[...] + pv

    def stage(t, buf):
        scores(t + 1, 1 - buf)
        softmax(buf, tk)
        weighted(t - 1, 1 - buf)

    n_static = n_lat_k + 1
    loop_pairs = (n_lat_k - 2) // 2 if (n_lat_k >= 4 and n_lat_k % 2 == 0) else 0
    scores(0, 0)
    t = 0
    while t < n_static:
        if t == 1 and loop_pairs:
            def body(i, carry):
                stage(1 + 2 * i, 1)
                stage(2 + 2 * i, 0)
                return carry
            lax.fori_loop(0, loop_pairs, body, 0)
            t += 2 * loop_pairs
            continue
        if t + 1 < n_static:
            scores(t + 1, (t + 1) % 2)
        softmax(t % 2, n_ctx if t == n_lat_k else tk)
        if t >= 1:
            weighted(t - 1, (t - 1) % 2)
        t += 1
    weighted(n_lat_k, n_lat_k % 2)

    lf = lam_ref[...]
    lam = (jnp.exp(jnp.sum(lf[0:1] * lf[1:2], axis=-1, keepdims=True))
           - jnp.exp(jnp.sum(lf[2:3] * lf[3:4], axis=-1, keepdims=True)) + lam_init)
    o = acc_ref[:, 0:LANES] / acc_ref[:, LANES:2 * LANES]
    out = o[:tq] - lam * o[tq:]
    y = _rms(out) * subln_ref[...] * (1.0 - lam_init)
    o_ref[...] = y.astype(BF16)


def _attn_call(qk, p, da_lambda, da_subln, lam_init, dims, *, tq, q_row0, n_q, rows, with_lat, prev=None):
    batch, seq, ctx_len, nl = dims["batch"], dims["seq"], dims["ctx"], dims["nl"]
    tk = 512
    q_map = lambda b, h, i: (q_row0 // tq + b * n_q + i, h)
    kv_lat = lambda off: pl.BlockSpec((seq, LANES), lambda b, h, i: (b, off + h))
    kv_ctx = lambda off: pl.BlockSpec((ctx_len, LANES), lambda b, h, i: (nl // ctx_len + b, off + h))
    in_specs = [pl.BlockSpec((4, DA_HEAD_DIM), lambda b, h, i: (0, 0)),
                pl.BlockSpec((1, DA_V_DIM), lambda b, h, i: (0, 0)),
                pl.BlockSpec((tq, LANES), q_map),
                kv_ctx(DA_HEADS), kv_ctx(V_OFF // LANES)]
    args = [da_lambda, da_subln.reshape(1, DA_V_DIM), qk, qk, p]
    if with_lat:
        in_specs += [kv_lat(DA_HEADS), kv_lat(V_OFF // LANES)]
        args += [qk, p]
    aliases = {}
    if prev is not None:
        in_specs.append(pl.BlockSpec(memory_space=pl.ANY))
        args.append(prev)
        aliases = {len(args) - 1: 0}
    n_keys = ctx_len + (seq if with_lat else 0)
    scratch = [pltpu.VMEM((2 * tq, LANES), BF16), pltpu.VMEM((2, 2 * tq, tk), F32), pltpu.VMEM((2, 2 * tq, tk), BF16),
               pltpu.VMEM((2 * tq, LANES), F32), pltpu.VMEM((2, 2 * tq, LANES), F32),
               pltpu.VMEM((2 * tq, 2 * LANES), F32), pltpu.VMEM((n_keys, 2 * LANES), BF16)]
    kern = functools.partial(_attn_kernel, tq=tq, tk=tk, n_lat_k=seq // tk if with_lat else 0, lam_init=lam_init)
    return pl.pallas_call(
        kern,
        out_shape=jax.ShapeDtypeStruct((rows, DA_WIDTH), BF16),
        grid=(batch, DA_HEADS, n_q),
        in_specs=in_specs,
        out_specs=pl.BlockSpec((tq, LANES), q_map),
        scratch_shapes=scratch,
        input_output_aliases=aliases,
        compiler_params=_cparams("parallel", "parallel", "arbitrary"),
        name="diff_attn" if with_lat else "diff_attn_ctx",
    )(*args)


def _attention(qk, p, da_lambda, da_subln, lam_init, with_ctx, dims):
    batch, seq, ctx_len, nl = dims["batch"], dims["seq"], dims["ctx"], dims["nl"]
    rows = nl + (batch * ctx_len if with_ctx else 0)
    tq = 512 if seq % 512 == 0 else TOK_TILE
    y = _attn_call(qk, p, da_lambda, da_subln, lam_init, dims, tq=tq, q_row0=0, n_q=seq // tq, rows=rows,
                   with_lat=True)
    if with_ctx:
        tqc = TOK_TILE
        y = _attn_call(qk, p, da_lambda, da_subln, lam_init, dims, tq=tqc, q_row0=nl, n_q=ctx_len // tqc,
                       rows=rows, with_lat=False, prev=y)
    return y


def _softplus(x):
    return jnp.maximum(x, 0.0) + jnp.log(1.0 + jnp.exp(-jnp.abs(x)))


def _ssd_dt_kernel(h_ref, w_ref, b_ref, alog_ref, dd_ref, ddt_ref):
    raw = jnp.dot(h_ref[...], w_ref[...].astype(BF16), preferred_element_type=F32)
    n = 2 * SSD_HEADS
    dt = _softplus(raw + b_ref[...])
    lane = lax.broadcasted_iota(jnp.int32, dt.shape, 1)
    dta = pltpu.roll(dt, n, 1) * (-jnp.exp(alog_ref[...]))
    dd = jnp.where(lane < n, dt, jnp.where(lane < 2 * n, dta, 0.0))
    dd_ref[...] = dd
    ddt_ref[...] = dd.T


def _ssd_dt(h, w_in, layer, bias2, alog2):
    t, d = h.shape
    tm = TOK_TILE
    assert REF_DT_OFF % LANES == 0
    return pl.pallas_call(
        _ssd_dt_kernel,
        out_shape=(jax.ShapeDtypeStruct((t, LANES), F32), jax.ShapeDtypeStruct((LANES, t), F32)),
        grid=(t // tm,),
        in_specs=[pl.BlockSpec((tm, d), lambda i: (i, 0)),
                  pl.BlockSpec((None, d, LANES), lambda i: (layer, 0, REF_DT_OFF // LANES)),
                  pl.BlockSpec((1, LANES), lambda i: (0, 0)),
                  pl.BlockSpec((1, LANES), lambda i: (0, 0))],
        out_specs=(pl.BlockSpec((tm, LANES), lambda i: (i, 0)), pl.BlockSpec((LANES, tm), lambda i: (0, i))),
        compiler_params=_cparams("parallel"),
        name="ssd_dt",
    )(h, w_in, bias2, alog2)


def _conv_kernel(prev_ref, x_ref, next_ref, w_ref, b_ref, o_ref, *, tm, n_lat_tiles, lat_per_seq, ctx_per_seq):
    i = pl.program_id(0)
    is_lat = i < n_lat_tiles
    pos = jnp.where(is_lat, i % lat_per_seq, (i - n_lat_tiles) % ctx_per_seq)
    per = jnp.where(is_lat, lat_per_seq, ctx_per_seq)
    keep_prev = (pos > 0).astype(F32)
    keep_next = (pos < per - 1).astype(F32)
    ext = jnp.concatenate([prev_ref[...].astype(F32) * keep_prev, x_ref[...].astype(F32),
                           next_ref[...].astype(F32) * keep_next], axis=0)
    n = tm + 2 * SUBLANES
    w = w_ref[...]
    acc = jnp.zeros((tm, ext.shape[1]), F32) + b_ref[...]
    for k in range(SSD_CONV):
        shift = (SSD_CONV // 2 - k) % n
        rolled = ext if shift == 0 else pltpu.roll(ext, shift, 0)
        acc = acc + rolled[SUBLANES:SUBLANES + tm] * w[k:k + 1, :]
    o_ref[...] = _silu(acc).astype(BF16)


def _ssd_conv(p, conv_w, conv_b, dims):
    t = p.shape[0]
    tm = TOK_TILE
    cb = 512
    col0 = XBC_OFF // cb
    r8 = tm // SUBLANES
    last8 = t // SUBLANES - 1
    kern = functools.partial(_conv_kernel, tm=tm, n_lat_tiles=dims["nl"] // tm, lat_per_seq=dims["seq"] // tm,
                             ctx_per_seq=dims["ctx"] // tm)
    return pl.pallas_call(
        kern,
        out_shape=jax.ShapeDtypeStruct((t, SSD_XBC), BF16),
        grid=(t // tm, SSD_XBC // cb),
        in_specs=[pl.BlockSpec((SUBLANES, cb), lambda i, j: (jnp.maximum(i * r8 - 1, 0), col0 + j)),
                  pl.BlockSpec((tm, cb), lambda i, j: (i, col0 + j)),
                  pl.BlockSpec((SUBLANES, cb), lambda i, j: (jnp.minimum((i + 1) * r8, last8), col0 + j)),
                  pl.BlockSpec((SSD_CONV, cb), lambda i, j: (0, j)),
                  pl.BlockSpec((1, cb), lambda i, j: (0, j))],
        out_specs=pl.BlockSpec((tm, cb), lambda i, j: (i, j)),
        compiler_params=_cparams("parallel", "parallel"),
        name="ssd_conv",
    )(p, p, p, conv_w, conv_b.reshape(1, SSD_XBC))


def _ssd_scan_kernel(*refs, direction, final, ncc, write_ctx):
    if final:
        xbc_ref, dd_ref, ddt_ref, yf_ref, z_ref, dskip_ref, norm_ref, o_ref, h_ref = refs
    else:
        xbc_ref, dd_ref, ddt_ref, o_ref, h_ref = refs
    j = pl.program_id(1)
    q = SSD_CHUNK
    gw = SSD_HPG * SSD_HEAD_DIM

    @pl.when(j == 0)
    def _():
        h_ref[...] = jnp.zeros_like(h_ref)

    xbc = xbc_ref[...]
    xs = xbc[:, :SSD_WIDTH].astype(F32)
    bm = xbc[:, SSD_WIDTH:SSD_WIDTH + SSD_GROUPS * SSD_STATE]
    cm = xbc[:, SSD_WIDTH + SSD_GROUPS * SSD_STATE:]
    dd = dd_ref[...]
    ddt = ddt_ref[...]

    ii = lax.broadcasted_iota(jnp.int32, (q, q), 0)
    jj = lax.broadcasted_iota(jnp.int32, (q, q), 1)
    if direction == 0:
        mask = jj <= ii
        last = q - 1
    else:
        mask = jj >= ii
        last = 0
    tri = mask.astype(F32)
    tri_t = (ii <= jj).astype(F32) if direction == 0 else (ii >= jj).astype(F32)
    ac = jnp.dot(tri, dd, precision=HIGHEST, preferred_element_type=F32)
    ac_t = jnp.dot(ddt, tri_t, precision=HIGHEST, preferred_element_type=F32)

    er = lax.broadcasted_iota(jnp.int32, (LANES, SSD_WIDTH), 0)
    ec = lax.broadcasted_iota(jnp.int32, (LANES, SSD_WIDTH), 1) // SSD_HEAD_DIM
    dt_col = direction * SSD_HEADS
    ac_col = 2 * SSD_HEADS + direction * SSD_HEADS
    dt_exp = jnp.dot(dd, (er == ec + dt_col).astype(F32), precision=HIGHEST, preferred_element_type=F32)
    ac_exp = jnp.dot(ac, (er == ec + ac_col).astype(F32), precision=HIGHEST, preferred_element_type=F32)
    ac_last = ac_exp[last:last + 1, :]
    eac = jnp.exp(ac_exp)
    dec_end = jnp.exp(ac_last - ac_exp)
    chunk_dec = jnp.exp(ac_last)
    xdt = xs * dt_exp
    xdt_b = xdt.astype(BF16)
    xde_b = (xdt * dec_end).astype(BF16)

    lane = lax.broadcasted_iota(jnp.int32, (q, LANES), 1)
    left = lane < SSD_HEAD_DIM
    zero_b = jnp.zeros((q, LANES), BF16)
    pieces = []
    for g in range(SSD_GROUPS):
        bg = bm[:, g * SSD_STATE:(g + 1) * SSD_STATE]
        cg = cm[:, g * SSD_STATE:(g + 1) * SSD_STATE]
        cb = lax.dot_general(cg, bg, (((1,), (1,)), ((), ())), preferred_element_type=F32)
        h_t = h_ref[g]
        y_off = jnp.dot(cg, h_t.astype(BF16), preferred_element_type=F32) * eac[:, g * gw:(g + 1) * gw]
        bg_t = bg.astype(F32).T.astype(BF16)
        s_t = jnp.dot(bg_t, xde_b[:, g * gw:(g + 1) * gw], preferred_element_type=F32)
        h_ref[g] = h_t * chunk_dec[:, g * gw:(g + 1) * gw] + s_t
        for kp in range(SSD_HPG // 2):
            k0 = g * SSD_HPG + 2 * kp
            ms = []
            for k in (k0, k0 + 1):
                c = ac_col + k
                seg = ac[:, c:c + 1] - ac_t[c:c + 1, :]
                ms.append((cb * jnp.exp(jnp.where(mask, seg, -1e30))).astype(BF16))
            xp = xdt_b[:, k0 * SSD_HEAD_DIM:k0 * SSD_HEAD_DIM + LANES]
            y_diag = (jnp.dot(ms[0], jnp.where(left, xp, zero_b), preferred_element_type=F32)
                      + jnp.dot(ms[1], jnp.where(left, zero_b, xp), preferred_element_type=F32))
            pieces.append(y_diag + y_off[:, kp * LANES:(kp + 1) * LANES])
    y = jnp.concatenate(pieces, axis=1)

    def emit():
        if final:
            yt = y + yf_ref[...] + dskip_ref[...] * xs
            yt = yt * _silu(z_ref[...].astype(F32))
            o_ref[...] = (_rms(yt) * norm_ref[...]).astype(BF16)
        else:
            o_ref[...] = y

    if write_ctx:
        emit()
    else:
        pl.when(j >= ncc)(emit)


def _ssd_scan(xbc, dd, ddt, direction, dims, with_ctx, final_args=None):
    batch, seq, ctx_len, nl = dims["batch"], dims["seq"], dims["ctx"], dims["nl"]
    q = SSD_CHUNK
    ncl, ncc = seq // q, ctx_len // q
    final = final_args is not None
    rows = nl + (batch * ctx_len if with_ctx else 0)

    def chunk(b, j):
        jc = j if direction == 0 else ncc - 1 - j
        jl = (j - ncc) if direction == 0 else ncl - 1 - (j - ncc)
        return jnp.where(j < ncc, nl // q + b * ncc + jc, b * ncl + jl)

    def out_chunk(b, j):
        if with_ctx:
            return chunk(b, j)
        return chunk(b, jnp.maximum(j, ncc))

    in_specs = [pl.BlockSpec((q, SSD_XBC), lambda b, j: (chunk(b, j), 0)),
                pl.BlockSpec((q, LANES), lambda b, j: (chunk(b, j), 0)),
                pl.BlockSpec((LANES, q), lambda b, j: (0, chunk(b, j)))]
    args = [xbc, dd, ddt]
    if final:
        yf, p, dskip, norm = final_args
        in_specs += [pl.BlockSpec((q, SSD_WIDTH), lambda b, j: (out_chunk(b, j), 0)),
                     pl.BlockSpec((q, SSD_WIDTH), lambda b, j: (chunk(b, j), Z_OFF // SSD_WIDTH)),
                     pl.BlockSpec((1, SSD_WIDTH), lambda b, j: (0, 0)),
                     pl.BlockSpec((1, SSD_WIDTH), lambda b, j: (0, 0))]
        args += [yf, p, dskip, norm]
    return pl.pallas_call(
        functools.partial(_ssd_scan_kernel, direction=direction, final=final, ncc=ncc, write_ctx=with_ctx),
        out_shape=jax.ShapeDtypeStruct((rows, SSD_WIDTH), BF16 if final else F32),
        grid=(batch, ncc + ncl),
        in_specs=in_specs,
        out_specs=pl.BlockSpec((q, SSD_WIDTH), lambda b, j: (out_chunk(b, j), 0)),
        scratch_shapes=[pltpu.VMEM((SSD_GROUPS, SSD_STATE, SSD_HPG * SSD_HEAD_DIM), F32)],
        compiler_params=_cparams("parallel", "arbitrary"),
        name="ssd_scan_bwd" if direction else "ssd_scan_fwd",
    )(*args)


S5_BLOCK_GROUPS = LANES // S5_GROUP
S5_BLOCKS = S5_GROUPS // S5_BLOCK_GROUPS
S5_BS = S5_BLOCK_GROUPS * S5_STATE
S5_LAG_ROWS = 24


def _expm1(x):
    poly = 1.0 + x / 10.0
    for n in range(9, 1, -1):
        poly = 1.0 + (x / n) * poly
    return jnp.where(jnp.abs(x) < 0.35, x * poly, jnp.exp(x) - 1.0)


def _s5_params(lam_re, lam_im, log_step, b_re, b_im, c_re, c_im, s5_d):
    nb, gl = S5_BLOCKS, S5_BLOCK_GROUPS
    rows = [lam_re[0], lam_im[0], lam_re[1], lam_im[1],
            jnp.repeat(log_step[0], S5_STATE), jnp.repeat(log_step[1], S5_STATE)]
    rows = [r.reshape(nb, S5_BS) for r in rows] + [jnp.zeros((nb, S5_BS), F32)] * 2
    lam_rows = jnp.stack(rows, axis=1)
    eye = jnp.eye(gl, dtype=F32)

    def bd_in(b):
        return jnp.einsum("jgpe,gh->jgehp", b.reshape(nb, gl, S5_STATE, S5_GROUP), eye).reshape(nb, LANES, S5_BS)

    def bd_out(c):
        return jnp.einsum("jgfp,gh->jgfhp", c.reshape(nb, gl, S5_GROUP, S5_STATE), eye).reshape(nb, LANES, S5_BS)

    b_bd = jnp.stack([bd_in(b_re), bd_in(b_im)], axis=1)
    ct_bd = jnp.stack([bd_out(c_re), bd_out(c_im)], axis=1)
    return lam_rows, b_bd, ct_bd, s5_d.reshape(nb, 1, LANES)


def _gelu_tanh(x):
    return 0.5 * x * (1.0 + jnp.tanh(math.sqrt(2.0 / math.pi) * (x + 0.044715 * x * x * x)))


def _s5_kernel(u_ref, lam_ref, b_ref, ct_ref, d_ref, o_ref,
               x_ref, w_ref, m_ref, s_ref, y_ref, taps_ref, pw_ref, bb_ref, *, batch, ncl, ncc):
    tc = S5_CHUNK
    nch = x_ref.shape[0]
    sw = S5_BS
    nt = (((1,), (1,)), ((), ()))

    for s in range(tc):
        x_ref[:, s * LANES:(s + 1) * LANES] = u_ref[pl.ds(s, nch, stride=tc), :].astype(BF16)

    rows = lam_ref[0]
    kk = lax.broadcasted_iota(jnp.int32, (S5_LAG_ROWS, sw), 0).astype(F32)
    b_re, b_im = b_ref[0, 0], b_ref[0, 1]
    for d in range(2):
        l_re, l_im = rows[2 * d:2 * d + 1], rows[2 * d + 1:2 * d + 2]
        delta = jnp.exp(rows[4 + d:5 + d])
        lr, li = l_re * delta, l_im * delta
        mag = jnp.exp(kk * lr)
        pw_ref[2 * d] = mag * jnp.cos(kk * li)
        pw_ref[2 * d + 1] = mag * jnp.sin(kk * li)
        xr = _expm1(lr) * jnp.cos(li) - 2.0 * jnp.sin(0.5 * li) ** 2
        xi = jnp.exp(lr) * jnp.sin(li)
        den = l_re * l_re + l_im * l_im
        co_re = (xr * l_re + xi * l_im) / den
        co_im = (xi * l_re - xr * l_im) / den
        bb_ref[2 * d] = co_re * b_re - co_im * b_im
        bb_ref[2 * d + 1] = co_re * b_im + co_im * b_re

    def power(d, lag):
        return pw_ref[2 * d, lag:lag + 1, :], pw_ref[2 * d + 1, lag:lag + 1, :]

    ct_re, ct_im = ct_ref[0, 0], ct_ref[0, 1]
    ct_re_b, ct_im_b = ct_re.astype(BF16), ct_im.astype(BF16)
    for d in range(2):
        tp = []
        for part in range(2):
            q = 2 * d + part
            for s in range(tc):
                pr, pi = power(d, tc - 1 - s if d == 0 else s)
                if part == 0:
                    tile = pr * bb_ref[2 * d] - pi * bb_ref[2 * d + 1]
                else:
                    tile = pr * bb_ref[2 * d + 1] + pi * bb_ref[2 * d]
                w_ref[q, s * LANES:(s + 1) * LANES, :] = tile.astype(BF16)
            w = w_ref[q]
            s_ref[:, q * sw:(q + 1) * sw] = jnp.dot(x_ref[...], w, preferred_element_type=F32)
            tp.append(lax.dot_general(w, ct_re_b if part == 0 else ct_im_b, nt, preferred_element_type=F32))
        taps_ref[d] = tp[0] - tp[1]

    ri = lax.broadcasted_iota(jnp.int32, (LANES, LANES), 0)
    ci = lax.broadcasted_iota(jnp.int32, (LANES, LANES), 1)
    skip = jnp.where(ri == ci, d_ref[0], 0.0)
    for s in range(tc):
        for t in range(tc):
            if t >= s:
                lag_tile = tc - 1 - (t - s)
                tile = taps_ref[0, lag_tile * LANES:(lag_tile + 1) * LANES, :]
                if t == s:
                    tile = tile + taps_ref[1, 0:LANES, :] + skip
            else:
                tile = taps_ref[1, (s - t) * LANES:(s - t + 1) * LANES, :]
            m_ref[s * LANES:(s + 1) * LANES, t * LANES:(t + 1) * LANES] = tile.astype(BF16)

    af_re, af_im = power(0, tc)
    ab_re, ab_im = power(1, tc)

    def advance(row, h, dir_off, ar, ai):
        h_re, h_im = h
        s_re = s_ref[pl.ds(row, 1), dir_off:dir_off + sw]
        s_im = s_ref[pl.ds(row, 1), dir_off + sw:dir_off + 2 * sw]
        s_ref[pl.ds(row, 1), dir_off:dir_off + sw] = h_re
        s_ref[pl.ds(row, 1), dir_off + sw:dir_off + 2 * sw] = h_im
        return ar * h_re - ai * h_im + s_re, ar * h_im + ai * h_re + s_im

    def sweep(first_chunk, n):
        def body(i, c):
            return tuple((advance(first_chunk(b) + i, c[b][0], 0, af_re, af_im),
                          advance(first_chunk(b) + n - 1 - i, c[b][1], 2 * sw, ab_re, ab_im)) for b in range(batch))
        return body

    zero = jnp.zeros((1, sw), F32)
    c = lax.fori_loop(0, ncc, sweep(lambda b: batch * ncl + b * ncc, ncc), (((zero, zero), (zero, zero)),) * batch)
    lax.fori_loop(0, ncl, sweep(lambda b: b * ncl, ncl), c)

    y = jnp.dot(x_ref[...], m_ref[...], preferred_element_type=F32)
    for q in range(4):
        d, part = divmod(q, 2)
        for t in range(tc):
            pr, pi = power(d, t + 1 if d == 0 else tc - t)
            tile = ct_re * pr - ct_im * pi if part == 0 else -(ct_re * pi + ct_im * pr)
            w_ref[q, t * LANES:(t + 1) * LANES, :] = tile.astype(BF16)
        h = s_ref[:, q * sw:(q + 1) * sw].astype(BF16)
        y = y + lax.dot_general(h, w_ref[q], nt, preferred_element_type=F32)
    y_ref[...] = y
    for t in range(tc):
        o_ref[pl.ds(t, nch, stride=tc), :] = _gelu_tanh(y_ref[:, t * LANES:(t + 1) * LANES])


def _s5(u, params, dims):
    lam_rows, b_bd, ct_bd, d_skip = params
    t = u.shape[0]
    nch = t // S5_CHUNK
    ncl, ncc = dims["seq"] // S5_CHUNK, dims["ctx"] // S5_CHUNK
    width = S5_CHUNK * LANES
    pspec = pl.BlockSpec((1, 2, LANES, S5_BS), lambda i: (i, 0, 0, 0))
    return pl.pallas_call(
        functools.partial(_s5_kernel, batch=dims["batch"], ncl=ncl, ncc=ncc),
        out_shape=jax.ShapeDtypeStruct((t, S5_WIDTH), F32),
        grid=(S5_BLOCKS,),
        in_specs=[pl.BlockSpec((t, LANES), lambda i: (0, i)),
                  pl.BlockSpec((1, SUBLANES, S5_BS), lambda i: (i, 0, 0)), pspec, pspec,
                  pl.BlockSpec((1, 1, LANES), lambda i: (i, 0, 0))],
        out_specs=pl.BlockSpec((t, LANES), lambda i: (0, i)),
        scratch_shapes=[pltpu.VMEM((nch, width), BF16),
                        pltpu.VMEM((4, width, S5_BS), BF16),
                        pltpu.VMEM((width, width), BF16),
                        pltpu.VMEM((nch, 4 * S5_BS), F32),
                        pltpu.VMEM((nch, width), F32),
                        pltpu.VMEM((2, width, LANES), F32),
                        pltpu.VMEM((4, S5_LAG_ROWS, S5_BS), F32),
                        pltpu.VMEM((4, LANES, S5_BS), F32)],
        compiler_params=_cparams("parallel"),
        name="s5_scan",
    )(u, lam_rows, b_bd, ct_bd, d_skip)


def _glu_kernel(x_ref, wa_ref, wb_ref, ba_ref, bb_ref, o_ref, wab_ref, wbb_ref):
    @pl.when(pl.program_id(1) == 0)
    def _():
        wab_ref[...] = wa_ref[...].astype(BF16)
        wbb_ref[...] = wb_ref[...].astype(BF16)

    x = x_ref[...].astype(BF16)
    a = jnp.dot(x, wab_ref[...], preferred_element_type=F32) + ba_ref[...]
    b = jnp.dot(x, wbb_ref[...], preferred_element_type=F32) + bb_ref[...]
    o_ref[...] = (a * _sigmoid(b)).astype(BF16)


def _glu(x, w, bias, layer, rows, tm):
    k = x.shape[1]
    tn = W_TILE
    nb = S5_WIDTH // tn
    return pl.pallas_call(
        _glu_kernel,
        out_shape=jax.ShapeDtypeStruct((rows, S5_WIDTH), BF16),
        grid=(nb, rows // tm),
        in_specs=[pl.BlockSpec((tm, k), lambda j, i: (i, 0)),
                  pl.BlockSpec((None, k, tn), lambda j, i: (layer, 0, j)),
                  pl.BlockSpec((None, k, tn), lambda j, i: (layer, 0, nb + j)),
                  pl.BlockSpec((None, 1, tn), lambda j, i: (layer, 0, j)),
                  pl.BlockSpec((None, 1, tn), lambda j, i: (layer, 0, nb + j))],
        out_specs=pl.BlockSpec((tm, tn), lambda j, i: (i, j)),
        scratch_shapes=[pltpu.VMEM((k, tn), BF16), pltpu.VMEM((k, tn), BF16)],
        compiler_params=_cparams("parallel", "arbitrary"),
        name="s5_glu",
    )(x, w, w, bias, bias)


def _merge_kernel(ya_ref, yb_ref, yc_ref, wa_ref, wb_ref, wc_ref, ga_ref, gb_ref, gc_ref, o_ref, wbf_ref):
    @pl.when(pl.program_id(1) == 0)
    def _():
        for n, w_ref in enumerate((wa_ref, wb_ref, wc_ref)):
            wbf_ref[n] = w_ref[0].astype(BF16)

    acc = None
    for n, (y_ref, g_ref) in enumerate(((ya_ref, ga_ref), (yb_ref, gb_ref), (yc_ref, gc_ref))):
        br = jnp.dot(y_ref[...], wbf_ref[n], preferred_element_type=F32)
        term = _sigmoid(g_ref[...].astype(F32)) * br
        acc = term if acc is None else acc + term
    o_ref[...] = acc.astype(BF16)


def _merge(ya, yb, yc, w_branch, layer, gates, rows, tm):
    k = ya.shape[1]
    d = w_branch.shape[3]
    tn = W_TILE
    gstep = d // tn
    yspec = pl.BlockSpec((tm, k), lambda j, i: (i, 0))
    wspec = lambda n: pl.BlockSpec((None, 1, k, tn), lambda j, i: (layer, n, 0, j))
    gspec = lambda n: pl.BlockSpec((tm, tn), lambda j, i: (i, n * gstep + j))
    return pl.pallas_call(
        _merge_kernel,
        out_shape=jax.ShapeDtypeStruct((rows, d), BF16),
        grid=(d // tn, rows // tm),
        in_specs=[yspec, yspec, yspec, wspec(0), wspec(1), wspec(2), gspec(0), gspec(1), gspec(2)],
        out_specs=pl.BlockSpec((tm, tn), lambda j, i: (i, j)),
        scratch_shapes=[pltpu.VMEM((N_BRANCH, k, tn), BF16)],
        compiler_params=_cparams("parallel", "arbitrary"),
        name="branch_merge",
    )(ya, yb, yc, w_branch, w_branch, w_branch, gates, gates, gates)


def _out_proj_kernel(g_ref, w_ref, x_ref, mod_ref, npost_ref, npre_ref, xo_ref, ho_ref):
    o = jnp.dot(g_ref[...], w_ref[...], preferred_element_type=F32)
    gate = mod_ref[0, 2:3, :]
    xn = x_ref[...] + gate * (_rms(o) * npost_ref[...])
    xo_ref[...] = xn
    shift = mod_ref[0, 3:4, :]
    scale = mod_ref[0, 4:5, :]
    ho_ref[...] = (_rms(xn) * npre_ref[...] * (1.0 + scale) + shift).astype(BF16)


def _out_proj(g, w_out, x, mod, npost, npre, dims):
    r, d = g.shape
    tm = TOK_TILE
    seg = functools.partial(_seg_of_tile, tm=tm, n_lat=dims["nl"], seq=dims["seq"], batch=dims["batch"])
    row = pl.BlockSpec((tm, d), lambda i: (i, 0))
    vec = pl.BlockSpec((1, d), lambda i: (0, 0))
    return pl.pallas_call(
        _out_proj_kernel,
        out_shape=(jax.ShapeDtypeStruct((r, d), F32), jax.ShapeDtypeStruct((r, d), BF16)),
        grid=(r // tm,),
        in_specs=[row, pl.BlockSpec((d, d), lambda i: (0, 0)), row,
                  pl.BlockSpec((1, 6, d), lambda i: (seg(i), 0, 0)), vec, vec],
        out_specs=(row, row),
        compiler_params=_cparams("parallel"),
        name="out_proj",
    )(g, w_out, x, mod, npost.reshape(1, d), npre.reshape(1, d))


def _ffn_up_kernel(h_ref, wg_ref, wu_ref, o_ref, wgb_ref, wub_ref):
    @pl.when(pl.program_id(1) == 0)
    def _():
        wgb_ref[...] = wg_ref[...].astype(BF16)
        wub_ref[...] = wu_ref[...].astype(BF16)

    h = h_ref[...]
    a = jnp.dot(h, wgb_ref[...], preferred_element_type=F32)
    b = jnp.dot(h, wub_ref[...], preferred_element_type=F32)
    o_ref[...] = (_silu(a) * b).astype(BF16)


def _ffn_up(h, wg, wu, layer, tm):
    r, d = h.shape
    f = wg.shape[2]
    tn = W_TILE
    wspec = pl.BlockSpec((None, d, tn), lambda j, i: (layer, 0, j))
    return pl.pallas_call(
        _ffn_up_kernel,
        out_shape=jax.ShapeDtypeStruct((r, f), BF16),
        grid=(f // tn, r // tm),
        in_specs=[pl.BlockSpec((tm, d), lambda j, i: (i, 0)), wspec, wspec],
        out_specs=pl.BlockSpec((tm, tn), lambda j, i: (i, j)),
        scratch_shapes=[pltpu.VMEM((d, tn), BF16), pltpu.VMEM((d, tn), BF16)],
        compiler_params=_cparams("parallel", "arbitrary"),
        name="ffn_up",
    )(h, wg, wu)


def _ffn_down_kernel(a_ref, w_ref, x_ref, mod_ref, npost_ref, o_ref, acc_ref):
    k = pl.program_id(1)

    @pl.when(k == 0)
    def _():
        acc_ref[...] = jnp.zeros_like(acc_ref)

    acc_ref[...] += jnp.dot(a_ref[...], w_ref[...], preferred_element_type=F32)

    @pl.when(k == pl.num_programs(1) - 1)
    def _():
        gate = mod_ref[0, 5:6, :]
        o_ref[...] = x_ref[...] + gate * (_rms(acc_ref[...]) * npost_ref[...])


def _ffn_down(act, wd, x, mod, npost, dims):
    r, f = act.shape
    d = wd.shape[1]
    tm = TOK_TILE * 2
    nk = 4
    tk = f // nk
    seg = functools.partial(_seg_of_tile, tm=tm, n_lat=dims["nl"], seq=dims["seq"], batch=dims["batch"])
    row = pl.BlockSpec((tm, d), lambda i, k: (i, 0))
    return pl.pallas_call(
        _ffn_down_kernel,
        out_shape=jax.ShapeDtypeStruct((r, d), F32),
        grid=(r // tm, nk),
        in_specs=[pl.BlockSpec((tm, tk), lambda i, k: (i, k)),
                  pl.BlockSpec((tk, d), lambda i, k: (k, 0)),
                  row,
                  pl.BlockSpec((1, 6, d), lambda i, k: (seg(i), 0, 0)),
                  pl.BlockSpec((1, d), lambda i, k: (0, 0))],
        out_specs=row,
        scratch_shapes=[pltpu.VMEM((tm, d), F32)],
        compiler_params=_cparams("parallel", "arbitrary"),
        name="ffn_down",
    )(act, wd, x, mod, npost.reshape(1, d))


def _pick_tile(rows, cands):
    for c in cands:
        if rows % c == 0:
            return c
    raise ValueError(f"no tile in {cands} divides {rows}")


def kernel(x, c, ctx, c_ctx, ada_w, ada_b, norm_mix_pre, norm_mix_post, norm_ffn_pre, norm_ffn_post, w_in, da_lambda, da_subln, ssd_conv_w, ssd_conv_b, ssd_dt_bias, ssd_a_log, ssd_d, ssd_norm, s5_lam_re, s5_lam_im, s5_log_step, s5_b_re, s5_b_im, s5_c_re, s5_c_im, s5_d, s5_glu_w, s5_glu_b, w_branch, w_out, ffn_w_gate, ffn_w_up, ffn_w_down):
    batch, seq, d = x.shape
    ctx_len = ctx.shape[1]
    depth = ada_w.shape[0]
    nl, nc = batch * seq, batch * ctx_len
    dims = dict(batch=batch, seq=seq, ctx=ctx_len, nl=nl, nc=nc)
    assert batch < MOD_ROWS and seq % TOK_TILE == 0 and ctx_len % TOK_TILE == 0 and seq % ctx_len == 0
    assert nc % (2 * TOK_TILE) == 0 and seq % GRID_W == 0

    xt = jnp.concatenate([x.reshape(nl, d), ctx.reshape(nc, d)], axis=0)
    cc = jnp.concatenate([c, c_ctx[None], jnp.zeros((MOD_ROWS - batch - 1, d), F32)], axis=0)
    mod_all = _ada(cc, ada_w, ada_b).reshape(depth, MOD_ROWS, 6, d)
    rope_tabs = _rope_tables(seq, ctx_len)
    n_dt = 2 * SSD_HEADS
    pad = lambda v, before: jnp.concatenate(
        [jnp.zeros((1, before), F32), v.reshape(1, n_dt), jnp.zeros((1, LANES - n_dt - before), F32)], axis=1)
    tm_all = _pick_tile(nl + nc, (2176, 1088, 512, 256))

    for l in range(depth):
        last = l == depth - 1
        with_ctx = not last
        lam_init = 0.8 - 0.6 * math.exp(-0.3 * l)
        mod = mod_all[l]
        rows = nl + nc if with_ctx else nl
        tm = _pick_tile(rows, (1088, 1024, 512))

        h = _norm_mod(xt, norm_mix_pre[l], mod, 0, dims)
        p = _proj(h, w_in, l, 0, REF_DT_OFF, BF16, tm_all, "in_proj_main")
        u = _proj(h, w_in, l, REF_U_OFF, S5_WIDTH, F32, tm_all, "in_proj_u")
        gates = _proj(h, w_in, l, REF_U_OFF + S5_WIDTH, N_BRANCH * d, BF16, tm_all, "in_proj_gates")

        qk = _rope(p, rope_tabs, dims)
        y_attn = _attention(qk, p, da_lambda[l], da_subln[l], lam_init, with_ctx, dims)

        dd, ddt = _ssd_dt(h, w_in, l, pad(ssd_dt_bias[l], 0), pad(ssd_a_log[l], n_dt))
        xbc = _ssd_conv(p, ssd_conv_w[l], ssd_conv_b[l], dims)
        y_f = _ssd_scan(xbc, dd, ddt, 0, dims, with_ctx)
        dskip = jnp.repeat(ssd_d[l], SSD_HEAD_DIM).reshape(1, SSD_WIDTH)
        y_ssd = _ssd_scan(xbc, dd, ddt, 1, dims, with_ctx,
                          final_args=(y_f, p, dskip, ssd_norm[l].reshape(1, SSD_WIDTH)))

        s5p = _s5_params(s5_lam_re[l], s5_lam_im[l], s5_log_step[l], s5_b_re[l], s5_b_im[l],
                         s5_c_re[l], s5_c_im[l], s5_d[l])
        yg = _s5(u, s5p, dims)
        y_s5 = _glu(yg, s5_glu_w, s5_glu_b.reshape(depth, 1, 2 * S5_WIDTH), l, rows, tm)

        g = _merge(y_attn, y_ssd, y_s5, w_branch, l, gates, rows, tm)
        xt, h2 = _out_proj(g, _cast_bf16(w_out, l), xt, mod, norm_mix_post[l], norm_ffn_pre[l], dims)
        act = _ffn_up(h2, ffn_w_gate, ffn_w_up, l, tm)
        xt = _ffn_down(act, _cast_bf16(ffn_w_down, l), xt, mod, norm_ffn_post[l], dims)

    return xt[:nl].reshape(batch, seq, d)
```

```python
import functools
import math

import jax
import jax.numpy as jnp
from jax import lax
from jax.experimental import pallas as pl
from jax.experimental.pallas import tpu as pltpu

F32 = jnp.float32
BF16 = jnp.bfloat16
HIGHEST = lax.Precision.HIGHEST

GRID_W = 64
N_BRANCH = 3
DA_HEADS = 8
DA_HEAD_DIM = 64
DA_V_DIM = 2 * DA_HEAD_DIM
DA_WIDTH = DA_HEADS * DA_V_DIM
ROPE_THETA = 10000.0
SSD_HEADS = 16
SSD_HEAD_DIM = 64
SSD_GROUPS = 2
SSD_HPG = SSD_HEADS // SSD_GROUPS
SSD_STATE = 128
SSD_WIDTH = SSD_HEADS * SSD_HEAD_DIM
SSD_XBC = SSD_WIDTH + 2 * SSD_GROUPS * SSD_STATE
SSD_CONV = 5
SSD_CHUNK = 128
S5_GROUP = 16
S5_GROUPS = 64
S5_WIDTH = S5_GROUPS * S5_GROUP
S5_STATE = 64
S5_CHUNK = 16
S5_PAIRS = S5_GROUPS // 2
S5_PAIRS_PER_STEP = 4
RMS_EPS = 1e-6

REF_XBC_OFF = 4 * DA_WIDTH
REF_DT_OFF = REF_XBC_OFF + SSD_XBC
REF_U_OFF = REF_DT_OFF + 2 * SSD_HEADS
Q_OFF = 0
K_OFF = DA_WIDTH
V_OFF = 2 * DA_WIDTH
Z_OFF = 3 * DA_WIDTH
XBC_OFF = 4 * DA_WIDTH
U_OFF = XBC_OFF + SSD_XBC
GATE_OFF = U_OFF + S5_WIDTH

LANES = 128
SUBLANES = 8
VMEM_LIMIT_BYTES = 52 * 1024 * 1024
MOD_ROWS = 8

TOK_TILE = 256


def _cparams(*sem):
    return pltpu.CompilerParams(dimension_semantics=sem, vmem_limit_bytes=VMEM_LIMIT_BYTES)


def _rms(x):
    return x * lax.rsqrt(jnp.mean(x * x, axis=-1, keepdims=True) + RMS_EPS)


def _sigmoid(x):
    return 1.0 / (1.0 + jnp.exp(-x))


def _silu(x):
    return x * _sigmoid(x)


def _seg_of_tile(i, tm, n_lat, seq, batch):
    return jnp.where(i < n_lat // tm, i // (seq // tm), batch)


def _ada_kernel(c_ref, w_ref, b_ref, o_ref):
    c = c_ref[...]
    o_ref[0] = jnp.dot(_silu(c), w_ref[0], precision=HIGHEST, preferred_element_type=F32) + b_ref[0]


def _ada(cc, ada_w, ada_b):
    depth, d, n = ada_w.shape
    tn = 1024
    return pl.pallas_call(
        _ada_kernel,
        out_shape=jax.ShapeDtypeStruct((depth, MOD_ROWS, n), F32),
        grid=(depth, n // tn),
        in_specs=[pl.BlockSpec((MOD_ROWS, d), lambda l, j: (0, 0)),
                  pl.BlockSpec((1, d, tn), lambda l, j: (l, 0, j)),
                  pl.BlockSpec((1, 1, tn), lambda l, j: (l, 0, j))],
        out_specs=pl.BlockSpec((1, MOD_ROWS, tn), lambda l, j: (l, 0, j)),
        compiler_params=_cparams("parallel", "parallel"),
        name="ada_mod",
    )(cc, ada_w, ada_b.reshape(depth, 1, n))


def _stream_specs(xs, tm, d, nl):
    n_lat = nl // tm
    if len(xs) == 1:
        return [pl.BlockSpec((tm, d), lambda i: (i, 0))]
    return [pl.BlockSpec((tm, d), lambda i: (jnp.minimum(i, n_lat - 1), 0)),
            pl.BlockSpec((tm, d), lambda i: (jnp.maximum(i - n_lat, 0), 0))]


def _stream_tile(x_refs, n_lat):
    if len(x_refs) == 1:
        return x_refs[0][...]
    return jnp.where(pl.program_id(0) < n_lat, x_refs[0][...], x_refs[1][...])


def _norm_mod_kernel(*refs, shift_idx, n_lat):
    *x_refs, g_ref, mod_ref, o_ref = refs
    y = _rms(_stream_tile(x_refs, n_lat)) * g_ref[...]
    shift = mod_ref[0, shift_idx:shift_idx + 1, :]
    scale = mod_ref[0, shift_idx + 1:shift_idx + 2, :]
    o_ref[...] = (y * (1.0 + scale) + shift).astype(BF16)


def _norm_mod(xs, g, mod, shift_idx, dims):
    d = xs[0].shape[1]
    t = dims["nl"] + dims["nc"]
    tm = TOK_TILE
    seg = functools.partial(_seg_of_tile, tm=tm, n_lat=dims["nl"], seq=dims["seq"], batch=dims["batch"])
    return pl.pallas_call(
        functools.partial(_norm_mod_kernel, shift_idx=shift_idx, n_lat=dims["nl"] // tm),
        out_shape=jax.ShapeDtypeStruct((t, d), BF16),
        grid=(t // tm,),
        in_specs=_stream_specs(xs, tm, d, dims["nl"]) + [
            pl.BlockSpec((1, d), lambda i: (0, 0)),
            pl.BlockSpec((1, 6, d), lambda i: (seg(i), 0, 0))],
        out_specs=pl.BlockSpec((tm, d), lambda i: (i, 0)),
        compiler_params=_cparams("parallel"),
        name="norm_mod",
    )(*xs, g.reshape(1, d), mod)


W_TILE = 512


def _shifted_cols(w0, w1, shift):
    width = w0.shape[1]
    lane = lax.broadcasted_iota(jnp.int32, w0.shape, 1)
    return jnp.where(lane < width - shift, pltpu.roll(w0, width - shift, 1), pltpu.roll(w1, width - shift, 1))


def _proj_kernel(*refs, shift):
    if shift:
        x_ref, w0_ref, w1_ref, o_ref, wb_ref = refs
    else:
        x_ref, w0_ref, o_ref, wb_ref = refs

    @pl.when(pl.program_id(1) == 0)
    def _():
        w = _shifted_cols(w0_ref[...], w1_ref[...], shift) if shift else w0_ref[...]
        wb_ref[...] = w.astype(BF16)

    o_ref[...] = jnp.dot(x_ref[...], wb_ref[...], preferred_element_type=F32).astype(o_ref.dtype)


def _proj(x, w, layer, col0, n_cols, out_dtype, tm, name):
    m, k = x.shape
    tn = W_TILE
    cb0, shift = col0 // tn, col0 % tn
    w_specs = [pl.BlockSpec((None, k, tn), lambda j, i: (layer, 0, cb0 + j))]
    if shift:
        w_specs.append(pl.BlockSpec((None, k, tn), lambda j, i: (layer, 0, cb0 + j + 1)))
    return pl.pallas_call(
        functools.partial(_proj_kernel, shift=shift),
        out_shape=jax.ShapeDtypeStruct((m, n_cols), out_dtype),
        grid=(n_cols // tn, m // tm),
        in_specs=[pl.BlockSpec((tm, k), lambda j, i: (i, 0))] + w_specs,
        out_specs=pl.BlockSpec((tm, tn), lambda j, i: (i, j)),
        scratch_shapes=[pltpu.VMEM((k, tn), BF16)],
        compiler_params=_cparams("parallel", "arbitrary"),
        name=name,
    )(x, *([w] * len(w_specs)))


def _cast_kernel(w_ref, o_ref):
    o_ref[...] = w_ref[...].astype(BF16)


def _cast_bf16(w, layer):
    _, r, n = w.shape
    tr = _pick_tile(r, (512, 256, 128, 8))
    return pl.pallas_call(
        _cast_kernel,
        out_shape=jax.ShapeDtypeStruct((r, n), BF16),
        grid=(r // tr,),
        in_specs=[pl.BlockSpec((None, tr, n), lambda i: (layer, i, 0))],
        out_specs=pl.BlockSpec((tr, n), lambda i: (i, 0)),
        compiler_params=_cparams("parallel"),
        name="cast_bf16",
    )(w)


def _rope_tables(seq, ctx_len):
    n_rows = seq // GRID_W
    row = jnp.repeat(jnp.arange(n_rows, dtype=F32), GRID_W)
    col = jnp.tile(jnp.arange(GRID_W, dtype=F32), n_rows)
    half = DA_HEAD_DIM // 2
    inv_freq = ROPE_THETA ** (-jnp.arange(0, half, 2, dtype=F32) / half)
    ar = row[:, None] * inv_freq[None, :]
    ac = col[:, None] * inv_freq[None, :]
    ang = jnp.concatenate([ar, ar, ac, ac], axis=-1)
    ang = jnp.concatenate([ang, jnp.zeros((ctx_len, DA_HEAD_DIM), F32)], axis=0)
    cos = jnp.tile(jnp.cos(ang), (1, 2))
    sin = jnp.tile(jnp.sin(ang), (1, 2))
    first = (jnp.arange(LANES) % half) < (half // 2)
    sin_a = jnp.where(first[None, :], -sin, 0.0)
    sin_b = jnp.where(first[None, :], 0.0, sin)
    return cos, sin_a, sin_b


def _rope_kernel(p_ref, cos_ref, sa_ref, sb_ref, o_ref):
    cos = cos_ref[...]
    sa = sa_ref[...]
    sb = sb_ref[...]
    quarter = DA_HEAD_DIM // 4
    for h in range(2 * DA_HEADS):
        sl = slice(h * LANES, (h + 1) * LANES)
        x = p_ref[:, sl].astype(F32)
        r = x * cos + pltpu.roll(x, LANES - quarter, 1) * sa + pltpu.roll(x, quarter, 1) * sb
        if h < DA_HEADS:
            r = r * (DA_HEAD_DIM ** -0.5)
        o_ref[:, sl] = r.astype(BF16)


def _rope(p, tables, dims):
    t = p.shape[0]
    tm = TOK_TILE
    nl, seq, ctx_len = dims["nl"], dims["seq"], dims["ctx"]

    def tab(i):
        return (jnp.where(i < nl // tm, i % (seq // tm), seq // tm + (i - nl // tm) % (ctx_len // tm)), 0)

    w = 2 * DA_WIDTH
    tspec = pl.BlockSpec((tm, LANES), tab)
    return pl.pallas_call(
        _rope_kernel,
        out_shape=jax.ShapeDtypeStruct((t, w), BF16),
        grid=(t // tm,),
        in_specs=[pl.BlockSpec((tm, w), lambda i: (i, 0)), tspec, tspec, tspec],
        out_specs=pl.BlockSpec((tm, w), lambda i: (i, 0)),
        compiler_params=_cparams("parallel"),
        name="rope_qk",
    )(p, *tables)


ATTN_ROW_BLOCK = 64


def _attn_kernel(*refs, tq, tk, n_lat_k, lam_init):
    if n_lat_k:
        lam_ref, subln_ref, q_ref, kc_ref, vc_ref, kl_ref, vl_ref, o_ref = refs[:8]
    else:
        lam_ref, subln_ref, q_ref, kc_ref, vc_ref = refs[:5]
        o_ref = refs[-8]
    qs_ref, s_ref, p_ref, m_ref, a_ref, acc_ref, vx_ref = refs[-7:]
    rb = ATTN_ROW_BLOCK
    n_ctx = kc_ref.shape[0]

    n_lat = n_lat_k * tk

    @pl.when(pl.program_id(2) == 0)
    def _():
        if n_lat_k:
            vx_ref[0:n_lat, 0:LANES] = vl_ref[...]
        vx_ref[n_lat:, 0:LANES] = vc_ref[...]
        vx_ref[:, LANES:2 * LANES] = jnp.ones((vx_ref.shape[0], LANES), BF16)

    q = q_ref[...]
    lane = lax.broadcasted_iota(jnp.int32, q.shape, 1)
    zero = jnp.zeros_like(q)
    qs_ref[0:tq, :] = jnp.where(lane < DA_HEAD_DIM, q, zero)
    qs_ref[tq:2 * tq, :] = jnp.where(lane >= DA_HEAD_DIM, q, zero)
    m_ref[...] = jnp.full(m_ref.shape, -1e30, F32)
    acc_ref[...] = jnp.zeros(acc_ref.shape, F32)

    def key_rows(t, nk):
        return pl.ds(t * tk if isinstance(t, int) else pl.multiple_of(t * tk, tk), nk)

    def keys(t):
        if isinstance(t, int) and t == n_lat_k:
            return kc_ref[...], n_ctx
        return kl_ref[key_rows(t, tk), :], tk

    def scores(t, buf):
        k, nk = keys(t)
        s_ref[buf, :, 0:nk] = lax.dot_general(qs_ref[...], k, (((1,), (1,)), ((), ())),
                                              preferred_element_type=F32)

    def softmax(buf, nk):
        reps = nk // LANES
        for r in range(2 * tq // rb):
            rows = slice(r * rb, (r + 1) * rb)
            m_prev = m_ref[rows, :]
            m_new = jnp.maximum(m_prev, jnp.max(s_ref[buf, rows, 0:nk], axis=-1, keepdims=True))
            a_ref[buf, rows, :] = jnp.exp(m_prev - m_new)
            m_ref[rows, :] = m_new
            p = jnp.exp(s_ref[buf, rows, 0:nk] - jnp.concatenate([m_new] * reps, axis=1))
            p_ref[buf, rows, 0:nk] = p.astype(BF16)

    def weighted(t, buf):
        nk = n_ctx if isinstance(t, int) and t == n_lat_k else tk
        vx = vx_ref[key_rows(t, nk), :]
        alpha = a_ref[buf]
        pv = jnp.dot(p_ref[buf, :, 0:nk], vx, preferred_element_type=F32)
        acc_ref[...] = jnp.concatenate([alpha, alpha], axis=1) * acc_ref[...] + pv

    def stage(t, buf):
        scores(t + 1, 1 - buf)
        softmax(buf, tk)
        weighted(t - 1, 1 - buf)

    n_static = n_lat_k + 1
    loop_pairs = (n_lat_k - 2) // 2 if (n_lat_k >= 4 and n_lat_k % 2 == 0) else 0
    scores(0, 0)
    t = 0
    while t < n_static:
        if t == 1 and loop_pairs:
            def body(i, carry):
                stage(1 + 2 * i, 1)
                stage(2 + 2 * i, 0)
                return carry
            lax.fori_loop(0, loop_pairs, body, 0)
            t += 2 * loop_pairs
            continue
        if t + 1 < n_static:
            scores(t + 1, (t + 1) % 2)
        softmax(t % 2, n_ctx if t == n_lat_k else tk)
        if t >= 1:
            weighted(t - 1, (t - 1) % 2)
        t += 1
    weighted(n_lat_k, n_lat_k % 2)

    lf = lam_ref[...]
    lam = (jnp.exp(jnp.sum(lf[0:1] * lf[1:2], axis=-1, keepdims=True))
           - jnp.exp(jnp.sum(lf[2:3] * lf[3:4], axis=-1, keepdims=True)) + lam_init)
    o = acc_ref[:, 0:LANES] / acc_ref[:, LANES:2 * LANES]
    out = o[:tq] - lam * o[tq:]
    y = _rms(out) * subln_ref[...] * (1.0 - lam_init)
    o_ref[...] = y.astype(BF16)


def _attn_call(qk, p, da_lambda, da_subln, lam_init, dims, *, tq, q_row0, n_q, rows, with_lat):
    batch, seq, ctx_len, nl = dims["batch"], dims["seq"], dims["ctx"], dims["nl"]
    tk = 512
    q_map = lambda b, h, i: (q_row0 // tq + b * n_q + i, h)
    o_map = lambda b, h, i: (b * n_q + i, h)
    kv_lat = lambda off: pl.BlockSpec((seq, LANES), lambda b, h, i: (b, off + h))
    kv_ctx = lambda off: pl.BlockSpec((ctx_len, LANES), lambda b, h, i: (nl // ctx_len + b, off + h))
    in_specs = [pl.BlockSpec((4, DA_HEAD_DIM), lambda b, h, i: (0, 0)),
                pl.BlockSpec((1, DA_V_DIM), lambda b, h, i: (0, 0)),
                pl.BlockSpec((tq, LANES), q_map),
                kv_ctx(DA_HEADS), kv_ctx(V_OFF // LANES)]
    args = [da_lambda, da_subln.reshape(1, DA_V_DIM), qk, qk, p]
    if with_lat:
        in_specs += [kv_lat(DA_HEADS), kv_lat(V_OFF // LANES)]
        args += [qk, p]
    n_keys = ctx_len + (seq if with_lat else 0)
    scratch = [pltpu.VMEM((2 * tq, LANES), BF16), pltpu.VMEM((2, 2 * tq, tk), F32), pltpu.VMEM((2, 2 * tq, tk), BF16),
               pltpu.VMEM((2 * tq, LANES), F32), pltpu.VMEM((2, 2 * tq, LANES), F32),
               pltpu.VMEM((2 * tq, 2 * LANES), F32), pltpu.VMEM((n_keys, 2 * LANES), BF16)]
    kern = functools.partial(_attn_kernel, tq=tq, tk=tk, n_lat_k=seq // tk if with_lat else 0, lam_init=lam_init)
    return pl.pallas_call(
        kern,
        out_shape=jax.ShapeDtypeStruct((rows, DA_WIDTH), BF16),
        grid=(batch, DA_HEADS, n_q),
        in_specs=in_specs,
        out_specs=pl.BlockSpec((tq, LANES), o_map),
        scratch_shapes=scratch,
        compiler_params=_cparams("parallel", "parallel", "arbitrary"),
        name="diff_attn" if with_lat else "diff_attn_ctx",
    )(*args)


def _attention(qk, p, da_lambda, da_subln, lam_init, with_ctx, dims):
    batch, seq, ctx_len, nl = dims["batch"], dims["seq"], dims["ctx"], dims["nl"]
    tq = _pick_tile(seq, (1024, 512, TOK_TILE))
    y = _attn_call(qk, p, da_lambda, da_subln, lam_init, dims, tq=tq, q_row0=0, n_q=seq // tq, rows=nl,
                   with_lat=True)
    if with_ctx:
        tqc = TOK_TILE
        y_ctx = _attn_call(qk, p, da_lambda, da_subln, lam_init, dims, tq=tqc, q_row0=nl, n_q=ctx_len // tqc,
                           rows=batch * ctx_len, with_lat=False)
        y = jnp.concatenate([y, y_ctx], axis=0)
    return y


def _softplus(x):
    return jnp.maximum(x, 0.0) + jnp.log(1.0 + jnp.exp(-jnp.abs(x)))


def _ssd_dt_kernel(h_ref, w_ref, b_ref, alog_ref, dd_ref):
    raw = jnp.dot(h_ref[...], w_ref[...].astype(BF16), preferred_element_type=F32)
    n = 2 * SSD_HEADS
    dt = _softplus(raw + b_ref[...])
    lane = lax.broadcasted_iota(jnp.int32, dt.shape, 1)
    dta = pltpu.roll(dt, n, 1) * (-jnp.exp(alog_ref[...]))
    dd = jnp.where(lane < n, dt, jnp.where(lane < 2 * n, dta, 0.0))
    dd_ref[...] = dd


def _ssd_dt(h, w_in, layer, bias2, alog2):
    t, d = h.shape
    tm = TOK_TILE
    assert REF_DT_OFF % LANES == 0
    return pl.pallas_call(
        _ssd_dt_kernel,
        out_shape=jax.ShapeDtypeStruct((t, LANES), F32),
        grid=(t // tm,),
        in_specs=[pl.BlockSpec((tm, d), lambda i: (i, 0)),
                  pl.BlockSpec((None, d, LANES), lambda i: (layer, 0, REF_DT_OFF // LANES)),
                  pl.BlockSpec((1, LANES), lambda i: (0, 0)),
                  pl.BlockSpec((1, LANES), lambda i: (0, 0))],
        out_specs=pl.BlockSpec((tm, LANES), lambda i: (i, 0)),
        compiler_params=_cparams("parallel"),
        name="ssd_dt",
    )(h, w_in, bias2, alog2)


def _conv_kernel(prev_ref, x_ref, next_ref, w_ref, b_ref, o_ref, *, tm, n_lat_tiles, lat_per_seq, ctx_per_seq):
    i = pl.program_id(0)
    is_lat = i < n_lat_tiles
    pos = jnp.where(is_lat, i % lat_per_seq, (i - n_lat_tiles) % ctx_per_seq)
    per = jnp.where(is_lat, lat_per_seq, ctx_per_seq)
    keep_prev = (pos > 0).astype(F32)
    keep_next = (pos < per - 1).astype(F32)
    ext = jnp.concatenate([prev_ref[...].astype(F32) * keep_prev, x_ref[...].astype(F32),
                           next_ref[...].astype(F32) * keep_next], axis=0)
    n = tm + 2 * SUBLANES
    w = w_ref[...]
    acc = jnp.zeros((tm, ext.shape[1]), F32) + b_ref[...]
    for k in range(SSD_CONV):
        shift = (SSD_CONV // 2 - k) % n
        rolled = ext if shift == 0 else pltpu.roll(ext, shift, 0)
        acc = acc + rolled[SUBLANES:SUBLANES + tm] * w[k:k + 1, :]
    o_ref[...] = _silu(acc).astype(BF16)


def _ssd_conv(p, conv_w, conv_b, dims):
    t = p.shape[0]
    tm = TOK_TILE
    cb = 512
    col0 = XBC_OFF // cb
    r8 = tm // SUBLANES
    last8 = t // SUBLANES - 1
    kern = functools.partial(_conv_kernel, tm=tm, n_lat_tiles=dims["nl"] // tm, lat_per_seq=dims["seq"] // tm,
                             ctx_per_seq=dims["ctx"] // tm)
    return pl.pallas_call(
        kern,
        out_shape=jax.ShapeDtypeStruct((t, SSD_XBC), BF16),
        grid=(t // tm, SSD_XBC // cb),
        in_specs=[pl.BlockSpec((SUBLANES, cb), lambda i, j: (jnp.maximum(i * r8 - 1, 0), col0 + j)),
                  pl.BlockSpec((tm, cb), lambda i, j: (i, col0 + j)),
                  pl.BlockSpec((SUBLANES, cb), lambda i, j: (jnp.minimum((i + 1) * r8, last8), col0 + j)),
                  pl.BlockSpec((SSD_CONV, cb), lambda i, j: (0, j)),
                  pl.BlockSpec((1, cb), lambda i, j: (0, j))],
        out_specs=pl.BlockSpec((tm, cb), lambda i, j: (i, j)),
        compiler_params=_cparams("parallel", "parallel"),
        name="ssd_conv",
    )(p, p, p, conv_w, conv_b.reshape(1, SSD_XBC))


def _split3(x):
    hi = x.astype(BF16)
    rest = x - hi.astype(F32)
    mid = rest.astype(BF16)
    return hi, mid, (rest - mid.astype(F32)).astype(BF16)


def _ssd_scan_kernel(*refs, direction, final, ncc, write_ctx):
    if final:
        xbc_ref, dd_ref, yf_ref, z_ref, dskip_ref, norm_ref, o_ref, h_ref = refs
    else:
        xbc_ref, dd_ref, o_ref, h_ref = refs
    j = pl.program_id(1)
    q = SSD_CHUNK
    gw = SSD_HPG * SSD_HEAD_DIM

    @pl.when(j == 0)
    def _():
        h_ref[...] = jnp.zeros_like(h_ref)

    xbc = xbc_ref[...]
    xs = xbc[:, :SSD_WIDTH].astype(F32)
    bm = xbc[:, SSD_WIDTH:SSD_WIDTH + SSD_GROUPS * SSD_STATE]
    cm = xbc[:, SSD_WIDTH + SSD_GROUPS * SSD_STATE:]
    dd = dd_ref[...]

    ii =lax.broadcasted_iota(jnp.int32, (q, q), 0)
    jj = lax.broadcasted_iota(jnp.int32, (q, q), 1)
    if direction == 0:
        mask = jj <= ii
        last = q - 1
    else:
        mask = jj >= ii
        last = 0
    tri = jnp.where(mask, 1.0, 0.0).astype(BF16)
    ac = sum(jnp.dot(tri, piece, preferred_element_type=F32) for piece in _split3(dd))
    ac_t = ac.T

    er = lax.broadcasted_iota(jnp.int32, (LANES, SSD_WIDTH), 0)
    ec = lax.broadcasted_iota(jnp.int32, (LANES, SSD_WIDTH), 1) // SSD_HEAD_DIM
    dt_col = direction * SSD_HEADS
    ac_col = 2 * SSD_HEADS + direction * SSD_HEADS
    e_dt = jnp.where(er == ec + dt_col, 1.0, 0.0).astype(BF16)
    e_ac = jnp.where(er == ec + ac_col, 1.0, 0.0).astype(BF16)
    dt_exp = sum(jnp.dot(piece, e_dt, preferred_element_type=F32) for piece in _split3(dd))
    ac_exp = sum(jnp.dot(piece, e_ac, preferred_element_type=F32) for piece in _split3(ac))
    ac_last = ac_exp[last:last + 1, :]
    eac = jnp.exp(ac_exp)
    dec_end = jnp.exp(ac_last - ac_exp)
    chunk_dec = jnp.exp(ac_last)
    xdt = xs * dt_exp
    xdt_b = xdt.astype(BF16)
    xde_b = (xdt * dec_end).astype(BF16)

    lane = lax.broadcasted_iota(jnp.int32, (q, LANES), 1)
    left = lane < SSD_HEAD_DIM
    zero_b = jnp.zeros((q, LANES), BF16)
    pieces = []
    for g in range(SSD_GROUPS):
        bg = bm[:, g * SSD_STATE:(g + 1) * SSD_STATE]
        cg = cm[:, g * SSD_STATE:(g + 1) * SSD_STATE]
        cb = lax.dot_general(cg, bg, (((1,), (1,)), ((), ())), preferred_element_type=F32)
        h_t = h_ref[g]
        y_off = jnp.dot(cg, h_t.astype(BF16), preferred_element_type=F32) * eac[:, g * gw:(g + 1) * gw]
        bg_t = bg.astype(F32).T.astype(BF16)
        s_t = jnp.dot(bg_t, xde_b[:, g * gw:(g + 1) * gw], preferred_element_type=F32)
        h_ref[g] = h_t * chunk_dec[:, g * gw:(g + 1) * gw] + s_t
        for kp in range(SSD_HPG // 2):
            k0 = g * SSD_HPG + 2 * kp
            ms = []
            for k in (k0, k0 + 1):
                c = ac_col + k
                seg = ac[:, c:c + 1] - ac_t[c:c + 1, :]
                ms.append((cb * jnp.exp(jnp.where(mask, seg, -1e30))).astype(BF16))
            xp = xdt_b[:, k0 * SSD_HEAD_DIM:k0 * SSD_HEAD_DIM + LANES]
            y_diag = (jnp.dot(ms[0], jnp.where(left, xp, zero_b), preferred_element_type=F32)
                      + jnp.dot(ms[1], jnp.where(left, zero_b, xp), preferred_element_type=F32))
            pieces.append(y_diag + y_off[:, kp * LANES:(kp + 1) * LANES])
    y = jnp.concatenate(pieces, axis=1)

    def emit():
        if final:
            yt = y + yf_ref[...] + dskip_ref[...] * xs
            yt = yt * _silu(z_ref[...].astype(F32))
            o_ref[...] = (_rms(yt) * norm_ref[...]).astype(BF16)
        else:
            o_ref[...] = y

    if write_ctx:
        emit()
    else:
        pl.when(j >= ncc)(emit)


def _ssd_scan(xbc, dd, direction, dims, with_ctx, final_args=None):
    batch, seq, ctx_len, nl = dims["batch"], dims["seq"], dims["ctx"], dims["nl"]
    q = SSD_CHUNK
    ncl, ncc = seq // q, ctx_len // q
    final = final_args is not None
    rows = nl + (batch * ctx_len if with_ctx else 0)

    def chunk(b, j):
        jc = j if direction == 0 else ncc - 1 - j
        jl = (j - ncc) if direction == 0 else ncl - 1 - (j - ncc)
        return jnp.where(j < ncc, nl // q + b * ncc + jc, b * ncl + jl)

    def out_chunk(b, j):
        if with_ctx:
            return chunk(b, j)
        return chunk(b, jnp.maximum(j, ncc))

    in_specs = [pl.BlockSpec((q, SSD_XBC), lambda b, j: (chunk(b, j), 0)),
                pl.BlockSpec((q, LANES), lambda b, j: (chunk(b, j), 0))]
    args = [xbc, dd]
    if final:
        yf, p, dskip, norm = final_args
        in_specs += [pl.BlockSpec((q, SSD_WIDTH), lambda b, j: (out_chunk(b, j), 0)),
                     pl.BlockSpec((q, SSD_WIDTH), lambda b, j: (chunk(b, j), Z_OFF // SSD_WIDTH)),
                     pl.BlockSpec((1, SSD_WIDTH), lambda b, j: (0, 0)),
                     pl.BlockSpec((1, SSD_WIDTH), lambda b, j: (0, 0))]
        args += [yf, p, dskip, norm]
    return pl.pallas_call(
        functools.partial(_ssd_scan_kernel, direction=direction, final=final, ncc=ncc, write_ctx=with_ctx),
        out_shape=jax.ShapeDtypeStruct((rows, SSD_WIDTH), BF16 if final else F32),
        grid=(batch, ncc + ncl),
        in_specs=in_specs,
        out_specs=pl.BlockSpec((q, SSD_WIDTH), lambda b, j: (out_chunk(b, j), 0)),
        scratch_shapes=[pltpu.VMEM((SSD_GROUPS, SSD_STATE, SSD_HPG * SSD_HEAD_DIM), F32)],
        compiler_params=_cparams("parallel", "arbitrary"),
        name="ssd_scan_bwd" if direction else "ssd_scan_fwd",
    )(*args)


S5_BLOCK_GROUPS = LANES // S5_GROUP
S5_BLOCKS = S5_GROUPS // S5_BLOCK_GROUPS
S5_BS = S5_BLOCK_GROUPS * S5_STATE
S5_LAG_ROWS = 24


def _expm1(x):
    poly = 1.0 + x / 10.0
    for n in range(9, 1, -1):
        poly = 1.0 + (x / n) * poly
    return jnp.where(jnp.abs(x) < 0.35, x * poly, jnp.exp(x) - 1.0)


def _s5_params(lam_re, lam_im, log_step, b_re, b_im, c_re, c_im, s5_d):
    nb, gl = S5_BLOCKS, S5_BLOCK_GROUPS
    rows = [lam_re[0], lam_im[0], lam_re[1], lam_im[1],
            jnp.repeat(log_step[0], S5_STATE), jnp.repeat(log_step[1], S5_STATE)]
    rows = [r.reshape(nb, S5_BS) for r in rows] + [jnp.zeros((nb, S5_BS), F32)] * 2
    lam_rows = jnp.stack(rows, axis=1)
    eye = jnp.eye(gl, dtype=F32)

    def bd_in(b):
        return jnp.einsum("jgpe,gh->jgehp", b.reshape(nb, gl, S5_STATE, S5_GROUP), eye).reshape(nb, LANES, S5_BS)

    def bd_out(c):
        return jnp.einsum("jgfp,gh->jgfhp", c.reshape(nb, gl, S5_GROUP, S5_STATE), eye).reshape(nb, LANES, S5_BS)

    b_bd = jnp.stack([bd_in(b_re), bd_in(b_im)], axis=1)
    ct_bd = jnp.stack([bd_out(c_re), bd_out(c_im)], axis=1)
    return lam_rows, b_bd, ct_bd, s5_d.reshape(nb, 1, LANES)


def _gelu_tanh(x):
    return 0.5 * x * (1.0 + jnp.tanh(math.sqrt(2.0 / math.pi) * (x + 0.044715 * x * x * x)))


def _s5_kernel(u_ref, lam_ref, b_ref, ct_ref, d_ref, o_ref,
               x_ref, w_ref, m_ref, s_ref, y_ref, taps_ref, pw_ref, bb_ref, *, batch, ncl, ncc):
    tc = S5_CHUNK
    nch = x_ref.shape[0]
    sw = S5_BS
    nt = (((1,), (1,)), ((), ()))

    for s in range(tc):
        x_ref[:, s * LANES:(s + 1) * LANES] = u_ref[pl.ds(s, nch, stride=tc), :].astype(BF16)

    rows = lam_ref[0]
    kk = lax.broadcasted_iota(jnp.int32, (S5_LAG_ROWS, sw), 0).astype(F32)
    b_re, b_im = b_ref[0, 0], b_ref[0, 1]
    for d in range(2):
        l_re, l_im = rows[2 * d:2 * d + 1], rows[2 * d + 1:2 * d + 2]
        delta = jnp.exp(rows[4 + d:5 + d])
        lr, li = l_re * delta, l_im * delta
        mag = jnp.exp(kk * lr)
        pw_ref[2 * d] = mag * jnp.cos(kk * li)
        pw_ref[2 * d + 1] = mag * jnp.sin(kk * li)
        xr = _expm1(lr) * jnp.cos(li) - 2.0 * jnp.sin(0.5 * li) ** 2
        xi = jnp.exp(lr) * jnp.sin(li)
        den = l_re * l_re + l_im * l_im
        co_re = (xr * l_re + xi * l_im) / den
        co_im = (xi * l_re - xr * l_im) / den
        bb_ref[2 * d] = co_re * b_re - co_im * b_im
        bb_ref[2 * d + 1] = co_re * b_im + co_im * b_re

    def power(d, lag):
        return pw_ref[2 * d, lag:lag + 1, :], pw_ref[2 * d + 1, lag:lag + 1, :]

    ct_re, ct_im = ct_ref[0, 0], ct_ref[0, 1]
    ct_re_b, ct_im_b = ct_re.astype(BF16), ct_im.astype(BF16)
    for d in range(2):
        tp = []
        for part in range(2):
            q = 2 * d + part
            for s in range(tc):
                pr, pi = power(d, tc - 1 - s if d == 0 else s)
                if part == 0:
                    tile = pr * bb_ref[2 * d] - pi * bb_ref[2 * d + 1]
                else:
                    tile = pr * bb_ref[2 * d + 1] + pi * bb_ref[2 * d]
                w_ref[q, s * LANES:(s + 1) * LANES, :] = tile.astype(BF16)
            w = w_ref[q]
            s_ref[:, q * sw:(q + 1) * sw] = jnp.dot(x_ref[...], w, preferred_element_type=F32)
            tp.append(lax.dot_general(w, ct_re_b if part == 0 else ct_im_b, nt, preferred_element_type=F32))
        taps_ref[d] = tp[0] - tp[1]

    ri = lax.broadcasted_iota(jnp.int32, (LANES, LANES), 0)
    ci = lax.broadcasted_iota(jnp.int32, (LANES, LANES), 1)
    skip = jnp.where(ri == ci, d_ref[0], 0.0)
    for s in range(tc):
        for t in range(tc):
            if t >= s:
                lag_tile = tc - 1 - (t - s)
                tile = taps_ref[0, lag_tile * LANES:(lag_tile + 1) * LANES, :]
                if t == s:
                    tile = tile + taps_ref[1, 0:LANES, :] + skip
            else:
                tile = taps_ref[1, (s - t) * LANES:(s - t + 1) * LANES, :]
            m_ref[s * LANES:(s + 1) * LANES, t * LANES:(t + 1) * LANES] = tile.astype(BF16)

    af_re, af_im = power(0, tc)
    ab_re, ab_im = power(1, tc)

    def advance(row, h, dir_off, ar, ai):
        h_re, h_im = h
        s_re = s_ref[pl.ds(row, 1), dir_off:dir_off + sw]
        s_im = s_ref[pl.ds(row, 1), dir_off + sw:dir_off + 2 * sw]
        s_ref[pl.ds(row, 1), dir_off:dir_off + sw] = h_re
        s_ref[pl.ds(row, 1), dir_off + sw:dir_off + 2 * sw] = h_im
        return ar * h_re - ai * h_im + s_re, ar * h_im + ai * h_re + s_im

    def sweep(first_chunk, n):
        def body(i, c):
            return tuple((advance(first_chunk(b) + i, c[b][0], 0, af_re, af_im),
                          advance(first_chunk(b) + n - 1 - i, c[b][1], 2 * sw, ab_re, ab_im)) for b in range(batch))
        return body

    zero = jnp.zeros((1, sw), F32)
    c = lax.fori_loop(0, ncc, sweep(lambda b: batch * ncl + b * ncc, ncc), (((zero, zero), (zero, zero)),) * batch)
    lax.fori_loop(0, ncl, sweep(lambda b: b * ncl, ncl), c)

    y = jnp.dot(x_ref[...], m_ref[...], preferred_element_type=F32)
    for q in range(4):
        d, part = divmod(q, 2)
        for t in range(tc):
            pr, pi = power(d, t + 1 if d == 0 else tc - t)
            tile = ct_re * pr - ct_im * pi if part == 0 else -(ct_re * pi + ct_im * pr)
            w_ref[q, t * LANES:(t + 1) * LANES, :] = tile.astype(BF16)
        h = s_ref[:, q * sw:(q + 1) * sw].astype(BF16)
        y = y + lax.dot_general(h, w_ref[q], nt, preferred_element_type=F32)
    y_ref[...] = y
    for t in range(tc):
        o_ref[pl.ds(t, nch, stride=tc), :] = _gelu_tanh(y_ref[:, t * LANES:(t + 1) * LANES])


def _s5(u, params, dims):
    lam_rows, b_bd, ct_bd, d_skip = params
    t = u.shape[0]
    nch = t // S5_CHUNK
    ncl, ncc = dims["seq"] // S5_CHUNK, dims["ctx"] // S5_CHUNK
    width = S5_CHUNK * LANES
    pspec = pl.BlockSpec((1, 2, LANES, S5_BS), lambda i: (i, 0, 0, 0))
    return pl.pallas_call(
        functools.partial(_s5_kernel, batch=dims["batch"], ncl=ncl, ncc=ncc),
        out_shape=jax.ShapeDtypeStruct((t, S5_WIDTH), F32),
        grid=(S5_BLOCKS,),
        in_specs=[pl.BlockSpec((t, LANES), lambda i: (0, i)),
                  pl.BlockSpec((1, SUBLANES, S5_BS), lambda i: (i, 0, 0)), pspec, pspec,
                  pl.BlockSpec((1, 1, LANES), lambda i: (i, 0, 0))],
        out_specs=pl.BlockSpec((t, LANES), lambda i: (0, i)),
        scratch_shapes=[pltpu.VMEM((nch, width), BF16),
                        pltpu.VMEM((4, width, S5_BS), BF16),
                        pltpu.VMEM((width, width), BF16),
                        pltpu.VMEM((nch, 4 * S5_BS), F32),
                        pltpu.VMEM((nch, width), F32),
                        pltpu.VMEM((2, width, LANES), F32),
                        pltpu.VMEM((4, S5_LAG_ROWS, S5_BS), F32),
                        pltpu.VMEM((4, LANES, S5_BS), F32)],
        compiler_params=_cparams("parallel"),
        name="s5_scan",
    )(u, lam_rows, b_bd, ct_bd, d_skip)


def _glu_kernel(x_ref, wa_ref, wb_ref, ba_ref, bb_ref, o_ref, wab_ref, wbb_ref):
    @pl.when(pl.program_id(1) == 0)
    def _():
        wab_ref[...] = wa_ref[...].astype(BF16)
        wbb_ref[...] = wb_ref[...].astype(BF16)

    x = x_ref[...].astype(BF16)
    a = jnp.dot(x, wab_ref[...], preferred_element_type=F32) + ba_ref[...]
    b = jnp.dot(x, wbb_ref[...], preferred_element_type=F32) + bb_ref[...]
    o_ref[...] = (a * _sigmoid(b)).astype(BF16)


def _glu(x, w, bias, layer, rows, tm):
    k = x.shape[1]
    tn = W_TILE
    nb = S5_WIDTH // tn
    return pl.pallas_call(
        _glu_kernel,
        out_shape=jax.ShapeDtypeStruct((rows, S5_WIDTH), BF16),
        grid=(nb, rows // tm),
        in_specs=[pl.BlockSpec((tm, k), lambda j, i: (i, 0)),
                  pl.BlockSpec((None, k, tn), lambda j, i: (layer, 0, j)),
                  pl.BlockSpec((None, k, tn), lambda j, i: (layer, 0, nb + j)),
                  pl.BlockSpec((None, 1, tn), lambda j, i: (layer, 0, j)),
                  pl.BlockSpec((None, 1, tn), lambda j, i: (layer, 0, nb + j))],
        out_specs=pl.BlockSpec((tm, tn), lambda j, i: (i, j)),
        scratch_shapes=[pltpu.VMEM((k, tn), BF16), pltpu.VMEM((k, tn), BF16)],
        compiler_params=_cparams("parallel", "arbitrary"),
        name="s5_glu",
    )(x, w, w, bias, bias)


def _merge_kernel(ya_ref, yb_ref, yc_ref, wa_ref, wb_ref, wc_ref, ga_ref, gb_ref, gc_ref, o_ref, wbf_ref):
    @pl.when(pl.program_id(1) == 0)
    def _():
        for n, w_ref in enumerate((wa_ref, wb_ref, wc_ref)):
            wbf_ref[n] = w_ref[0].astype(BF16)

    acc = None
    for n, (y_ref, g_ref) in enumerate(((ya_ref, ga_ref), (yb_ref, gb_ref), (yc_ref, gc_ref))):
        br = jnp.dot(y_ref[...], wbf_ref[n], preferred_element_type=F32)
        term = _sigmoid(g_ref[...].astype(F32)) * br
        acc = term if acc is None else acc + term
    o_ref[...] = acc.astype(BF16)


def _merge(ya, yb, yc, w_branch, layer, gates, rows, tm):
    k = ya.shape[1]
    d = w_branch.shape[3]
    tn = W_TILE
    gstep = d // tn
    yspec = pl.BlockSpec((tm, k), lambda j, i: (i, 0))
    wspec = lambda n: pl.BlockSpec((None, 1, k, tn), lambda j, i: (layer, n, 0, j))
    gspec = lambda n: pl.BlockSpec((tm, tn), lambda j, i: (i, n * gstep + j))
    return pl.pallas_call(
        _merge_kernel,
        out_shape=jax.ShapeDtypeStruct((rows, d), BF16),
        grid=(d // tn, rows // tm),
        in_specs=[yspec, yspec, yspec, wspec(0), wspec(1), wspec(2), gspec(0), gspec(1), gspec(2)],
        out_specs=pl.BlockSpec((tm, tn), lambda j, i: (i, j)),
        scratch_shapes=[pltpu.VMEM((N_BRANCH, k, tn), BF16)],
        compiler_params=_cparams("parallel", "arbitrary"),
        name="branch_merge",
    )(ya, yb, yc, w_branch, w_branch, w_branch, gates, gates, gates)


def _out_proj_kernel(*refs, n_lat):
    g_ref, w_ref, *x_refs, mod_ref, npost_ref, npre_ref, xo_ref, ho_ref = refs
    o = jnp.dot(g_ref[...], w_ref[...], preferred_element_type=F32)
    gate = mod_ref[0, 2:3, :]
    xn = _stream_tile(x_refs, n_lat) + gate * (_rms(o) * npost_ref[...])
    xo_ref[...] = xn
    shift = mod_ref[0, 3:4, :]
    scale = mod_ref[0, 4:5, :]
    ho_ref[...] = (_rms(xn) * npre_ref[...] * (1.0 + scale) + shift).astype(BF16)


def _out_proj(g, w_out, xs, mod, npost, npre, dims):
    r, d = g.shape
    tm = TOK_TILE
    seg = functools.partial(_seg_of_tile, tm=tm, n_lat=dims["nl"], seq=dims["seq"], batch=dims["batch"])
    row = pl.BlockSpec((tm, d), lambda i: (i, 0))
    vec = pl.BlockSpec((1, d), lambda i: (0, 0))
    return pl.pallas_call(
        functools.partial(_out_proj_kernel, n_lat=dims["nl"] // tm),
        out_shape=(jax.ShapeDtypeStruct((r, d), F32), jax.ShapeDtypeStruct((r, d), BF16)),
        grid=(r // tm,),
        in_specs=[row, pl.BlockSpec((d, d), lambda i: (0, 0))]
        + _stream_specs(xs, tm, d, dims["nl"])
        + [pl.BlockSpec((1, 6, d), lambda i: (seg(i), 0, 0)), vec, vec],
        out_specs=(row, row),
        compiler_params=_cparams("parallel"),
        name="out_proj",
    )(g, w_out, *xs, mod, npost.reshape(1, d), npre.reshape(1, d))


def _ffn_up_kernel(h_ref, wg_ref, wu_ref, o_ref, wgb_ref, wub_ref):
    @pl.when(pl.program_id(1) == 0)
    def _():
        wgb_ref[...] = wg_ref[...].astype(BF16)
        wub_ref[...] = wu_ref[...].astype(BF16)

    h = h_ref[...]
    a = jnp.dot(h, wgb_ref[...], preferred_element_type=F32)
    b = jnp.dot(h, wub_ref[...], preferred_element_type=F32)
    o_ref[...] = (_silu(a) * b).astype(BF16)


def _ffn_up(h, wg, wu, layer, tm):
    r, d = h.shape
    f = wg.shape[2]
    tn = W_TILE
    wspec = pl.BlockSpec((None, d, tn), lambda j, i: (layer, 0, j))
    return pl.pallas_call(
        _ffn_up_kernel,
        out_shape=jax.ShapeDtypeStruct((r, f), BF16),
        grid=(f // tn, r // tm),
        in_specs=[pl.BlockSpec((tm, d), lambda j, i: (i, 0)), wspec, wspec],
        out_specs=pl.BlockSpec((tm, tn), lambda j, i: (i, j)),
        scratch_shapes=[pltpu.VMEM((d, tn), BF16), pltpu.VMEM((d, tn), BF16)],
        compiler_params=_cparams("parallel", "arbitrary"),
        name="ffn_up",
    )(h, wg, wu)


def _ffn_down_kernel(a_ref, w_ref, x_ref, mod_ref, npost_ref, o_ref, acc_ref):
    k = pl.program_id(1)

    @pl.when(k == 0)
    def _():
        acc_ref[...] = jnp.zeros_like(acc_ref)

    acc_ref[...] += jnp.dot(a_ref[...], w_ref[...], preferred_element_type=F32)

    @pl.when(k == pl.num_programs(1) - 1)
    def _():
        gate = mod_ref[0, 5:6, :]
        o_ref[...] = x_ref[...] + gate * (_rms(acc_ref[...]) * npost_ref[...])


def _ffn_down(act, wd, x, mod, npost, dims):
    r, f = act.shape
    d = wd.shape[1]
    tm = TOK_TILE * 2
    nk = 4
    tk = f // nk
    seg = functools.partial(_seg_of_tile, tm=tm, n_lat=dims["nl"], seq=dims["seq"], batch=dims["batch"])
    row = pl.BlockSpec((tm, d), lambda i, k: (i, 0))
    return pl.pallas_call(
        _ffn_down_kernel,
        out_shape=jax.ShapeDtypeStruct((r, d), F32),
        grid=(r // tm, nk),
        in_specs=[pl.BlockSpec((tm, tk), lambda i, k: (i, k)),
                  pl.BlockSpec((tk, d), lambda i, k: (k, 0)),
                  row,
                  pl.BlockSpec((1, 6, d), lambda i, k: (seg(i), 0, 0)),
                  pl.BlockSpec((1, d), lambda i, k: (0, 0))],
        out_specs=row,
        scratch_shapes=[pltpu.VMEM((tm, d), F32)],
        compiler_params=_cparams("parallel", "arbitrary"),
        name="ffn_down",
    )(act, wd, x, mod, npost.reshape(1, d))


def _pick_tile(rows, cands):
    for c in cands:
        if rows % c == 0:
            return c
    raise ValueError(f"no tile in {cands} divides {rows}")


def kernel(x, c, ctx, c_ctx, ada_w, ada_b, norm_mix_pre, norm_mix_post, norm_ffn_pre, norm_ffn_post, w_in, da_lambda, da_subln, ssd_conv_w, ssd_conv_b, ssd_dt_bias, ssd_a_log, ssd_d, ssd_norm, s5_lam_re, s5_lam_im, s5_log_step, s5_b_re, s5_b_im, s5_c_re, s5_c_im, s5_d, s5_glu_w, s5_glu_b, w_branch, w_out, ffn_w_gate, ffn_w_up, ffn_w_down):
    batch, seq, d = x.shape
    ctx_len = ctx.shape[1]
    depth = ada_w.shape[0]
    nl, nc = batch * seq, batch * ctx_len
    dims = dict(batch=batch, seq=seq, ctx=ctx_len, nl=nl, nc=nc)
    assert batch < MOD_ROWS and seq % TOK_TILE == 0 and ctx_len % TOK_TILE == 0 and seq % ctx_len == 0
    assert nc % (2 * TOK_TILE) == 0 and seq % GRID_W == 0

    xs = (x.reshape(nl, d), ctx.reshape(nc, d))
    cc =jnp.concatenate([c, c_ctx[None], jnp.zeros((MOD_ROWS - batch - 1, d), F32)], axis=0)
    mod_all = _ada(cc, ada_w, ada_b).reshape(depth, MOD_ROWS, 6, d)
    rope_tabs = _rope_tables(seq, ctx_len)
    n_dt = 2 * SSD_HEADS
    pad = lambda v, before: jnp.concatenate(
        [jnp.zeros((1, before), F32), v.reshape(1, n_dt), jnp.zeros((1, LANES - n_dt - before), F32)], axis=1)
    tm_all = _pick_tile(nl + nc, (2176, 1088, 512, 256))

    for l in range(depth):
        last = l == depth - 1
        with_ctx = not last
        lam_init = 0.8 - 0.6 * math.exp(-0.3 * l)
        mod = mod_all[l]
        rows = nl + nc if with_ctx else nl
        tm = _pick_tile(rows, (1088, 1024, 512))

        h = _norm_mod(xs, norm_mix_pre[l], mod, 0, dims)
        p = _proj(h, w_in, l, 0, REF_DT_OFF, BF16, tm_all, "in_proj_main")
        u = _proj(h, w_in, l, REF_U_OFF, S5_WIDTH, F32, tm_all, "in_proj_u")
        gates = _proj(h, w_in, l, REF_U_OFF + S5_WIDTH, N_BRANCH * d, BF16, tm_all, "in_proj_gates")

        qk = _rope(p, rope_tabs, dims)
        y_attn = _attention(qk, p, da_lambda[l], da_subln[l], lam_init, with_ctx, dims)

        dd = _ssd_dt(h, w_in, l, pad(ssd_dt_bias[l], 0), pad(ssd_a_log[l], n_dt))
        xbc = _ssd_conv(p, ssd_conv_w[l], ssd_conv_b[l], dims)
        y_f = _ssd_scan(xbc, dd, 0, dims, with_ctx)
        dskip = jnp.repeat(ssd_d[l], SSD_HEAD_DIM).reshape(1, SSD_WIDTH)
        y_ssd = _ssd_scan(xbc, dd, 1, dims, with_ctx,
                          final_args=(y_f, p, dskip, ssd_norm[l].reshape(1, SSD_WIDTH)))

        s5p = _s5_params(s5_lam_re[l], s5_lam_im[l], s5_log_step[l], s5_b_re[l], s5_b_im[l],
                         s5_c_re[l], s5_c_im[l], s5_d[l])
        yg = _s5(u, s5p, dims)
        y_s5 = _glu(yg, s5_glu_w, s5_glu_b.reshape(depth, 1, 2 * S5_WIDTH), l, rows, tm)

        g = _merge(y_attn, y_ssd, y_s5, w_branch, l, gates, rows, tm)
        xt, h2 = _out_proj(g, _cast_bf16(w_out, l), xs, mod, norm_mix_post[l], norm_ffn_pre[l], dims)
        act = _ffn_up(h2, ffn_w_gate, ffn_w_up, l, tm)
        xt = _ffn_down(act, _cast_bf16(ffn_w_down, l), xt, mod, norm_ffn_post[l], dims)
        xs = (xt,)

    return xt[:nl].reshape(batch, seq, d)
```

```python
import functools
import math

import jax
import jax.numpy as jnp
from jax import lax
from jax.experimental import pallas as pl
from jax.experimental.pallas import tpu as pltpu

F32 = jnp.float32
BF16 = jnp.bfloat16
HIGHEST = lax.Precision.HIGHEST

GRID_W = 64
N_BRANCH = 3
DA_HEADS = 8
DA_HEAD_DIM = 64
DA_V_DIM = 2 * DA_HEAD_DIM
DA_WIDTH = DA_HEADS * DA_V_DIM
ROPE_THETA = 10000.0
SSD_HEADS = 16
SSD_HEAD_DIM = 64
SSD_GROUPS = 2
SSD_HPG = SSD_HEADS // SSD_GROUPS
SSD_STATE = 128
SSD_WIDTH = SSD_HEADS * SSD_HEAD_DIM
SSD_XBC = SSD_WIDTH + 2 * SSD_GROUPS * SSD_STATE
SSD_CONV = 5
SSD_CHUNK = 128
S5_GROUP = 16
S5_GROUPS = 64
S5_WIDTH = S5_GROUPS * S5_GROUP
S5_STATE = 64
S5_CHUNK = 16
RMS_EPS = 1e-6

V_OFF = 2 * DA_WIDTH
Z_OFF = 3 * DA_WIDTH
XBC_OFF = 4 * DA_WIDTH
REF_DT_OFF = XBC_OFF + SSD_XBC
REF_U_OFF = REF_DT_OFF + 2 * SSD_HEADS

LANES = 128
SUBLANES = 8
VMEM_LIMIT_BYTES = 52 * 1024 * 1024
MOD_ROWS = 8

TOK_TILE = 256


def _cparams(*sem):
    return pltpu.CompilerParams(dimension_semantics=sem, vmem_limit_bytes=VMEM_LIMIT_BYTES)


def _rms(x):
    return x * lax.rsqrt(jnp.mean(x * x, axis=-1, keepdims=True) + RMS_EPS)


def _sigmoid(x):
    return 1.0 / (1.0 + jnp.exp(-x))


def _silu(x):
    return x * _sigmoid(x)


def _seg_of_tile(i, tm, n_lat, seq, batch):
    return jnp.where(i < n_lat // tm, i // (seq // tm), batch)


def _ada_kernel(c_ref, w_ref, b_ref, o_ref):
    c = c_ref[...]
    o_ref[0] = jnp.dot(_silu(c), w_ref[0], precision=HIGHEST, preferred_element_type=F32) + b_ref[0]


def _ada(cc, ada_w, ada_b):
    depth, d, n = ada_w.shape
    tn = 1024
    return pl.pallas_call(
        _ada_kernel,
        out_shape=jax.ShapeDtypeStruct((depth, MOD_ROWS, n), F32),
        grid=(depth, n // tn),
        in_specs=[pl.BlockSpec((MOD_ROWS, d), lambda l, j: (0, 0)),
                  pl.BlockSpec((1, d, tn), lambda l, j: (l, 0, j)),
                  pl.BlockSpec((1, 1, tn), lambda l, j: (l, 0, j))],
        out_specs=pl.BlockSpec((1, MOD_ROWS, tn), lambda l, j: (l, 0, j)),
        compiler_params=_cparams("parallel", "parallel"),
        name="ada_mod",
    )(cc, ada_w, ada_b.reshape(depth, 1, n))


def _stream_specs(xs, tm, d, nl):
    n_lat = nl // tm
    if len(xs) == 1:
        return [pl.BlockSpec((tm, d), lambda i: (i, 0))]
    return [pl.BlockSpec((tm, d), lambda i: (jnp.minimum(i, n_lat - 1), 0)),
            pl.BlockSpec((tm, d), lambda i: (jnp.maximum(i - n_lat, 0), 0))]


def _stream_tile(x_refs, n_lat):
    if len(x_refs) == 1:
        return x_refs[0][...]
    return jnp.where(pl.program_id(0) < n_lat, x_refs[0][...], x_refs[1][...])


def _norm_mod_kernel(*refs, shift_idx, n_lat):
    *x_refs, g_ref, mod_ref, o_ref = refs
    y = _rms(_stream_tile(x_refs, n_lat)) * g_ref[...]
    shift = mod_ref[0, shift_idx:shift_idx + 1, :]
    scale = mod_ref[0, shift_idx + 1:shift_idx + 2, :]
    o_ref[...] = (y * (1.0 + scale) + shift).astype(BF16)


def _norm_mod(xs, g, mod, shift_idx, dims):
    d = xs[0].shape[1]
    t = dims["nl"] + dims["nc"]
    tm = TOK_TILE
    seg = functools.partial(_seg_of_tile, tm=tm, n_lat=dims["nl"], seq=dims["seq"], batch=dims["batch"])
    return pl.pallas_call(
        functools.partial(_norm_mod_kernel, shift_idx=shift_idx, n_lat=dims["nl"] // tm),
        out_shape=jax.ShapeDtypeStruct((t, d), BF16),
        grid=(t // tm,),
        in_specs=_stream_specs(xs, tm, d, dims["nl"]) + [
            pl.BlockSpec((1, d), lambda i: (0, 0)),
            pl.BlockSpec((1, 6, d), lambda i: (seg(i), 0, 0))],
        out_specs=pl.BlockSpec((tm, d), lambda i: (i, 0)),
        compiler_params=_cparams("parallel"),
        name="norm_mod",
    )(*xs, g.reshape(1, d), mod)


W_TILE = 512


def _shifted_rows(w0, w1, shift):
    return jnp.concatenate([w0[shift:], w1[:shift]], axis=0)


def _proj_kernel(*refs, shift):
    if shift:
        x_ref, w0_ref, w1_ref, o_ref, wb_ref = refs
    else:
        x_ref, w0_ref, o_ref, wb_ref = refs

    @pl.when(pl.program_id(1) == 0)
    def _():
        w = _shifted_rows(w0_ref[...], w1_ref[...], shift) if shift else w0_ref[...]
        wb_ref[...] = w.T.astype(BF16)

    o_ref[...] = jnp.dot(x_ref[...], wb_ref[...], preferred_element_type=F32).astype(o_ref.dtype)


def _proj(x, w_t, layer, col0, n_cols, out_dtype, tm, name):
    m, k = x.shape
    tn = W_TILE
    cb0, shift = col0 // tn, col0 % tn
    assert shift % SUBLANES == 0
    w_specs = [pl.BlockSpec((None, tn, k), lambda j, i: (layer, cb0 + j, 0))]
    if shift:
        w_specs.append(pl.BlockSpec((None, tn, k), lambda j, i: (layer, cb0 + j + 1, 0)))
    return pl.pallas_call(
        functools.partial(_proj_kernel, shift=shift),
        out_shape=jax.ShapeDtypeStruct((m, n_cols), out_dtype),
        grid=(n_cols // tn, m // tm),
        in_specs=[pl.BlockSpec((tm, k), lambda j, i: (i, 0))] + w_specs,
        out_specs=pl.BlockSpec((tm, tn), lambda j, i: (i, j)),
        scratch_shapes=[pltpu.VMEM((k, tn), BF16)],
        compiler_params=_cparams("parallel", "arbitrary"),
        name=name,
    )(x, *([w_t] * len(w_specs)))


def _cast_kernel(w_ref, o_ref):
    o_ref[...] = w_ref[...].astype(BF16)


def _cast_bf16(w, layer):
    _, r, n = w.shape
    tr = _pick_tile(r, (512, 256, 128, 8))
    return pl.pallas_call(
        _cast_kernel,
        out_shape=jax.ShapeDtypeStruct((r, n), BF16),
        grid=(r // tr,),
        in_specs=[pl.BlockSpec((None, tr, n), lambda i: (layer, i, 0))],
        out_specs=pl.BlockSpec((tr, n), lambda i: (i, 0)),
        compiler_params=_cparams("parallel"),
        name="cast_bf16",
    )(w)


def _rope_tables(seq, ctx_len):
    n_rows = seq // GRID_W
    row = jnp.repeat(jnp.arange(n_rows, dtype=F32), GRID_W)
    col = jnp.tile(jnp.arange(GRID_W, dtype=F32), n_rows)
    half = DA_HEAD_DIM // 2
    inv_freq = ROPE_THETA ** (-jnp.arange(0, half, 2, dtype=F32) / half)
    ar = row[:, None] * inv_freq[None, :]
    ac = col[:, None] * inv_freq[None, :]
    ang = jnp.concatenate([ar, ar, ac, ac], axis=-1)
    ang = jnp.concatenate([ang, jnp.zeros((ctx_len, DA_HEAD_DIM), F32)], axis=0)
    cos = jnp.tile(jnp.cos(ang), (1, 2))
    sin = jnp.tile(jnp.sin(ang), (1, 2))
    first = (jnp.arange(LANES) % half) < (half // 2)
    sin_a = jnp.where(first[None, :], -sin, 0.0)
    sin_b = jnp.where(first[None, :], 0.0, sin)
    return cos, sin_a, sin_b


def _rope_kernel(p_ref, cos_ref, sa_ref, sb_ref, o_ref):
    cos = cos_ref[...]
    sa = sa_ref[...]
    sb = sb_ref[...]
    quarter = DA_HEAD_DIM // 4
    for h in range(2 * DA_HEADS):
        sl = slice(h * LANES, (h + 1) * LANES)
        x = p_ref[:, sl].astype(F32)
        r = x * cos + pltpu.roll(x, LANES - quarter, 1) * sa + pltpu.roll(x, quarter, 1) * sb
        if h < DA_HEADS:
            r = r * (DA_HEAD_DIM ** -0.5)
        o_ref[:, sl] = r.astype(BF16)


def _rope(p, tables, dims):
    t = p.shape[0]
    tm = TOK_TILE
    nl, seq, ctx_len = dims["nl"], dims["seq"], dims["ctx"]

    def tab(i):
        return (jnp.where(i < nl // tm, i % (seq // tm), seq // tm + (i - nl // tm) % (ctx_len // tm)), 0)

    w = 2 * DA_WIDTH
    tspec = pl.BlockSpec((tm, LANES), tab)
    return pl.pallas_call(
        _rope_kernel,
        out_shape=jax.ShapeDtypeStruct((t, w), BF16),
        grid=(t // tm,),
        in_specs=[pl.BlockSpec((tm, w), lambda i: (i, 0)), tspec, tspec, tspec],
        out_specs=pl.BlockSpec((tm, w), lambda i: (i, 0)),
        compiler_params=_cparams("parallel"),
        name="rope_qk",
    )(p, *tables)


ATTN_ROW_BLOCK = 64


def _attn_kernel(*refs, tq, tk, n_lat_k, lam_init):
    if n_lat_k:
        lam_ref, subln_ref, q_ref, kc_ref, vc_ref, kl_ref, vl_ref, o_ref = refs[:8]
    else:
        lam_ref, subln_ref, q_ref, kc_ref, vc_ref = refs[:5]
        o_ref = refs[-8]
    qs_ref, s_ref, p_ref, m_ref, a_ref, acc_ref, vx_ref = refs[-7:]
    rb = ATTN_ROW_BLOCK
    n_ctx = kc_ref.shape[0]

    n_lat = n_lat_k * tk

    @pl.when(pl.program_id(2) == 0)
    def _():
        if n_lat_k:
            vx_ref[0:n_lat, 0:LANES] = vl_ref[...]
        vx_ref[n_lat:, 0:LANES] = vc_ref[...]
        vx_ref[:, LANES:2 * LANES] = jnp.ones((vx_ref.shape[0], LANES), BF16)

    q = q_ref[...]
    lane = lax.broadcasted_iota(jnp.int32, q.shape, 1)
    zero = jnp.zeros_like(q)
    qs_ref[0:tq, :] = jnp.where(lane < DA_HEAD_DIM, q, zero)
    qs_ref[tq:2 * tq, :] = jnp.where(lane >= DA_HEAD_DIM, q, zero)
    m_ref[...] = jnp.full(m_ref.shape, -1e30, F32)
    acc_ref[...] = jnp.zeros(acc_ref.shape, F32)

    def key_rows(t, nk):
        return pl.ds(t * tk if isinstance(t, int) else pl.multiple_of(t * tk, tk), nk)

    def keys(t):
        if isinstance(t, int) and t == n_lat_k:
            return kc_ref[...], n_ctx
        return kl_ref[key_rows(t, tk), :], tk

    def scores(t, buf):
        k, nk = keys(t)
        s_ref[buf, :, 0:nk] = lax.dot_general(qs_ref[...], k, (((1,), (1,)), ((), ())),
                                              preferred_element_type=F32)

    def softmax(buf, nk):
        reps = nk // LANES
        for r in range(2 * tq // rb):
            rows = slice(r * rb, (r + 1) * rb)
            m_prev = m_ref[rows, :]
            m_new = jnp.maximum(m_prev, jnp.max(s_ref[buf, rows, 0:nk], axis=-1, keepdims=True))
            a_ref[buf, rows, :] = jnp.exp(m_prev - m_new)
            m_ref[rows, :] = m_new
            p = jnp.exp(s_ref[buf, rows, 0:nk] - jnp.concatenate([m_new] * reps, axis=1))
            p_ref[buf, rows, 0:nk] = p.astype(BF16)

    def weighted(t, buf):
        nk = n_ctx if isinstance(t, int) and t == n_lat_k else tk
        vx = vx_ref[key_rows(t, nk), :]
        alpha = a_ref[buf]
        pv = jnp.dot(p_ref[buf, :, 0:nk], vx, preferred_element_type=F32)
        acc_ref[...] = jnp.concatenate([alpha, alpha], axis=1) * acc_ref[...] + pv

    def stage(t, buf):
        scores(t + 1, 1 - buf)
        softmax(buf, tk)
        weighted(t - 1, 1 - buf)

    n_static = n_lat_k + 1
    loop_pairs = (n_lat_k - 2) // 2 if (n_lat_k >= 4 and n_lat_k % 2 == 0) else 0
    scores(0, 0)
    t = 0
    while t < n_static:
        if t == 1 and loop_pairs:
            def body(i, carry):
                stage(1 + 2 * i, 1)
                stage(2 + 2 * i, 0)
                return carry
            lax.fori_loop(0, loop_pairs, body, 0)
            t += 2 * loop_pairs
            continue
        if t + 1 < n_static:
            scores(t + 1, (t + 1) % 2)
        softmax(t % 2, n_ctx if t == n_lat_k else tk)
        if t >= 1:
            weighted(t - 1, (t - 1) % 2)
        t += 1
    weighted(n_lat_k, n_lat_k % 2)

    lf = lam_ref[...]
    lam = (jnp.exp(jnp.sum(lf[0:1] * lf[1:2], axis=-1, keepdims=True))
           - jnp.exp(jnp.sum(lf[2:3] * lf[3:4], axis=-1, keepdims=True)) + lam_init)
    o = acc_ref[:, 0:LANES] / acc_ref[:, LANES:2 * LANES]
    out = o[:tq] - lam * o[tq:]
    y = _rms(out) * subln_ref[...] * (1.0 - lam_init)
    o_ref[...] = y.astype(BF16)


def _attn_call(qk, p, da_lambda, da_subln, lam_init, dims, *, tq, q_row0, n_q, rows, with_lat):
    batch, seq, ctx_len, nl = dims["batch"], dims["seq"], dims["ctx"], dims["nl"]
    tk = 512
    q_map = lambda b, h, i: (q_row0 // tq + b * n_q + i, h)
    o_map = lambda b, h, i: (b * n_q + i, h)
    kv_lat = lambda off: pl.BlockSpec((seq, LANES), lambda b, h, i: (b, off + h))
    kv_ctx = lambda off: pl.BlockSpec((ctx_len, LANES), lambda b, h, i: (nl // ctx_len + b, off + h))
    in_specs = [pl.BlockSpec((4, DA_HEAD_DIM), lambda b, h, i: (0, 0)),
                pl.BlockSpec((1, DA_V_DIM), lambda b, h, i: (0, 0)),
                pl.BlockSpec((tq, LANES), q_map),
                kv_ctx(DA_HEADS), kv_ctx(V_OFF // LANES)]
    args = [da_lambda, da_subln.reshape(1, DA_V_DIM), qk, qk, p]
    if with_lat:
        in_specs += [kv_lat(DA_HEADS), kv_lat(V_OFF // LANES)]
        args += [qk, p]
    n_keys = ctx_len + (seq if with_lat else 0)
    scratch = [pltpu.VMEM((2 * tq, LANES), BF16), pltpu.VMEM((2, 2 * tq, tk), F32), pltpu.VMEM((2, 2 * tq, tk), BF16),
               pltpu.VMEM((2 * tq, LANES), F32), pltpu.VMEM((2, 2 * tq, LANES), F32),
               pltpu.VMEM((2 * tq, 2 * LANES), F32), pltpu.VMEM((n_keys, 2 * LANES), BF16)]
    kern = functools.partial(_attn_kernel, tq=tq, tk=tk, n_lat_k=seq // tk if with_lat else 0, lam_init=lam_init)
    return pl.pallas_call(
        kern,
        out_shape=jax.ShapeDtypeStruct((rows, DA_WIDTH), BF16),
        grid=(batch, DA_HEADS, n_q),
        in_specs=in_specs,
        out_specs=pl.BlockSpec((tq, LANES), o_map),
        scratch_shapes=scratch,
        compiler_params=_cparams("parallel", "parallel", "arbitrary"),
        name="diff_attn" if with_lat else "diff_attn_ctx",
    )(*args)


def _attention(qk, p, da_lambda, da_subln, lam_init, with_ctx, dims):
    batch, seq, ctx_len, nl = dims["batch"], dims["seq"], dims["ctx"], dims["nl"]
    tq = _pick_tile(seq, (1024, 512, TOK_TILE))
    y = _attn_call(qk, p, da_lambda, da_subln, lam_init, dims, tq=tq, q_row0=0, n_q=seq // tq, rows=nl,
                   with_lat=True)
    if with_ctx:
        tqc = TOK_TILE
        y_ctx = _attn_call(qk, p, da_lambda, da_subln, lam_init, dims, tq=tqc, q_row0=nl, n_q=ctx_len // tqc,
                           rows=batch * ctx_len, with_lat=False)
        y = jnp.concatenate([y, y_ctx], axis=0)
    return y


def _softplus(x):
    return jnp.maximum(x, 0.0) + jnp.log(1.0 + jnp.exp(-jnp.abs(x)))


def _ssd_dt_kernel(h_ref, w_ref, b_ref, alog_ref, dd_ref):
    raw = lax.dot_general(h_ref[...], w_ref[...].astype(BF16), (((1,), (1,)), ((), ())), preferred_element_type=F32)
    n = 2 * SSD_HEADS
    dt = _softplus(raw + b_ref[...])
    lane = lax.broadcasted_iota(jnp.int32, dt.shape, 1)
    dta = pltpu.roll(dt, n, 1) * (-jnp.exp(alog_ref[...]))
    dd = jnp.where(lane < n, dt, jnp.where(lane < 2 * n, dta, 0.0))
    dd_ref[...] = dd


def _ssd_dt(h, w_in, layer, bias2, alog2):
    t, d = h.shape
    tm = TOK_TILE
    assert REF_DT_OFF % LANES == 0
    return pl.pallas_call(
        _ssd_dt_kernel,
        out_shape=jax.ShapeDtypeStruct((t, LANES), F32),
        grid=(t // tm,),
        in_specs=[pl.BlockSpec((tm, d), lambda i: (i, 0)),
                  pl.BlockSpec((None, LANES, d), lambda i: (layer, REF_DT_OFF // LANES, 0)),
                  pl.BlockSpec((1, LANES), lambda i: (0, 0)),
                  pl.BlockSpec((1, LANES), lambda i: (0, 0))],
        out_specs=pl.BlockSpec((tm, LANES), lambda i: (i, 0)),
        compiler_params=_cparams("parallel"),
        name="ssd_dt",
    )(h, w_in, bias2, alog2)


def _conv_kernel(prev_ref, x_ref, next_ref, w_ref, b_ref, o_ref, *, tm, n_lat_tiles, lat_per_seq, ctx_per_seq):
    i = pl.program_id(0)
    is_lat = i < n_lat_tiles
    pos = jnp.where(is_lat, i % lat_per_seq, (i - n_lat_tiles) % ctx_per_seq)
    per = jnp.where(is_lat, lat_per_seq, ctx_per_seq)
    keep_prev = (pos > 0).astype(F32)
    keep_next = (pos < per - 1).astype(F32)
    ext = jnp.concatenate([prev_ref[...].astype(F32) * keep_prev, x_ref[...].astype(F32),
                           next_ref[...].astype(F32) * keep_next], axis=0)
    n = tm + 2 * SUBLANES
    w = w_ref[...]
    acc = jnp.zeros((tm, ext.shape[1]), F32) + b_ref[...]
    for k in range(SSD_CONV):
        shift = (SSD_CONV // 2 - k) % n
        rolled = ext if shift == 0 else pltpu.roll(ext, shift, 0)
        acc = acc + rolled[SUBLANES:SUBLANES + tm] * w[k:k + 1, :]
    o_ref[...] = _silu(acc).astype(BF16)


def _ssd_conv(p, conv_w, conv_b, dims):
    t = p.shape[0]
    tm = TOK_TILE
    cb = 512
    col0 = XBC_OFF // cb
    r8 = tm // SUBLANES
    last8 = t // SUBLANES - 1
    kern = functools.partial(_conv_kernel, tm=tm, n_lat_tiles=dims["nl"] // tm, lat_per_seq=dims["seq"] // tm,
                             ctx_per_seq=dims["ctx"] // tm)
    return pl.pallas_call(
        kern,
        out_shape=jax.ShapeDtypeStruct((t, SSD_XBC), BF16),
        grid=(t // tm, SSD_XBC // cb),
        in_specs=[pl.BlockSpec((SUBLANES, cb), lambda i, j: (jnp.maximum(i * r8 - 1, 0), col0 + j)),
                  pl.BlockSpec((tm, cb), lambda i, j: (i, col0 + j)),
                  pl.BlockSpec((SUBLANES, cb), lambda i, j: (jnp.minimum((i + 1) * r8, last8), col0 + j)),
                  pl.BlockSpec((SSD_CONV, cb), lambda i, j: (0, j)),
                  pl.BlockSpec((1, cb), lambda i, j: (0, j))],
        out_specs=pl.BlockSpec((tm, cb), lambda i, j: (i, j)),
        compiler_params=_cparams("parallel", "parallel"),
        name="ssd_conv",
    )(p, p, p, conv_w, conv_b.reshape(1, SSD_XBC))


def _split3(x):
    hi = x.astype(BF16)
    rest = x - hi.astype(F32)
    mid = rest.astype(BF16)
    return hi, mid, (rest - mid.astype(F32)).astype(BF16)


def _ssd_scan_kernel(*refs, direction, final, ncc, write_ctx):
    if final:
        xbc_ref, dd_ref, yf_ref, z_ref, dskip_ref, norm_ref, o_ref, h_ref = refs
    else:
        xbc_ref, dd_ref, o_ref, h_ref = refs
    j = pl.program_id(1)
    q = SSD_CHUNK
    gw = SSD_HPG * SSD_HEAD_DIM

    @pl.when(j == 0)
    def _():
        h_ref[...] = jnp.zeros_like(h_ref)

    xbc = xbc_ref[...]
    xs = xbc[:, :SSD_WIDTH].astype(F32)
    bm = xbc[:, SSD_WIDTH:SSD_WIDTH + SSD_GROUPS * SSD_STATE]
    cm = xbc[:, SSD_WIDTH + SSD_GROUPS * SSD_STATE:]
    dd = dd_ref[...]

    ii =lax.broadcasted_iota(jnp.int32, (q, q), 0)
    jj = lax.broadcasted_iota(jnp.int32, (q, q), 1)
    if direction == 0:
        mask = jj <= ii
        last = q - 1
    else:
        mask = jj >= ii
        last = 0
    tri = jnp.where(mask, 1.0, 0.0).astype(BF16)
    ac = sum(jnp.dot(tri, piece, preferred_element_type=F32) for piece in _split3(dd))
    ac_t = ac.T

    er = lax.broadcasted_iota(jnp.int32, (LANES, SSD_WIDTH), 0)
    ec = lax.broadcasted_iota(jnp.int32, (LANES, SSD_WIDTH), 1) // SSD_HEAD_DIM
    dt_col = direction * SSD_HEADS
    ac_col = 2 * SSD_HEADS + direction * SSD_HEADS
    e_dt = jnp.where(er == ec + dt_col, 1.0, 0.0).astype(BF16)
    e_ac = jnp.where(er == ec + ac_col, 1.0, 0.0).astype(BF16)
    dt_exp = sum(jnp.dot(piece, e_dt, preferred_element_type=F32) for piece in _split3(dd))
    ac_exp = sum(jnp.dot(piece, e_ac, preferred_element_type=F32) for piece in _split3(ac))
    ac_last = ac_exp[last:last + 1, :]
    eac = jnp.exp(ac_exp)
    dec_end = jnp.exp(ac_last - ac_exp)
    chunk_dec = jnp.exp(ac_last)
    xdt = xs * dt_exp
    xdt_b = xdt.astype(BF16)
    xde_b = (xdt * dec_end).astype(BF16)

    lane = lax.broadcasted_iota(jnp.int32, (q, LANES), 1)
    left = lane < SSD_HEAD_DIM
    zero_b = jnp.zeros((q, LANES), BF16)
    pieces = []
    for g in range(SSD_GROUPS):
        bg = bm[:, g * SSD_STATE:(g + 1) * SSD_STATE]
        cg = cm[:, g * SSD_STATE:(g + 1) * SSD_STATE]
        cb = lax.dot_general(cg, bg, (((1,), (1,)), ((), ())), preferred_element_type=F32)
        h_t = h_ref[g]
        y_off = jnp.dot(cg, h_t.astype(BF16), preferred_element_type=F32) * eac[:, g * gw:(g + 1) * gw]
        bg_t = bg.astype(F32).T.astype(BF16)
        s_t = jnp.dot(bg_t, xde_b[:, g * gw:(g + 1) * gw], preferred_element_type=F32)
        h_ref[g] = h_t * chunk_dec[:, g * gw:(g + 1) * gw] + s_t
        for kp in range(SSD_HPG // 2):
            k0 = g * SSD_HPG + 2 * kp
            ms = []
            for k in (k0, k0 + 1):
                c = ac_col + k
                seg = ac[:, c:c + 1] - ac_t[c:c + 1, :]
                ms.append((cb * jnp.exp(jnp.where(mask, seg, -1e30))).astype(BF16))
            xp = xdt_b[:, k0 * SSD_HEAD_DIM:k0 * SSD_HEAD_DIM + LANES]
            y_diag = (jnp.dot(ms[0], jnp.where(left, xp, zero_b), preferred_element_type=F32)
                      + jnp.dot(ms[1], jnp.where(left, zero_b, xp), preferred_element_type=F32))
            pieces.append(y_diag + y_off[:, kp * LANES:(kp + 1) * LANES])
    y = jnp.concatenate(pieces, axis=1)

    def emit():
        if final:
            yt = y + yf_ref[...] + dskip_ref[...] * xs
            yt = yt * _silu(z_ref[...].astype(F32))
            o_ref[...] = (_rms(yt) * norm_ref[...]).astype(BF16)
        else:
            o_ref[...] = y

    if write_ctx:
        emit()
    else:
        pl.when(j >= ncc)(emit)


def _ssd_scan(xbc, dd, direction, dims, with_ctx, final_args=None):
    batch, seq, ctx_len, nl = dims["batch"], dims["seq"], dims["ctx"], dims["nl"]
    q = SSD_CHUNK
    ncl, ncc = seq // q, ctx_len // q
    final = final_args is not None
    rows = nl + (batch * ctx_len if with_ctx else 0)

    def chunk(b, j):
        jc = j if direction == 0 else ncc - 1 - j
        jl = (j - ncc) if direction == 0 else ncl - 1 - (j - ncc)
        return jnp.where(j < ncc, nl // q + b * ncc + jc, b * ncl + jl)

    def out_chunk(b, j):
        if with_ctx:
            return chunk(b, j)
        return chunk(b, jnp.maximum(j, ncc))

    in_specs = [pl.BlockSpec((q, SSD_XBC), lambda b, j: (chunk(b, j), 0)),
                pl.BlockSpec((q, LANES), lambda b, j: (chunk(b, j), 0))]
    args = [xbc, dd]
    if final:
        yf, p, dskip, norm = final_args
        in_specs += [pl.BlockSpec((q, SSD_WIDTH), lambda b, j: (out_chunk(b, j), 0)),
                     pl.BlockSpec((q, SSD_WIDTH), lambda b, j: (chunk(b, j), Z_OFF // SSD_WIDTH)),
                     pl.BlockSpec((1, SSD_WIDTH), lambda b, j: (0, 0)),
                     pl.BlockSpec((1, SSD_WIDTH), lambda b, j: (0, 0))]
        args += [yf, p, dskip, norm]
    return pl.pallas_call(
        functools.partial(_ssd_scan_kernel, direction=direction, final=final, ncc=ncc, write_ctx=with_ctx),
        out_shape=jax.ShapeDtypeStruct((rows, SSD_WIDTH), BF16 if final else F32),
        grid=(batch, ncc + ncl),
        in_specs=in_specs,
        out_specs=pl.BlockSpec((q, SSD_WIDTH), lambda b, j: (out_chunk(b, j), 0)),
        scratch_shapes=[pltpu.VMEM((SSD_GROUPS, SSD_STATE, SSD_HPG * SSD_HEAD_DIM), F32)],
        compiler_params=_cparams("parallel", "arbitrary"),
        name="ssd_scan_bwd" if direction else "ssd_scan_fwd",
    )(*args)


S5_BLOCK_GROUPS = LANES // S5_GROUP
S5_BLOCKS = S5_GROUPS // S5_BLOCK_GROUPS
S5_BS = S5_BLOCK_GROUPS * S5_STATE
S5_LAG_ROWS = 24


def _expm1(x):
    poly = 1.0 + x / 10.0
    for n in range(9, 1, -1):
        poly = 1.0 + (x / n) * poly
    return jnp.where(jnp.abs(x) < 0.35, x * poly, jnp.exp(x) - 1.0)


def _s5_params(lam_re, lam_im, log_step, b_re, b_im, c_re, c_im, s5_d):
    nb, gl = S5_BLOCKS, S5_BLOCK_GROUPS
    rows = [lam_re[0], lam_im[0], lam_re[1], lam_im[1],
            jnp.repeat(log_step[0], S5_STATE), jnp.repeat(log_step[1], S5_STATE)]
    rows = [r.reshape(nb, S5_BS) for r in rows] + [jnp.zeros((nb, S5_BS), F32)] * 2
    lam_rows = jnp.stack(rows, axis=1)
    eye = jnp.eye(gl, dtype=F32)

    def bd_in(b):
        return jnp.einsum("jgpe,gh->jgehp", b.reshape(nb, gl, S5_STATE, S5_GROUP), eye).reshape(nb, LANES, S5_BS)

    def bd_out(c):
        return jnp.einsum("jgfp,gh->jgfhp", c.reshape(nb, gl, S5_GROUP, S5_STATE), eye).reshape(nb, LANES, S5_BS)

    b_bd = jnp.stack([bd_in(b_re), bd_in(b_im)], axis=1)
    ct_bd = jnp.stack([bd_out(c_re), bd_out(c_im)], axis=1)
    return lam_rows, b_bd, ct_bd, s5_d.reshape(nb, 1, LANES)


def _gelu_tanh(x):
    return 0.5 * x * (1.0 + jnp.tanh(math.sqrt(2.0 / math.pi) * (x + 0.044715 * x * x * x)))


def _s5_kernel(u_ref, lam_ref, b_ref, ct_ref, d_ref, o_ref,
               x_ref, w_ref, m_ref, s_ref, y_ref, taps_ref, pw_ref, bb_ref, *, batch, ncl, ncc):
    tc = S5_CHUNK
    nch = x_ref.shape[0]
    sw = S5_BS
    nt = (((1,), (1,)), ((), ()))

    for s in range(tc):
        x_ref[:, s * LANES:(s + 1) * LANES] = u_ref[pl.ds(s, nch, stride=tc), :].astype(BF16)

    rows = lam_ref[0]
    kk = lax.broadcasted_iota(jnp.int32, (S5_LAG_ROWS, sw), 0).astype(F32)
    b_re, b_im = b_ref[0, 0], b_ref[0, 1]
    for d in range(2):
        l_re, l_im = rows[2 * d:2 * d + 1], rows[2 * d + 1:2 * d + 2]
        delta = jnp.exp(rows[4 + d:5 + d])
        lr, li = l_re * delta, l_im * delta
        mag = jnp.exp(kk * lr)
        pw_ref[2 * d] = mag * jnp.cos(kk * li)
        pw_ref[2 * d + 1] = mag * jnp.sin(kk * li)
        xr = _expm1(lr) * jnp.cos(li) - 2.0 * jnp.sin(0.5 * li) ** 2
        xi = jnp.exp(lr) * jnp.sin(li)
        den = l_re * l_re + l_im * l_im
        co_re = (xr * l_re + xi * l_im) / den
        co_im = (xi * l_re - xr * l_im) / den
        bb_ref[2 * d] = co_re * b_re - co_im * b_im
        bb_ref[2 * d + 1] = co_re * b_im + co_im * b_re

    def power(d, lag):
        return pw_ref[2 * d, lag:lag + 1, :], pw_ref[2 * d + 1, lag:lag + 1, :]

    ct_re, ct_im = ct_ref[0, 0], ct_ref[0, 1]
    ct_re_b, ct_im_b = ct_re.astype(BF16), ct_im.astype(BF16)
    for d in range(2):
        tp = []
        for part in range(2):
            q = 2 * d + part
            for s in range(tc):
                pr, pi = power(d, tc - 1 - s if d == 0 else s)
                if part == 0:
                    tile = pr * bb_ref[2 * d] - pi * bb_ref[2 * d + 1]
                else:
                    tile = pr * bb_ref[2 * d + 1] + pi * bb_ref[2 * d]
                w_ref[q, s * LANES:(s + 1) * LANES, :] = tile.astype(BF16)
            w = w_ref[q]
            s_ref[:, q * sw:(q + 1) * sw] = jnp.dot(x_ref[...], w, preferred_element_type=F32)
            tp.append(lax.dot_general(w, ct_re_b if part == 0 else ct_im_b, nt, preferred_element_type=F32))
        taps_ref[d] = tp[0] - tp[1]

    ri = lax.broadcasted_iota(jnp.int32, (LANES, LANES), 0)
    ci = lax.broadcasted_iota(jnp.int32, (LANES, LANES), 1)
    skip = jnp.where(ri == ci, d_ref[0], 0.0)
    for s in range(tc):
        for t in range(tc):
            if t >= s:
                lag_tile = tc - 1 - (t - s)
                tile = taps_ref[0, lag_tile * LANES:(lag_tile + 1) * LANES, :]
                if t == s:
                    tile = tile + taps_ref[1, 0:LANES, :] + skip
            else:
                tile = taps_ref[1, (s - t) * LANES:(s - t + 1) * LANES, :]
            m_ref[s * LANES:(s + 1) * LANES, t * LANES:(t + 1) * LANES] = tile.astype(BF16)

    af_re, af_im = power(0, tc)
    ab_re, ab_im = power(1, tc)

    def advance(row, h, dir_off, ar, ai):
        h_re, h_im = h
        s_re = s_ref[pl.ds(row, 1), dir_off:dir_off + sw]
        s_im = s_ref[pl.ds(row, 1), dir_off + sw:dir_off + 2 * sw]
        s_ref[pl.ds(row, 1), dir_off:dir_off + sw] = h_re
        s_ref[pl.ds(row, 1), dir_off + sw:dir_off + 2 * sw] = h_im
        return ar * h_re - ai * h_im + s_re, ar * h_im + ai * h_re + s_im

    def sweep(first_chunk, n):
        def body(i, c):
            return tuple((advance(first_chunk(b) + i, c[b][0], 0, af_re, af_im),
                          advance(first_chunk(b) + n - 1 - i, c[b][1], 2 * sw, ab_re, ab_im)) for b in range(batch))
        return body

    zero = jnp.zeros((1, sw), F32)
    c = lax.fori_loop(0, ncc, sweep(lambda b: batch * ncl + b * ncc, ncc), (((zero, zero), (zero, zero)),) * batch)
    lax.fori_loop(0, ncl, sweep(lambda b: b * ncl, ncl), c)

    y = jnp.dot(x_ref[...], m_ref[...], preferred_element_type=F32)
    for q in range(4):
        d, part = divmod(q, 2)
        for t in range(tc):
            pr, pi = power(d, t + 1 if d == 0 else tc - t)
            tile = ct_re * pr - ct_im * pi if part == 0 else -(ct_re * pi + ct_im * pr)
            w_ref[q, t * LANES:(t + 1) * LANES, :] = tile.astype(BF16)
        h = s_ref[:, q * sw:(q + 1) * sw].astype(BF16)
        y = y + lax.dot_general(h, w_ref[q], nt, preferred_element_type=F32)
    y_ref[...] = y
    for t in range(tc):
        o_ref[pl.ds(t, nch, stride=tc), :] = _gelu_tanh(y_ref[:, t * LANES:(t + 1) * LANES])


def _s5(u, params, dims):
    lam_rows, b_bd, ct_bd, d_skip = params
    t = u.shape[0]
    nch = t // S5_CHUNK
    ncl, ncc = dims["seq"] // S5_CHUNK, dims["ctx"] // S5_CHUNK
    width = S5_CHUNK * LANES
    pspec = pl.BlockSpec((1, 2, LANES, S5_BS), lambda i: (i, 0, 0, 0))
    return pl.pallas_call(
        functools.partial(_s5_kernel, batch=dims["batch"], ncl=ncl, ncc=ncc),
        out_shape=jax.ShapeDtypeStruct((t, S5_WIDTH), F32),
        grid=(S5_BLOCKS,),
        in_specs=[pl.BlockSpec((t, LANES), lambda i: (0, i)),
                  pl.BlockSpec((1, SUBLANES, S5_BS), lambda i: (i, 0, 0)), pspec, pspec,
                  pl.BlockSpec((1, 1, LANES), lambda i: (i, 0, 0))],
        out_specs=pl.BlockSpec((t, LANES), lambda i: (0, i)),
        scratch_shapes=[pltpu.VMEM((nch, width), BF16),
                        pltpu.VMEM((4, width, S5_BS), BF16),
                        pltpu.VMEM((width, width), BF16),
                        pltpu.VMEM((nch, 4 * S5_BS), F32),
                        pltpu.VMEM((nch, width), F32),
                        pltpu.VMEM((2, width, LANES), F32),
                        pltpu.VMEM((4, S5_LAG_ROWS, S5_BS), F32),
                        pltpu.VMEM((4, LANES, S5_BS), F32)],
        compiler_params=_cparams("parallel"),
        name="s5_scan",
    )(u, lam_rows, b_bd, ct_bd, d_skip)


def _glu_kernel(x_ref, wa_ref, wb_ref, ba_ref, bb_ref, o_ref, wab_ref, wbb_ref):
    @pl.when(pl.program_id(1) == 0)
    def _():
        wab_ref[...] = wa_ref[...].astype(BF16)
        wbb_ref[...] = wb_ref[...].astype(BF16)

    x = x_ref[...].astype(BF16)
    a = jnp.dot(x, wab_ref[...], preferred_element_type=F32) + ba_ref[...]
    b = jnp.dot(x, wbb_ref[...], preferred_element_type=F32) + bb_ref[...]
    o_ref[...] = (a * _sigmoid(b)).astype(BF16)


def _glu(x, w, bias, layer, rows, tm):
    k = x.shape[1]
    tn = W_TILE
    nb = S5_WIDTH // tn
    return pl.pallas_call(
        _glu_kernel,
        out_shape=jax.ShapeDtypeStruct((rows, S5_WIDTH), BF16),
        grid=(nb, rows // tm),
        in_specs=[pl.BlockSpec((tm, k), lambda j, i: (i, 0)),
                  pl.BlockSpec((None, k, tn), lambda j, i: (layer, 0, j)),
                  pl.BlockSpec((None, k, tn), lambda j, i: (layer, 0, nb + j)),
                  pl.BlockSpec((None, 1, tn), lambda j, i: (layer, 0, j)),
                  pl.BlockSpec((None, 1, tn), lambda j, i: (layer, 0, nb + j))],
        out_specs=pl.BlockSpec((tm, tn), lambda j, i: (i, j)),
        scratch_shapes=[pltpu.VMEM((k, tn), BF16), pltpu.VMEM((k, tn), BF16)],
        compiler_params=_cparams("parallel", "arbitrary"),
        name="s5_glu",
    )(x, w, w, bias, bias)


def _merge_kernel(ya_ref, yb_ref, yc_ref, wa_ref, wb_ref, wc_ref, ga_ref, gb_ref, gc_ref, o_ref, wbf_ref):
    @pl.when(pl.program_id(1) == 0)
    def _():
        for n, w_ref in enumerate((wa_ref, wb_ref, wc_ref)):
            wbf_ref[n] = w_ref[0].astype(BF16)

    acc = None
    for n, (y_ref, g_ref) in enumerate(((ya_ref, ga_ref), (yb_ref, gb_ref), (yc_ref, gc_ref))):
        br = jnp.dot(y_ref[...], wbf_ref[n], preferred_element_type=F32)
        term = _sigmoid(g_ref[...].astype(F32)) * br
        acc = term if acc is None else acc + term
    o_ref[...] = acc.astype(BF16)


def _merge(ya, yb, yc, w_branch, layer, gates, rows, tm):
    k = ya.shape[1]
    d = w_branch.shape[3]
    tn = W_TILE
    gstep = d // tn
    yspec = pl.BlockSpec((tm, k), lambda j, i: (i, 0))
    wspec = lambda n: pl.BlockSpec((None, 1, k, tn), lambda j, i: (layer, n, 0, j))
    gspec = lambda n: pl.BlockSpec((tm, tn), lambda j, i: (i, n * gstep + j))
    return pl.pallas_call(
        _merge_kernel,
        out_shape=jax.ShapeDtypeStruct((rows, d), BF16),
        grid=(d // tn, rows // tm),
        in_specs=[yspec, yspec, yspec, wspec(0), wspec(1), wspec(2), gspec(0), gspec(1), gspec(2)],
        out_specs=pl.BlockSpec((tm, tn), lambda j, i: (i, j)),
        scratch_shapes=[pltpu.VMEM((N_BRANCH, k, tn), BF16)],
        compiler_params=_cparams("parallel", "arbitrary"),
        name="branch_merge",
    )(ya, yb, yc, w_branch, w_branch, w_branch, gates, gates, gates)


def _out_proj_kernel(*refs, n_lat):
    g_ref, w_ref, *x_refs, mod_ref, npost_ref, npre_ref, xo_ref, ho_ref = refs
    o = jnp.dot(g_ref[...], w_ref[...], preferred_element_type=F32)
    gate = mod_ref[0, 2:3, :]
    xn = _stream_tile(x_refs, n_lat) + gate * (_rms(o) * npost_ref[...])
    xo_ref[...] = xn
    shift = mod_ref[0, 3:4, :]
    scale = mod_ref[0, 4:5, :]
    ho_ref[...] = (_rms(xn) * npre_ref[...] * (1.0 + scale) + shift).astype(BF16)


def _out_proj(g, w_out, xs, mod, npost, npre, dims):
    r, d = g.shape
    tm = TOK_TILE
    seg = functools.partial(_seg_of_tile, tm=tm, n_lat=dims["nl"], seq=dims["seq"], batch=dims["batch"])
    row = pl.BlockSpec((tm, d), lambda i: (i, 0))
    vec = pl.BlockSpec((1, d), lambda i: (0, 0))
    return pl.pallas_call(
        functools.partial(_out_proj_kernel, n_lat=dims["nl"] // tm),
        out_shape=(jax.ShapeDtypeStruct((r, d), F32), jax.ShapeDtypeStruct((r, d), BF16)),
        grid=(r // tm,),
        in_specs=[row, pl.BlockSpec((d, d), lambda i: (0, 0))]
        + _stream_specs(xs, tm, d, dims["nl"])
        + [pl.BlockSpec((1, 6, d), lambda i: (seg(i), 0, 0)), vec, vec],
        out_specs=(row, row),
        compiler_params=_cparams("parallel"),
        name="out_proj",
    )(g, w_out, *xs, mod, npost.reshape(1, d), npre.reshape(1, d))


def _ffn_up_kernel(h_ref, wg_ref, wu_ref, o_ref, wgb_ref, wub_ref):
    @pl.when(pl.program_id(1) == 0)
    def _():
        wgb_ref[...] = wg_ref[...].astype(BF16)
        wub_ref[...] = wu_ref[...].astype(BF16)

    h = h_ref[...]
    a = jnp.dot(h, wgb_ref[...], preferred_element_type=F32)
    b = jnp.dot(h, wub_ref[...], preferred_element_type=F32)
    o_ref[...] = (_silu(a) * b).astype(BF16)


def _ffn_up(h, wg, wu, layer, tm):
    r, d = h.shape
    f = wg.shape[2]
    tn = W_TILE
    wspec = pl.BlockSpec((None, d, tn), lambda j, i: (layer, 0, j))
    return pl.pallas_call(
        _ffn_up_kernel,
        out_shape=jax.ShapeDtypeStruct((r, f), BF16),
        grid=(f // tn, r // tm),
        in_specs=[pl.BlockSpec((tm, d), lambda j, i: (i, 0)), wspec, wspec],
        out_specs=pl.BlockSpec((tm, tn), lambda j, i: (i, j)),
        scratch_shapes=[pltpu.VMEM((d, tn), BF16), pltpu.VMEM((d, tn), BF16)],
        compiler_params=_cparams("parallel", "arbitrary"),
        name="ffn_up",
    )(h, wg, wu)


def _ffn_down_kernel(a_ref, w_ref, x_ref, mod_ref, npost_ref, o_ref, acc_ref):
    k = pl.program_id(1)

    @pl.when(k == 0)
    def _():
        acc_ref[...] = jnp.zeros_like(acc_ref)

    acc_ref[...] += jnp.dot(a_ref[...], w_ref[...], preferred_element_type=F32)

    @pl.when(k == pl.num_programs(1) - 1)
    def _():
        gate = mod_ref[0, 5:6, :]
        o_ref[...] = x_ref[...] + gate * (_rms(acc_ref[...]) * npost_ref[...])


def _ffn_down(act, wd, x, mod, npost, dims):
    r, f = act.shape
    d = wd.shape[1]
    tm = TOK_TILE * 2
    nk = 4
    tk = f // nk
    seg = functools.partial(_seg_of_tile, tm=tm, n_lat=dims["nl"], seq=dims["seq"], batch=dims["batch"])
    row = pl.BlockSpec((tm, d), lambda i, k: (i, 0))
    return pl.pallas_call(
        _ffn_down_kernel,
        out_shape=jax.ShapeDtypeStruct((r, d), F32),
        grid=(r // tm, nk),
        in_specs=[pl.BlockSpec((tm, tk), lambda i, k: (i, k)),
                  pl.BlockSpec((tk, d), lambda i, k: (k, 0)),
                  row,
                  pl.BlockSpec((1, 6, d), lambda i, k: (seg(i), 0, 0)),
                  pl.BlockSpec((1, d), lambda i, k: (0, 0))],
        out_specs=row,
        scratch_shapes=[pltpu.VMEM((tm, d), F32)],
        compiler_params=_cparams("parallel", "arbitrary"),
        name="ffn_down",
    )(act, wd, x, mod, npost.reshape(1, d))


def _pick_tile(rows, cands):
    for c in cands:
        if rows % c == 0:
            return c
    raise ValueError(f"no tile in {cands} divides {rows}")


def kernel(x, c, ctx, c_ctx, ada_w, ada_b, norm_mix_pre, norm_mix_post, norm_ffn_pre, norm_ffn_post, w_in, da_lambda, da_subln, ssd_conv_w, ssd_conv_b, ssd_dt_bias, ssd_a_log, ssd_d, ssd_norm, s5_lam_re, s5_lam_im, s5_log_step, s5_b_re, s5_b_im, s5_c_re, s5_c_im, s5_d, s5_glu_w, s5_glu_b, w_branch, w_out, ffn_w_gate, ffn_w_up, ffn_w_down):
    batch, seq, d = x.shape
    ctx_len = ctx.shape[1]
    depth = ada_w.shape[0]
    nl, nc = batch * seq, batch * ctx_len
    dims = dict(batch=batch, seq=seq, ctx=ctx_len, nl=nl, nc=nc)
    assert batch < MOD_ROWS and seq % TOK_TILE == 0 and ctx_len % TOK_TILE == 0 and seq % ctx_len == 0
    assert nc % (2 * TOK_TILE) == 0 and seq % GRID_W == 0

    xs = (x.reshape(nl, d), ctx.reshape(nc, d))
    cc =jnp.concatenate([c, c_ctx[None], jnp.zeros((MOD_ROWS - batch - 1, d), F32)], axis=0)
    mod_all = _ada(cc, ada_w, ada_b).reshape(depth, MOD_ROWS, 6, d)
    rope_tabs = _rope_tables(seq, ctx_len)
    n_dt = 2 * SSD_HEADS
    pad = lambda v, before: jnp.concatenate(
        [jnp.zeros((1, before), F32), v.reshape(1, n_dt), jnp.zeros((1, LANES - n_dt - before), F32)], axis=1)
    tm_all = _pick_tile(nl + nc, (2176, 1088, 512, 256))
    w_in_t = jnp.swapaxes(w_in, 1, 2)

    for l in range(depth):
        last = l == depth - 1
        with_ctx = not last
        lam_init = 0.8 - 0.6 * math.exp(-0.3 * l)
        mod = mod_all[l]
        rows = nl + nc if with_ctx else nl
        tm = _pick_tile(rows, (1088, 1024, 512))

        h = _norm_mod(xs, norm_mix_pre[l], mod, 0, dims)
        p = _proj(h, w_in_t, l, 0, REF_DT_OFF, BF16, tm_all, "in_proj_main")
        u = _proj(h, w_in_t, l, REF_U_OFF, S5_WIDTH, F32, tm_all, "in_proj_u")
        gates = _proj(h, w_in_t, l, REF_U_OFF + S5_WIDTH, N_BRANCH * d, BF16, tm_all, "in_proj_gates")

        qk = _rope(p, rope_tabs, dims)
        y_attn = _attention(qk, p, da_lambda[l], da_subln[l], lam_init, with_ctx, dims)

        dd = _ssd_dt(h, w_in_t, l, pad(ssd_dt_bias[l], 0), pad(ssd_a_log[l], n_dt))
        xbc = _ssd_conv(p, ssd_conv_w[l], ssd_conv_b[l], dims)
        y_f = _ssd_scan(xbc, dd, 0, dims, with_ctx)
        dskip = jnp.repeat(ssd_d[l], SSD_HEAD_DIM).reshape(1, SSD_WIDTH)
        y_ssd = _ssd_scan(xbc, dd, 1, dims, with_ctx,
                          final_args=(y_f, p, dskip, ssd_norm[l].reshape(1, SSD_WIDTH)))

        s5p = _s5_params(s5_lam_re[l], s5_lam_im[l], s5_log_step[l], s5_b_re[l], s5_b_im[l],
                         s5_c_re[l], s5_c_im[l], s5_d[l])
        yg = _s5(u, s5p, dims)
        y_s5 = _glu(yg, s5_glu_w, s5_glu_b.reshape(depth, 1, 2 * S5_WIDTH), l, rows, tm)

        g = _merge(y_attn, y_ssd, y_s5, w_branch, l, gates, rows, tm)
        xt, h2 = _out_proj(g, _cast_bf16(w_out, l), xs, mod, norm_mix_post[l], norm_ffn_pre[l], dims)
        act = _ffn_up(h2, ffn_w_gate, ffn_w_up, l, tm)
        xt = _ffn_down(act, _cast_bf16(ffn_w_down, l), xt, mod, norm_ffn_post[l], dims)
        xs = (xt,)

    return xt[:nl].reshape(batch, seq, d)
```

```python
import functools
import math

import jax
import jax.numpy as jnp
from jax import lax
from jax.experimental import pallas as pl
from jax.experimental.pallas import tpu as pltpu

F32 = jnp.float32
BF16 = jnp.bfloat16
HIGHEST = lax.Precision.HIGHEST

GRID_W = 64
N_BRANCH = 3
DA_HEADS = 8
DA_HEAD_DIM = 64
DA_V_DIM = 2 * DA_HEAD_DIM
DA_WIDTH = DA_HEADS * DA_V_DIM
ROPE_THETA = 10000.0
SSD_HEADS = 16
SSD_HEAD_DIM = 64
SSD_GROUPS = 2
SSD_HPG = SSD_HEADS // SSD_GROUPS
SSD_STATE = 128
SSD_WIDTH = SSD_HEADS * SSD_HEAD_DIM
SSD_XBC = SSD_WIDTH + 2 * SSD_GROUPS * SSD_STATE
SSD_CONV = 5
SSD_CHUNK = 128
S5_GROUP = 16
S5_GROUPS = 64
S5_WIDTH = S5_GROUPS * S5_GROUP
S5_STATE = 64
S5_CHUNK = 16
RMS_EPS = 1e-6

V_OFF = 2 * DA_WIDTH
Z_OFF = 3 * DA_WIDTH
XBC_OFF = 4 * DA_WIDTH
REF_DT_OFF = XBC_OFF + SSD_XBC
REF_U_OFF = REF_DT_OFF + 2 * SSD_HEADS

LANES = 128
SUBLANES = 8
VMEM_LIMIT_BYTES = 52 * 1024 * 1024
MOD_ROWS = 8

TOK_TILE = 256


def _cparams(*sem):
    return pltpu.CompilerParams(dimension_semantics=sem, vmem_limit_bytes=VMEM_LIMIT_BYTES)


def _rms(x):
    return x * lax.rsqrt(jnp.mean(x * x, axis=-1, keepdims=True) + RMS_EPS)


def _sigmoid(x):
    return 1.0 / (1.0 + jnp.exp(-x))


def _silu(x):
    return x * _sigmoid(x)


def _seg_of_tile(i, tm, n_lat, seq, batch):
    return jnp.where(i < n_lat // tm, i // (seq // tm), batch)


def _ada_kernel(c_ref, w_ref, b_ref, o_ref):
    c = c_ref[...]
    o_ref[0] = jnp.dot(_silu(c), w_ref[0], precision=HIGHEST, preferred_element_type=F32) + b_ref[0]


def _ada(cc, ada_w, ada_b):
    depth, d, n = ada_w.shape
    tn = 2048
    return pl.pallas_call(
        _ada_kernel,
        out_shape=jax.ShapeDtypeStruct((depth, MOD_ROWS, n), F32),
        grid=(depth, n // tn),
        in_specs=[pl.BlockSpec((MOD_ROWS, d), lambda l, j: (0, 0)),
                  pl.BlockSpec((1, d, tn), lambda l, j: (l, 0, j)),
                  pl.BlockSpec((1, 1, tn), lambda l, j: (l, 0, j))],
        out_specs=pl.BlockSpec((1, MOD_ROWS, tn), lambda l, j: (l, 0, j)),
        compiler_params=_cparams("parallel", "parallel"),
        name="ada_mod",
    )(cc, ada_w, ada_b.reshape(depth, 1, n))


def _stream_specs(xs, tm, d, nl):
    n_lat = nl // tm
    if len(xs) == 1:
        return [pl.BlockSpec((tm, d), lambda i: (i, 0))]
    return [pl.BlockSpec((tm, d), lambda i: (jnp.minimum(i, n_lat - 1), 0)),
            pl.BlockSpec((tm, d), lambda i: (jnp.maximum(i - n_lat, 0), 0))]


def _stream_tile(x_refs, n_lat):
    if len(x_refs) == 1:
        return x_refs[0][...]
    return jnp.where(pl.program_id(0) < n_lat, x_refs[0][...], x_refs[1][...])


def _norm_mod_kernel(*refs, shift_idx, n_lat):
    *x_refs, g_ref, mod_ref, o_ref = refs
    y = _rms(_stream_tile(x_refs, n_lat)) * g_ref[...]
    shift = mod_ref[0, shift_idx:shift_idx + 1, :]
    scale = mod_ref[0, shift_idx + 1:shift_idx + 2, :]
    o_ref[...] = (y * (1.0 + scale) + shift).astype(BF16)


def _norm_mod(xs, g, mod, shift_idx, dims):
    d = xs[0].shape[1]
    t = dims["nl"] + dims["nc"]
    tm = TOK_TILE
    seg = functools.partial(_seg_of_tile, tm=tm, n_lat=dims["nl"], seq=dims["seq"], batch=dims["batch"])
    return pl.pallas_call(
        functools.partial(_norm_mod_kernel, shift_idx=shift_idx, n_lat=dims["nl"] // tm),
        out_shape=jax.ShapeDtypeStruct((t, d), BF16),
        grid=(t // tm,),
        in_specs=_stream_specs(xs, tm, d, dims["nl"]) + [
            pl.BlockSpec((1, d), lambda i: (0, 0)),
            pl.BlockSpec((1, 6, d), lambda i: (seg(i), 0, 0))],
        out_specs=pl.BlockSpec((tm, d), lambda i: (i, 0)),
        compiler_params=_cparams("parallel"),
        name="norm_mod",
    )(*xs, g.reshape(1, d), mod)


W_TILE = 512


def _shifted_rows(w0, w1, shift):
    return jnp.concatenate([w0[shift:], w1[:shift]], axis=0)


def _proj_kernel(*refs, shift):
    if shift:
        x_ref, w0_ref, w1_ref, o_ref, wb_ref = refs
    else:
        x_ref, w0_ref, o_ref, wb_ref = refs

    @pl.when(pl.program_id(1) == 0)
    def _():
        w = _shifted_rows(w0_ref[...], w1_ref[...], shift) if shift else w0_ref[...]
        wb_ref[...] = w.T.astype(BF16)

    o_ref[...] = jnp.dot(x_ref[...], wb_ref[...], preferred_element_type=F32).astype(o_ref.dtype)


def _proj(x, w_t, layer, col0, n_cols, out_dtype, tm, name):
    m, k = x.shape
    tn = W_TILE
    cb0, shift = col0 // tn, col0 % tn
    assert shift % SUBLANES == 0
    w_specs = [pl.BlockSpec((None, tn, k), lambda j, i: (layer, cb0 + j, 0))]
    if shift:
        w_specs.append(pl.BlockSpec((None, tn, k), lambda j, i: (layer, cb0 + j + 1, 0)))
    return pl.pallas_call(
        functools.partial(_proj_kernel, shift=shift),
        out_shape=jax.ShapeDtypeStruct((m, n_cols), out_dtype),
        grid=(n_cols // tn, m // tm),
        in_specs=[pl.BlockSpec((tm, k), lambda j, i: (i, 0))] + w_specs,
        out_specs=pl.BlockSpec((tm, tn), lambda j, i: (i, j)),
        scratch_shapes=[pltpu.VMEM((k, tn), BF16)],
        compiler_params=_cparams("parallel", "arbitrary"),
        name=name,
    )(x, *([w_t] * len(w_specs)))


def _cast_kernel(w_ref, o_ref):
    o_ref[...] = w_ref[...].astype(BF16)


def _cast_bf16(w, layer):
    _, r, n = w.shape
    tr = _pick_tile(r, (512, 256, 128, 8))
    return pl.pallas_call(
        _cast_kernel,
        out_shape=jax.ShapeDtypeStruct((r, n), BF16),
        grid=(r // tr,),
        in_specs=[pl.BlockSpec((None, tr, n), lambda i: (layer, i, 0))],
        out_specs=pl.BlockSpec((tr, n), lambda i: (i, 0)),
        compiler_params=_cparams("parallel"),
        name="cast_bf16",
    )(w)


def _rope_tables(seq, ctx_len):
    n_rows = seq // GRID_W
    row = jnp.repeat(jnp.arange(n_rows, dtype=F32), GRID_W)
    col = jnp.tile(jnp.arange(GRID_W, dtype=F32), n_rows)
    half = DA_HEAD_DIM // 2
    inv_freq = ROPE_THETA ** (-jnp.arange(0, half, 2, dtype=F32) / half)
    ar = row[:, None] * inv_freq[None, :]
    ac = col[:, None] * inv_freq[None, :]
    ang = jnp.concatenate([ar, ar, ac, ac], axis=-1)
    ang = jnp.concatenate([ang, jnp.zeros((ctx_len, DA_HEAD_DIM), F32)], axis=0)
    cos = jnp.tile(jnp.cos(ang), (1, 2))
    sin = jnp.tile(jnp.sin(ang), (1, 2))
    first = (jnp.arange(LANES) % half) < (half // 2)
    sin_a = jnp.where(first[None, :], -sin, 0.0)
    sin_b = jnp.where(first[None, :], 0.0, sin)
    return cos, sin_a, sin_b


def _rope_kernel(p_ref, cos_ref, sa_ref, sb_ref, o_ref):
    cos = cos_ref[...]
    sa = sa_ref[...]
    sb = sb_ref[...]
    quarter = DA_HEAD_DIM // 4
    for h in range(2 * DA_HEADS):
        sl = slice(h * LANES, (h + 1) * LANES)
        x = p_ref[:, sl].astype(F32)
        r = x * cos + pltpu.roll(x, LANES - quarter, 1) * sa + pltpu.roll(x, quarter, 1) * sb
        if h < DA_HEADS:
            r = r * (DA_HEAD_DIM ** -0.5)
        o_ref[:, sl] = r.astype(BF16)


def _rope(p, tables, dims):
    t = p.shape[0]
    tm = TOK_TILE
    nl, seq, ctx_len = dims["nl"], dims["seq"], dims["ctx"]

    def tab(i):
        return (jnp.where(i < nl // tm, i % (seq // tm), seq // tm + (i - nl // tm) % (ctx_len // tm)), 0)

    w = 2 * DA_WIDTH
    tspec = pl.BlockSpec((tm, LANES), tab)
    return pl.pallas_call(
        _rope_kernel,
        out_shape=jax.ShapeDtypeStruct((t, w), BF16),
        grid=(t // tm,),
        in_specs=[pl.BlockSpec((tm, w), lambda i: (i, 0)), tspec, tspec, tspec],
        out_specs=pl.BlockSpec((tm, w), lambda i: (i, 0)),
        compiler_params=_cparams("parallel"),
        name="rope_qk",
    )(p, *tables)


ATTN_STAGE_CHUNKS = 4
ATTN_ROW_BLOCK = 64


def _attn_kernel(*refs, tq, tk, n_lat_k, lam_init):
    if n_lat_k:
        lam_ref, subln_ref, q_ref, kc_ref, vc_ref, kl_ref, vl_ref, o_ref = refs[:8]
    else:
        lam_ref, subln_ref, q_ref, kc_ref, vc_ref = refs[:5]
        o_ref = refs[-8]
    qs_ref, s_ref, p_ref, m_ref, a_ref, acc_ref, vx_ref = refs[-7:]
    rb = ATTN_ROW_BLOCK
    n_ctx = kc_ref.shape[0]

    n_lat = n_lat_k * tk

    @pl.when(pl.program_id(2) == 0)
    def _():
        if n_lat_k:
            vx_ref[0:n_lat, 0:LANES] = vl_ref[...]
        vx_ref[n_lat:, 0:LANES] = vc_ref[...]
        vx_ref[:, LANES:2 * LANES] = jnp.ones((vx_ref.shape[0], LANES), BF16)

    q = q_ref[...]
    lane = lax.broadcasted_iota(jnp.int32, q.shape, 1)
    zero = jnp.zeros_like(q)
    qs_ref[0:tq, :] = jnp.where(lane < DA_HEAD_DIM, q, zero)
    qs_ref[tq:2 * tq, :] = jnp.where(lane >= DA_HEAD_DIM, q, zero)
    m_ref[...] = jnp.full(m_ref.shape, -1e30, F32)
    acc_ref[...] = jnp.zeros(acc_ref.shape, F32)

    def key_rows(t, nk):
        return pl.ds(t * tk if isinstance(t, int) else pl.multiple_of(t * tk, tk), nk)

    def keys(t):
        if isinstance(t, int) and t == n_lat_k:
            return kc_ref[...], n_ctx
        return kl_ref[key_rows(t, tk), :], tk

    def width(t):
        return n_ctx if isinstance(t, int) and t == n_lat_k else tk

    def stage(t_next, t_cur, t_prev, buf):
        other = 1 - buf
        if t_next is not None:
            k_next, nk_next = keys(t_next)
        if t_prev is not None:
            nk_prev = width(t_prev)
            vx = vx_ref[key_rows(t_prev, nk_prev), :]
        rows_c = 2 * tq // ATTN_STAGE_CHUNKS
        for c in range(ATTN_STAGE_CHUNKS):
            rc = slice(c * rows_c, (c + 1) * rows_c)
            if t_next is not None:
                s_ref[other, rc, 0:nk_next] = lax.dot_general(qs_ref[rc, :], k_next, (((1,), (1,)), ((), ())),
                                                              preferred_element_type=F32)
            if t_cur is not None:
                nk = width(t_cur)
                reps = nk // LANES
                for r in range(c * rows_c // rb, (c + 1) * rows_c // rb):
                    rows = slice(r * rb, (r + 1) * rb)
                    m_prev = m_ref[rows, :]
                    m_new = jnp.maximum(m_prev, jnp.max(s_ref[buf, rows, 0:nk], axis=-1, keepdims=True))
                    a_ref[buf, rows, :] = jnp.exp(m_prev - m_new)
                    m_ref[rows, :] = m_new
                    p = jnp.exp(s_ref[buf, rows, 0:nk] - jnp.concatenate([m_new] * reps, axis=1))
                    p_ref[buf, rows, 0:nk] = p.astype(BF16)
            if t_prev is not None:
                alpha = a_ref[other, rc, :]
                pv = jnp.dot(p_ref[other, rc, 0:nk_prev], vx, preferred_element_type=F32)
                acc_ref[rc, :] = jnp.concatenate([alpha, alpha], axis=1) * acc_ref[rc, :] + pv

    last = n_lat_k
    loop_pairs = (n_lat_k - 2) // 2 if (n_lat_k >= 4 and n_lat_k % 2 == 0) else 0
    stage(0, None, None, 1)
    t = 0
    while t <= last:
        if t == 1 and loop_pairs:
            def body(i, carry):
                stage(2 + 2 * i, 1 + 2 * i, 2 * i, 1)
                stage(3 + 2 * i, 2 + 2 * i, 1 + 2 * i, 0)
                return carry
            lax.fori_loop(0, loop_pairs, body, 0)
            t += 2 * loop_pairs
            continue
        stage(t + 1 if t < last else None, t, t - 1 if t >= 1 else None, t % 2)
        t += 1
    stage(None, None, last, (last + 1) % 2)

    lf = lam_ref[...]
    lam = (jnp.exp(jnp.sum(lf[0:1] * lf[1:2], axis=-1, keepdims=True))
           - jnp.exp(jnp.sum(lf[2:3] * lf[3:4], axis=-1, keepdims=True)) + lam_init)
    o = acc_ref[:, 0:LANES] / acc_ref[:, LANES:2 * LANES]
    out = o[:tq] - lam * o[tq:]
    y = _rms(out) * subln_ref[...] * (1.0 - lam_init)
    o_ref[...] = y.astype(BF16)


def _attn_call(qk, p, da_lambda, da_subln, lam_init, dims, *, tq, q_row0, n_q, rows, with_lat):
    batch, seq, ctx_len, nl = dims["batch"], dims["seq"], dims["ctx"], dims["nl"]
    tk = 512
    q_map = lambda b, h, i: (q_row0 // tq + b * n_q + i, h)
    o_map = lambda b, h, i: (b * n_q + i, h)
    kv_lat = lambda off: pl.BlockSpec((seq, LANES), lambda b, h, i: (b, off + h))
    kv_ctx = lambda off: pl.BlockSpec((ctx_len, LANES), lambda b, h, i: (nl // ctx_len + b, off + h))
    in_specs = [pl.BlockSpec((4, DA_HEAD_DIM), lambda b, h, i: (0, 0)),
                pl.BlockSpec((1, DA_V_DIM), lambda b, h, i: (0, 0)),
                pl.BlockSpec((tq, LANES), q_map),
                kv_ctx(DA_HEADS), kv_ctx(V_OFF // LANES)]
    args = [da_lambda, da_subln.reshape(1, DA_V_DIM), qk, qk, p]
    if with_lat:
        in_specs += [kv_lat(DA_HEADS), kv_lat(V_OFF // LANES)]
        args += [qk, p]
    n_keys = ctx_len + (seq if with_lat else 0)
    scratch = [pltpu.VMEM((2 * tq, LANES), BF16), pltpu.VMEM((2, 2 * tq, tk), F32), pltpu.VMEM((2, 2 * tq, tk), BF16),
               pltpu.VMEM((2 * tq, LANES), F32), pltpu.VMEM((2, 2 * tq, LANES), F32),
               pltpu.VMEM((2 * tq, 2 * LANES), F32), pltpu.VMEM((n_keys, 2 * LANES), BF16)]
    kern = functools.partial(_attn_kernel, tq=tq, tk=tk, n_lat_k=seq // tk if with_lat else 0, lam_init=lam_init)
    return pl.pallas_call(
        kern,
        out_shape=jax.ShapeDtypeStruct((rows, DA_WIDTH), BF16),
        grid=(batch, DA_HEADS, n_q),
        in_specs=in_specs,
        out_specs=pl.BlockSpec((tq, LANES), o_map),
        scratch_shapes=scratch,
        compiler_params=_cparams("parallel", "parallel", "arbitrary"),
        name="diff_attn" if with_lat else "diff_attn_ctx",
    )(*args)


def _attention(qk, p, da_lambda, da_subln, lam_init, with_ctx, dims):
    batch, seq, ctx_len, nl = dims["batch"], dims["seq"], dims["ctx"], dims["nl"]
    tq = _pick_tile(seq, (1024, 512, TOK_TILE))
    y = _attn_call(qk, p, da_lambda, da_subln, lam_init, dims, tq=tq, q_row0=0, n_q=seq // tq, rows=nl,
                   with_lat=True)
    if with_ctx:
        tqc = TOK_TILE
        y_ctx = _attn_call(qk, p, da_lambda, da_subln, lam_init, dims, tq=tqc, q_row0=nl, n_q=ctx_len // tqc,
                           rows=batch * ctx_len, with_lat=False)
        y = jnp.concatenate([y, y_ctx], axis=0)
    return y


def _softplus(x):
    return jnp.maximum(x, 0.0) + jnp.log(1.0 + jnp.exp(-jnp.abs(x)))


def _ssd_dt_kernel(h_ref, w_ref, b_ref, alog_ref, dd_ref):
    raw = lax.dot_general(h_ref[...], w_ref[...].astype(BF16), (((1,), (1,)), ((), ())), preferred_element_type=F32)
    n = 2 * SSD_HEADS
    dt = _softplus(raw + b_ref[...])
    lane = lax.broadcasted_iota(jnp.int32, dt.shape, 1)
    dta = pltpu.roll(dt, n, 1) * (-jnp.exp(alog_ref[...]))
    dd = jnp.where(lane < n, dt, jnp.where(lane < 2 * n, dta, 0.0))
    dd_ref[...] = dd


def _ssd_dt(h, w_in, layer, bias2, alog2):
    t, d = h.shape
    tm = TOK_TILE
    assert REF_DT_OFF % LANES == 0
    return pl.pallas_call(
        _ssd_dt_kernel,
        out_shape=jax.ShapeDtypeStruct((t, LANES), F32),
        grid=(t // tm,),
        in_specs=[pl.BlockSpec((tm, d), lambda i: (i, 0)),
                  pl.BlockSpec((None, LANES, d), lambda i: (layer, REF_DT_OFF // LANES, 0)),
                  pl.BlockSpec((1, LANES), lambda i: (0, 0)),
                  pl.BlockSpec((1, LANES), lambda i: (0, 0))],
        out_specs=pl.BlockSpec((tm, LANES), lambda i: (i, 0)),
        compiler_params=_cparams("parallel"),
        name="ssd_dt",
    )(h, w_in, bias2, alog2)


def _conv_kernel(prev_ref, x_ref, next_ref, w_ref, b_ref, o_ref, *, tm, n_lat_tiles, lat_per_seq, ctx_per_seq):
    i = pl.program_id(0)
    is_lat = i < n_lat_tiles
    pos = jnp.where(is_lat, i % lat_per_seq, (i - n_lat_tiles) % ctx_per_seq)
    per = jnp.where(is_lat, lat_per_seq, ctx_per_seq)
    keep_prev = (pos > 0).astype(F32)
    keep_next = (pos < per - 1).astype(F32)
    ext = jnp.concatenate([prev_ref[...].astype(F32) * keep_prev, x_ref[...].astype(F32),
                           next_ref[...].astype(F32) * keep_next], axis=0)
    n = tm + 2 * SUBLANES
    w = w_ref[...]
    acc = jnp.zeros((tm, ext.shape[1]), F32) + b_ref[...]
    for k in range(SSD_CONV):
        shift = (SSD_CONV // 2 - k) % n
        rolled = ext if shift == 0 else pltpu.roll(ext, shift, 0)
        acc = acc + rolled[SUBLANES:SUBLANES + tm] * w[k:k + 1, :]
    o_ref[...] = _silu(acc).astype(BF16)


def _ssd_conv(p, conv_w, conv_b, dims):
    t = p.shape[0]
    tm = TOK_TILE
    cb = 512
    col0 = XBC_OFF // cb
    r8 = tm // SUBLANES
    last8 = t // SUBLANES - 1
    kern = functools.partial(_conv_kernel, tm=tm, n_lat_tiles=dims["nl"] // tm, lat_per_seq=dims["seq"] // tm,
                             ctx_per_seq=dims["ctx"] // tm)
    return pl.pallas_call(
        kern,
        out_shape=jax.ShapeDtypeStruct((t, SSD_XBC), BF16),
        grid=(t // tm, SSD_XBC // cb),
        in_specs=[pl.BlockSpec((SUBLANES, cb), lambda i, j: (jnp.maximum(i * r8 - 1, 0), col0 + j)),
                  pl.BlockSpec((tm, cb), lambda i, j: (i, col0 + j)),
                  pl.BlockSpec((SUBLANES, cb), lambda i, j: (jnp.minimum((i + 1) * r8, last8), col0 + j)),
                  pl.BlockSpec((SSD_CONV, cb), lambda i, j: (0, j)),
                  pl.BlockSpec((1, cb), lambda i, j: (0, j))],
        out_specs=pl.BlockSpec((tm, cb), lambda i, j: (i, j)),
        compiler_params=_cparams("parallel", "parallel"),
        name="ssd_conv",
    )(p, p, p, conv_w, conv_b.reshape(1, SSD_XBC))


def _split3(x):
    hi = x.astype(BF16)
    rest = x - hi.astype(F32)
    mid = rest.astype(BF16)
    return hi, mid, (rest - mid.astype(F32)).astype(BF16)


def _ssd_scan_kernel(*refs, direction, final, ncc, write_ctx):
    if final:
        xbc_ref, dd_ref, yf_ref, z_ref, dskip_ref, norm_ref, o_ref, h_ref = refs
    else:
        xbc_ref, dd_ref, o_ref, h_ref = refs
    j = pl.program_id(1)
    q = SSD_CHUNK
    gw = SSD_HPG * SSD_HEAD_DIM

    @pl.when(j == 0)
    def _():
        h_ref[...] = jnp.zeros_like(h_ref)

    xbc = xbc_ref[...]
    xs = xbc[:, :SSD_WIDTH].astype(F32)
    bm = xbc[:, SSD_WIDTH:SSD_WIDTH + SSD_GROUPS * SSD_STATE]
    cm = xbc[:, SSD_WIDTH + SSD_GROUPS * SSD_STATE:]
    dd = dd_ref[...]

    ii =lax.broadcasted_iota(jnp.int32, (q, q), 0)
    jj = lax.broadcasted_iota(jnp.int32, (q, q), 1)
    if direction == 0:
        mask = jj <= ii
        last = q - 1
    else:
        mask = jj >= ii
        last = 0
    tri = jnp.where(mask, 1.0, 0.0).astype(BF16)
    ac = sum(jnp.dot(tri, piece, preferred_element_type=F32) for piece in _split3(dd))
    ac_t = ac.T

    er = lax.broadcasted_iota(jnp.int32, (LANES, SSD_WIDTH), 0)
    ec = lax.broadcasted_iota(jnp.int32, (LANES, SSD_WIDTH), 1) // SSD_HEAD_DIM
    dt_col = direction * SSD_HEADS
    ac_col = 2 * SSD_HEADS + direction * SSD_HEADS
    e_dt = jnp.where(er == ec + dt_col, 1.0, 0.0).astype(BF16)
    e_ac = jnp.where(er == ec + ac_col, 1.0, 0.0).astype(BF16)
    dt_exp = sum(jnp.dot(piece, e_dt, preferred_element_type=F32) for piece in _split3(dd))
    ac_exp = sum(jnp.dot(piece, e_ac, preferred_element_type=F32) for piece in _split3(ac))
    ac_last = ac_exp[last:last + 1, :]
    eac = jnp.exp(ac_exp)
    dec_end = jnp.exp(ac_last - ac_exp)
    chunk_dec = jnp.exp(ac_last)
    xdt = xs * dt_exp
    xdt_b = xdt.astype(BF16)
    xde_b = (xdt * dec_end).astype(BF16)

    lane = lax.broadcasted_iota(jnp.int32, (q, LANES), 1)
    left = lane < SSD_HEAD_DIM
    zero_b = jnp.zeros((q, LANES), BF16)
    pieces = []
    for g in range(SSD_GROUPS):
        bg = bm[:, g * SSD_STATE:(g + 1) * SSD_STATE]
        cg = cm[:, g * SSD_STATE:(g + 1) * SSD_STATE]
        cb = lax.dot_general(cg, bg, (((1,), (1,)), ((), ())), preferred_element_type=F32)
        h_t = h_ref[g]
        y_off = jnp.dot(cg, h_t.astype(BF16), preferred_element_type=F32) * eac[:, g * gw:(g + 1) * gw]
        bg_t = bg.astype(F32).T.astype(BF16)
        s_t = jnp.dot(bg_t, xde_b[:, g * gw:(g + 1) * gw], preferred_element_type=F32)
        h_ref[g] = h_t * chunk_dec[:, g * gw:(g + 1) * gw] + s_t
        for kp in range(SSD_HPG // 2):
            k0 = g * SSD_HPG + 2 * kp
            ms = []
            for k in (k0, k0 + 1):
                c = ac_col + k
                seg = ac[:, c:c + 1] - ac_t[c:c + 1, :]
                ms.append((cb * jnp.exp(jnp.where(mask, seg, -1e30))).astype(BF16))
            xp = xdt_b[:, k0 * SSD_HEAD_DIM:k0 * SSD_HEAD_DIM + LANES]
            y_diag = jnp.dot(jnp.concatenate(ms, axis=1),
                             jnp.concatenate([jnp.where(left, xp, zero_b), jnp.where(left, zero_b, xp)], axis=0),
                             preferred_element_type=F32)
            pieces.append(y_diag + y_off[:, kp * LANES:(kp + 1) * LANES])
    y = jnp.concatenate(pieces, axis=1)

    def emit():
        if final:
            yt = y + yf_ref[...] + dskip_ref[...] * xs
            yt = yt * _silu(z_ref[...].astype(F32))
            o_ref[...] = (_rms(yt) * norm_ref[...]).astype(BF16)
        else:
            o_ref[...] = y

    if write_ctx:
        emit()
    else:
        pl.when(j >= ncc)(emit)


def _ssd_scan(xbc, dd, direction, dims, with_ctx, final_args=None):
    batch, seq, ctx_len, nl = dims["batch"], dims["seq"], dims["ctx"], dims["nl"]
    q = SSD_CHUNK
    ncl, ncc = seq // q, ctx_len // q
    final = final_args is not None
    rows = nl + (batch * ctx_len if with_ctx else 0)

    def chunk(b, j):
        jc = j if direction == 0 else ncc - 1 - j
        jl = (j - ncc) if direction == 0 else ncl - 1 - (j - ncc)
        return jnp.where(j < ncc, nl // q + b * ncc + jc, b * ncl + jl)

    def out_chunk(b, j):
        if with_ctx:
            return chunk(b, j)
        return chunk(b, jnp.maximum(j, ncc))

    in_specs = [pl.BlockSpec((q, SSD_XBC), lambda b, j: (chunk(b, j), 0)),
                pl.BlockSpec((q, LANES), lambda b, j: (chunk(b, j), 0))]
    args = [xbc, dd]
    if final:
        yf, p, dskip, norm = final_args
        in_specs += [pl.BlockSpec((q, SSD_WIDTH), lambda b, j: (out_chunk(b, j), 0)),
                     pl.BlockSpec((q, SSD_WIDTH), lambda b, j: (chunk(b, j), Z_OFF // SSD_WIDTH)),
                     pl.BlockSpec((1, SSD_WIDTH), lambda b, j: (0, 0)),
                     pl.BlockSpec((1, SSD_WIDTH), lambda b, j: (0, 0))]
        args += [yf, p, dskip, norm]
    return pl.pallas_call(
        functools.partial(_ssd_scan_kernel, direction=direction, final=final, ncc=ncc, write_ctx=with_ctx),
        out_shape=jax.ShapeDtypeStruct((rows, SSD_WIDTH), BF16 if final else F32),
        grid=(batch, ncc + ncl),
        in_specs=in_specs,
        out_specs=pl.BlockSpec((q, SSD_WIDTH), lambda b, j: (out_chunk(b, j), 0)),
        scratch_shapes=[pltpu.VMEM((SSD_GROUPS, SSD_STATE, SSD_HPG * SSD_HEAD_DIM), F32)],
        compiler_params=_cparams("parallel", "arbitrary"),
        name="ssd_scan_bwd" if direction else "ssd_scan_fwd",
    )(*args)


S5_BLOCK_GROUPS = LANES // S5_GROUP
S5_BLOCKS = S5_GROUPS // S5_BLOCK_GROUPS
S5_BS = S5_BLOCK_GROUPS * S5_STATE
S5_LAG_ROWS = 24


def _expm1(x):
    poly = 1.0 + x / 10.0
    for n in range(9, 1, -1):
        poly = 1.0 + (x / n) * poly
    return jnp.where(jnp.abs(x) < 0.35, x * poly, jnp.exp(x) - 1.0)


def _s5_params(lam_re, lam_im, log_step, b_re, b_im, c_re, c_im, s5_d):
    nb, gl = S5_BLOCKS, S5_BLOCK_GROUPS
    rows = [lam_re[0], lam_im[0], lam_re[1], lam_im[1],
            jnp.repeat(log_step[0], S5_STATE), jnp.repeat(log_step[1], S5_STATE)]
    rows = [r.reshape(nb, S5_BS) for r in rows] + [jnp.zeros((nb, S5_BS), F32)] * 2
    lam_rows = jnp.stack(rows, axis=1)
    eye = jnp.eye(gl, dtype=F32)

    def bd_in(b):
        return jnp.einsum("jgpe,gh->jgehp", b.reshape(nb, gl, S5_STATE, S5_GROUP), eye).reshape(nb, LANES, S5_BS)

    def bd_out(c):
        return jnp.einsum("jgfp,gh->jgfhp", c.reshape(nb, gl, S5_GROUP, S5_STATE), eye).reshape(nb, LANES, S5_BS)

    b_bd = jnp.stack([bd_in(b_re), bd_in(b_im)], axis=1)
    ct_bd = jnp.stack([bd_out(c_re), bd_out(c_im)], axis=1)
    return lam_rows, b_bd, ct_bd, s5_d.reshape(nb, 1, LANES)


def _gelu_tanh(x):
    return 0.5 * x * (1.0 + jnp.tanh(math.sqrt(2.0 / math.pi) * (x + 0.044715 * x * x * x)))


def _s5_kernel(u_ref, lam_ref, b_ref, ct_ref, d_ref, o_ref,
               x_ref, w_ref, m_ref, s_ref, y_ref, taps_ref, pw_ref, bb_ref, *, batch, ncl, ncc):
    tc = S5_CHUNK
    nch = x_ref.shape[0]
    sw = S5_BS
    nt = (((1,), (1,)), ((), ()))

    for s in range(tc):
        x_ref[:, s * LANES:(s + 1) * LANES] = u_ref[pl.ds(s, nch, stride=tc), :].astype(BF16)

    rows = lam_ref[0]
    kk = lax.broadcasted_iota(jnp.int32, (S5_LAG_ROWS, sw), 0).astype(F32)
    b_re, b_im = b_ref[0, 0], b_ref[0, 1]
    for d in range(2):
        l_re, l_im = rows[2 * d:2 * d + 1], rows[2 * d + 1:2 * d + 2]
        delta = jnp.exp(rows[4 + d:5 + d])
        lr, li = l_re * delta, l_im * delta
        mag = jnp.exp(kk * lr)
        pw_ref[2 * d] = mag * jnp.cos(kk * li)
        pw_ref[2 * d + 1] = mag * jnp.sin(kk * li)
        xr = _expm1(lr) * jnp.cos(li) - 2.0 * jnp.sin(0.5 * li) ** 2
        xi = jnp.exp(lr) * jnp.sin(li)
        den = l_re * l_re + l_im * l_im
        co_re = (xr * l_re + xi * l_im) / den
        co_im = (xi * l_re - xr * l_im) / den
        bb_ref[2 * d] = co_re * b_re - co_im * b_im
        bb_ref[2 * d + 1] = co_re * b_im + co_im * b_re

    def power(d, lag):
        return pw_ref[2 * d, lag:lag + 1, :], pw_ref[2 * d + 1, lag:lag + 1, :]

    ct_re, ct_im = ct_ref[0, 0], ct_ref[0, 1]
    ct_re_b, ct_im_b = ct_re.astype(BF16), ct_im.astype(BF16)
    for d in range(2):
        tp = []
        for part in range(2):
            q = 2 * d + part
            for s in range(tc):
                pr, pi = power(d, tc - 1 - s if d == 0 else s)
                if part == 0:
                    tile = pr * bb_ref[2 * d] - pi * bb_ref[2 * d + 1]
                else:
                    tile = pr * bb_ref[2 * d + 1] + pi * bb_ref[2 * d]
                w_ref[q, s * LANES:(s + 1) * LANES, :] = tile.astype(BF16)
            w = w_ref[q]
            s_ref[:, q * sw:(q + 1) * sw] = jnp.dot(x_ref[...], w, preferred_element_type=F32)
            tp.append(lax.dot_general(w, ct_re_b if part == 0 else ct_im_b, nt, preferred_element_type=F32))
        taps_ref[d] = tp[0] - tp[1]

    ri = lax.broadcasted_iota(jnp.int32, (LANES, LANES), 0)
    ci = lax.broadcasted_iota(jnp.int32, (LANES, LANES), 1)
    skip = jnp.where(ri == ci, d_ref[0], 0.0)
    for s in range(tc):
        for t in range(tc):
            if t >= s:
                lag_tile = tc - 1 - (t - s)
                tile = taps_ref[0, lag_tile * LANES:(lag_tile + 1) * LANES, :]
                if t == s:
                    tile = tile + taps_ref[1, 0:LANES, :] + skip
            else:
                tile = taps_ref[1, (s - t) * LANES:(s - t + 1) * LANES, :]
            m_ref[s * LANES:(s + 1) * LANES, t * LANES:(t + 1) * LANES] = tile.astype(BF16)

    af_re, af_im = power(0, tc)
    ab_re, ab_im = power(1, tc)

    def advance(row, h, dir_off, ar, ai):
        h_re, h_im = h
        s_re = s_ref[pl.ds(row, 1), dir_off:dir_off + sw]
        s_im = s_ref[pl.ds(row, 1), dir_off + sw:dir_off + 2 * sw]
        s_ref[pl.ds(row, 1), dir_off:dir_off + sw] = h_re
        s_ref[pl.ds(row, 1), dir_off + sw:dir_off + 2 * sw] = h_im
        return ar * h_re - ai * h_im + s_re, ar * h_im + ai * h_re + s_im

    def sweep(first_chunk, n):
        def body(i, c):
            return tuple((advance(first_chunk(b) + i, c[b][0], 0, af_re, af_im),
                          advance(first_chunk(b) + n - 1 - i, c[b][1], 2 * sw, ab_re, ab_im)) for b in range(batch))
        return body

    zero = jnp.zeros((1, sw), F32)
    c = lax.fori_loop(0, ncc, sweep(lambda b: batch * ncl + b * ncc, ncc), (((zero, zero), (zero, zero)),) * batch)
    lax.fori_loop(0, ncl, sweep(lambda b: b * ncl, ncl), c)

    y = jnp.dot(x_ref[...], m_ref[...], preferred_element_type=F32)
    for q in range(4):
        d, part = divmod(q, 2)
        for t in range(tc):
            pr, pi = power(d, t + 1 if d == 0 else tc - t)
            tile = ct_re * pr - ct_im * pi if part == 0 else -(ct_re * pi + ct_im * pr)
            w_ref[q, t * LANES:(t + 1) * LANES, :] = tile.astype(BF16)
        h = s_ref[:, q * sw:(q + 1) * sw].astype(BF16)
        y = y + lax.dot_general(h, w_ref[q], nt, preferred_element_type=F32)
    y_ref[...] = y
    for t in range(tc):
        o_ref[pl.ds(t, nch, stride=tc), :] = _gelu_tanh(y_ref[:, t * LANES:(t + 1) * LANES])


def _s5(u, params, dims):
    lam_rows, b_bd, ct_bd, d_skip = params
    t = u.shape[0]
    nch = t // S5_CHUNK
    ncl, ncc = dims["seq"] // S5_CHUNK, dims["ctx"] // S5_CHUNK
    width = S5_CHUNK * LANES
    pspec = pl.BlockSpec((1, 2, LANES, S5_BS), lambda i: (i, 0, 0, 0))
    return pl.pallas_call(
        functools.partial(_s5_kernel, batch=dims["batch"], ncl=ncl, ncc=ncc),
        out_shape=jax.ShapeDtypeStruct((t, S5_WIDTH), F32),
        grid=(S5_BLOCKS,),
        in_specs=[pl.BlockSpec((t, LANES), lambda i: (0, i)),
                  pl.BlockSpec((1, SUBLANES, S5_BS), lambda i: (i, 0, 0)), pspec, pspec,
                  pl.BlockSpec((1, 1, LANES), lambda i: (i, 0, 0))],
        out_specs=pl.BlockSpec((t, LANES), lambda i: (0, i)),
        scratch_shapes=[pltpu.VMEM((nch, width), BF16),
                        pltpu.VMEM((4, width, S5_BS), BF16),
                        pltpu.VMEM((width, width), BF16),
                        pltpu.VMEM((nch, 4 * S5_BS), F32),
                        pltpu.VMEM((nch, width), F32),
                        pltpu.VMEM((2, width, LANES), F32),
                        pltpu.VMEM((4, S5_LAG_ROWS, S5_BS), F32),
                        pltpu.VMEM((4, LANES, S5_BS), F32)],
        compiler_params=_cparams("parallel"),
        name="s5_scan",
    )(u, lam_rows, b_bd, ct_bd, d_skip)


def _glu_kernel(x_ref, wa_ref, wb_ref, ba_ref, bb_ref, o_ref, wab_ref, wbb_ref):
    @pl.when(pl.program_id(1) == 0)
    def _():
        wab_ref[...] = wa_ref[...].astype(BF16)
        wbb_ref[...] = wb_ref[...].astype(BF16)

    x = x_ref[...].astype(BF16)
    a = jnp.dot(x, wab_ref[...], preferred_element_type=F32) + ba_ref[...]
    b = jnp.dot(x, wbb_ref[...], preferred_element_type=F32) + bb_ref[...]
    o_ref[...] = (a * _sigmoid(b)).astype(BF16)


def _glu(x, w, bias, layer, rows, tm):
    k = x.shape[1]
    tn = W_TILE
    nb = S5_WIDTH // tn
    return pl.pallas_call(
        _glu_kernel,
        out_shape=jax.ShapeDtypeStruct((rows, S5_WIDTH), BF16),
        grid=(nb, rows // tm),
        in_specs=[pl.BlockSpec((tm, k), lambda j, i: (i, 0)),
                  pl.BlockSpec((None, k, tn), lambda j, i: (layer, 0, j)),
                  pl.BlockSpec((None, k, tn), lambda j, i: (layer, 0, nb + j)),
                  pl.BlockSpec((None, 1, tn), lambda j, i: (layer, 0, j)),
                  pl.BlockSpec((None, 1, tn), lambda j, i: (layer, 0, nb + j))],
        out_specs=pl.BlockSpec((tm, tn), lambda j, i: (i, j)),
        scratch_shapes=[pltpu.VMEM((k, tn), BF16), pltpu.VMEM((k, tn), BF16)],
        compiler_params=_cparams("parallel", "arbitrary"),
        name="s5_glu",
    )(x, w, w, bias, bias)


def _merge_kernel(ya_ref, yb_ref, yc_ref, wa_ref, wb_ref, wc_ref, ga_ref, gb_ref, gc_ref, o_ref, wbf_ref):
    @pl.when(pl.program_id(1) == 0)
    def _():
        for n, w_ref in enumerate((wa_ref, wb_ref, wc_ref)):
            wbf_ref[n] = w_ref[0].astype(BF16)

    acc = None
    for n, (y_ref, g_ref) in enumerate(((ya_ref, ga_ref), (yb_ref, gb_ref), (yc_ref, gc_ref))):
        br = jnp.dot(y_ref[...], wbf_ref[n], preferred_element_type=F32)
        term = _sigmoid(g_ref[...].astype(F32)) * br
        acc = term if acc is None else acc + term
    o_ref[...] = acc.astype(BF16)


def _merge(ya, yb, yc, w_branch, layer, gates, rows, tm):
    k = ya.shape[1]
    d = w_branch.shape[3]
    tn = W_TILE
    gstep = d // tn
    yspec = pl.BlockSpec((tm, k), lambda j, i: (i, 0))
    wspec = lambda n: pl.BlockSpec((None, 1, k, tn), lambda j, i: (layer, n, 0, j))
    gspec = lambda n: pl.BlockSpec((tm, tn), lambda j, i: (i, n * gstep + j))
    return pl.pallas_call(
        _merge_kernel,
        out_shape=jax.ShapeDtypeStruct((rows, d), BF16),
        grid=(d // tn, rows // tm),
        in_specs=[yspec, yspec, yspec, wspec(0), wspec(1), wspec(2), gspec(0), gspec(1), gspec(2)],
        out_specs=pl.BlockSpec((tm, tn), lambda j, i: (i, j)),
        scratch_shapes=[pltpu.VMEM((N_BRANCH, k, tn), BF16)],
        compiler_params=_cparams("parallel", "arbitrary"),
        name="branch_merge",
    )(ya, yb, yc, w_branch, w_branch, w_branch, gates, gates, gates)


def _out_proj_kernel(*refs, n_lat):
    g_ref, w_ref, *x_refs, mod_ref, npost_ref, npre_ref, xo_ref, ho_ref = refs
    o = jnp.dot(g_ref[...], w_ref[...], preferred_element_type=F32)
    gate = mod_ref[0, 2:3, :]
    xn = _stream_tile(x_refs, n_lat) + gate * (_rms(o) * npost_ref[...])
    xo_ref[...] = xn
    shift = mod_ref[0, 3:4, :]
    scale = mod_ref[0, 4:5, :]
    ho_ref[...] = (_rms(xn) * npre_ref[...] * (1.0 + scale) + shift).astype(BF16)


def _out_proj(g, w_out, xs, mod, npost, npre, dims):
    r, d = g.shape
    tm = TOK_TILE
    seg = functools.partial(_seg_of_tile, tm=tm, n_lat=dims["nl"], seq=dims["seq"], batch=dims["batch"])
    row = pl.BlockSpec((tm, d), lambda i: (i, 0))
    vec = pl.BlockSpec((1, d), lambda i: (0, 0))
    return pl.pallas_call(
        functools.partial(_out_proj_kernel, n_lat=dims["nl"] // tm),
        out_shape=(jax.ShapeDtypeStruct((r, d), F32), jax.ShapeDtypeStruct((r, d), BF16)),
        grid=(r // tm,),
        in_specs=[row, pl.BlockSpec((d, d), lambda i: (0, 0))]
        + _stream_specs(xs, tm, d, dims["nl"])
        + [pl.BlockSpec((1, 6, d), lambda i: (seg(i), 0, 0)), vec, vec],
        out_specs=(row, row),
        compiler_params=_cparams("parallel"),
        name="out_proj",
    )(g, w_out, *xs, mod, npost.reshape(1, d), npre.reshape(1, d))


def _ffn_up_kernel(h_ref, wg_ref, wu_ref, o_ref, wgb_ref, wub_ref):
    @pl.when(pl.program_id(1) == 0)
    def _():
        wgb_ref[...] = wg_ref[...].astype(BF16)
        wub_ref[...] = wu_ref[...].astype(BF16)

    h = h_ref[...]
    a = jnp.dot(h, wgb_ref[...], preferred_element_type=F32)
    b = jnp.dot(h, wub_ref[...], preferred_element_type=F32)
    o_ref[...] = (_silu(a) * b).astype(BF16)


def _ffn_up(h, wg, wu, layer, tm):
    r, d = h.shape
    f = wg.shape[2]
    tn = W_TILE
    wspec = pl.BlockSpec((None, d, tn), lambda j, i: (layer, 0, j))
    return pl.pallas_call(
        _ffn_up_kernel,
        out_shape=jax.ShapeDtypeStruct((r, f), BF16),
        grid=(f // tn, r // tm),
        in_specs=[pl.BlockSpec((tm, d), lambda j, i: (i, 0)), wspec, wspec],
        out_specs=pl.BlockSpec((tm, tn), lambda j, i: (i, j)),
        scratch_shapes=[pltpu.VMEM((d, tn), BF16), pltpu.VMEM((d, tn), BF16)],
        compiler_params=_cparams("parallel", "arbitrary"),
        name="ffn_up",
    )(h, wg, wu)


def _ffn_down_kernel(a_ref, w_ref, x_ref, mod_ref, npost_ref, o_ref, acc_ref):
    k = pl.program_id(1)

    @pl.when(k == 0)
    def _():
        acc_ref[...] = jnp.zeros_like(acc_ref)

    acc_ref[...] += jnp.dot(a_ref[...], w_ref[...], preferred_element_type=F32)

    @pl.when(k == pl.num_programs(1) - 1)
    def _():
        gate = mod_ref[0, 5:6, :]
        o_ref[...] = x_ref[...] + gate * (_rms(acc_ref[...]) * npost_ref[...])


def _ffn_down(act, wd, x, mod, npost, dims):
    r, f = act.shape
    d = wd.shape[1]
    tm = TOK_TILE * 2
    nk = 4
    tk = f // nk
    seg = functools.partial(_seg_of_tile, tm=tm, n_lat=dims["nl"], seq=dims["seq"], batch=dims["batch"])
    row = pl.BlockSpec((tm, d), lambda i, k: (i, 0))
    return pl.pallas_call(
        _ffn_down_kernel,
        out_shape=jax.ShapeDtypeStruct((r, d), F32),
        grid=(r // tm, nk),
        in_specs=[pl.BlockSpec((tm, tk), lambda i, k: (i, k)),
                  pl.BlockSpec((tk, d), lambda i, k: (k, 0)),
                  row,
                  pl.BlockSpec((1, 6, d), lambda i, k: (seg(i), 0, 0)),
                  pl.BlockSpec((1, d), lambda i, k: (0, 0))],
        out_specs=row,
        scratch_shapes=[pltpu.VMEM((tm, d), F32)],
        compiler_params=_cparams("parallel", "arbitrary"),
        name="ffn_down",
    )(act, wd, x, mod, npost.reshape(1, d))


def _pick_tile(rows, cands):
    for c in cands:
        if rows % c == 0:
            return c
    raise ValueError(f"no tile in {cands} divides {rows}")


def kernel(x, c, ctx, c_ctx, ada_w, ada_b, norm_mix_pre, norm_mix_post, norm_ffn_pre, norm_ffn_post, w_in, da_lambda, da_subln, ssd_conv_w, ssd_conv_b, ssd_dt_bias, ssd_a_log, ssd_d, ssd_norm, s5_lam_re, s5_lam_im, s5_log_step, s5_b_re, s5_b_im, s5_c_re, s5_c_im, s5_d, s5_glu_w, s5_glu_b, w_branch, w_out, ffn_w_gate, ffn_w_up, ffn_w_down):
    batch, seq, d = x.shape
    ctx_len = ctx.shape[1]
    depth = ada_w.shape[0]
    nl, nc = batch * seq, batch * ctx_len
    dims = dict(batch=batch, seq=seq, ctx=ctx_len, nl=nl, nc=nc)
    assert batch < MOD_ROWS and seq % TOK_TILE == 0 and ctx_len % TOK_TILE == 0 and seq % ctx_len == 0
    assert nc % (2 * TOK_TILE) == 0 and seq % GRID_W == 0

    xs = (x.reshape(nl, d), ctx.reshape(nc, d))
    cc =jnp.concatenate([c, c_ctx[None], jnp.zeros((MOD_ROWS - batch - 1, d), F32)], axis=0)
    mod_all = _ada(cc, ada_w, ada_b).reshape(depth, MOD_ROWS, 6, d)
    rope_tabs = _rope_tables(seq, ctx_len)
    n_dt = 2 * SSD_HEADS
    pad = lambda v, before: jnp.concatenate(
        [jnp.zeros((1, before), F32), v.reshape(1, n_dt), jnp.zeros((1, LANES - n_dt - before), F32)], axis=1)
    tm_all = _pick_tile(nl + nc, (2176, 1088, 512, 256))
    w_in_t = jnp.swapaxes(w_in, 1, 2)

    for l in range(depth):
        last = l == depth - 1
        with_ctx = not last
        lam_init = 0.8 - 0.6 * math.exp(-0.3 * l)
        mod = mod_all[l]
        rows = nl + nc if with_ctx else nl
        tm = _pick_tile(rows, (1088, 1024, 512))

        h = _norm_mod(xs, norm_mix_pre[l], mod, 0, dims)
        p = _proj(h, w_in_t, l, 0, REF_DT_OFF, BF16, tm_all, "in_proj_main")
        u = _proj(h, w_in_t, l, REF_U_OFF, S5_WIDTH, F32, tm_all, "in_proj_u")
        gates = _proj(h, w_in_t, l, REF_U_OFF + S5_WIDTH, N_BRANCH * d, BF16, tm_all, "in_proj_gates")

        qk = _rope(p, rope_tabs, dims)
        y_attn = _attention(qk, p, da_lambda[l], da_subln[l], lam_init, with_ctx, dims)

        dd = _ssd_dt(h, w_in_t, l, pad(ssd_dt_bias[l], 0), pad(ssd_a_log[l], n_dt))
        xbc = _ssd_conv(p, ssd_conv_w[l], ssd_conv_b[l], dims)
        y_f = _ssd_scan(xbc, dd, 0, dims, with_ctx)
        dskip = jnp.repeat(ssd_d[l], SSD_HEAD_DIM).reshape(1, SSD_WIDTH)
        y_ssd = _ssd_scan(xbc, dd, 1, dims, with_ctx,
                          final_args=(y_f, p, dskip, ssd_norm[l].reshape(1, SSD_WIDTH)))

        s5p = _s5_params(s5_lam_re[l], s5_lam_im[l], s5_log_step[l], s5_b_re[l], s5_b_im[l],
                         s5_c_re[l], s5_c_im[l], s5_d[l])
        yg = _s5(u, s5p, dims)
        y_s5 = _glu(yg, s5_glu_w, s5_glu_b.reshape(depth, 1, 2 * S5_WIDTH), l, rows, tm)

        g = _merge(y_attn, y_ssd, y_s5, w_branch, l, gates, rows, tm)
        xt, h2 = _out_proj(g, _cast_bf16(w_out, l), xs, mod, norm_mix_post[l], norm_ffn_pre[l], dims)
        act = _ffn_up(h2, ffn_w_gate, ffn_w_up, l, tm)
        xt = _ffn_down(act, _cast_bf16(ffn_w_down, l), xt, mod, norm_ffn_post[l], dims)
        xs = (xt,)

    return xt[:nl].reshape(batch, seq, d)
```

```python
import functools
import math

import jax
import jax.numpy as jnp
from jax import lax
from jax.experimental import pallas as pl
from jax.experimental.pallas import tpu as pltpu

F32 = jnp.float32
BF16 = jnp.bfloat16
HIGHEST = lax.Precision.HIGHEST

GRID_W = 64
N_BRANCH = 3
DA_HEADS = 8
DA_HEAD_DIM = 64
DA_V_DIM = 2 * DA_HEAD_DIM
DA_WIDTH = DA_HEADS * DA_V_DIM
ROPE_THETA = 10000.0
SSD_HEADS = 16
SSD_HEAD_DIM = 64
SSD_GROUPS = 2
SSD_HPG = SSD_HEADS // SSD_GROUPS
SSD_STATE = 128
SSD_WIDTH = SSD_HEADS * SSD_HEAD_DIM
SSD_XBC = SSD_WIDTH + 2 * SSD_GROUPS * SSD_STATE
SSD_CONV = 5
SSD_CHUNK = 128
S5_GROUP = 16
S5_GROUPS = 64
S5_WIDTH = S5_GROUPS * S5_GROUP
S5_STATE = 64
S5_CHUNK = 16
RMS_EPS = 1e-6

V_OFF = 2 * DA_WIDTH
Z_OFF = 3 * DA_WIDTH
XBC_OFF = 4 * DA_WIDTH
REF_DT_OFF = XBC_OFF + SSD_XBC
REF_U_OFF = REF_DT_OFF + 2 * SSD_HEADS

LANES = 128
SUBLANES = 8
VMEM_LIMIT_BYTES = 52 * 1024 * 1024
MOD_ROWS = 8

TOK_TILE = 256


def _cparams(*sem):
    return pltpu.CompilerParams(dimension_semantics=sem, vmem_limit_bytes=VMEM_LIMIT_BYTES)


def _rms(x):
    return x * lax.rsqrt(jnp.mean(x * x, axis=-1, keepdims=True) + RMS_EPS)


def _sigmoid(x):
    return 1.0 / (1.0 + jnp.exp(-x))


def _silu(x):
    return x * _sigmoid(x)


def _seg_of_tile(i, tm, n_lat, seq, batch):
    return jnp.where(i < n_lat // tm, i // (seq // tm), batch)


def _ada_kernel(c_ref, w_ref, b_ref, o_ref):
    c = c_ref[...]
    o_ref[0] = jnp.dot(_silu(c), w_ref[0], precision=HIGHEST, preferred_element_type=F32) + b_ref[0]


def _ada(cc, ada_w, ada_b):
    depth, d, n = ada_w.shape
    tn = 2048
    return pl.pallas_call(
        _ada_kernel,
        out_shape=jax.ShapeDtypeStruct((depth, MOD_ROWS, n), F32),
        grid=(depth, n // tn),
        in_specs=[pl.BlockSpec((MOD_ROWS, d), lambda l, j: (0, 0)),
                  pl.BlockSpec((1, d, tn), lambda l, j: (l, 0, j)),
                  pl.BlockSpec((1, 1, tn), lambda l, j: (l, 0, j))],
        out_specs=pl.BlockSpec((1, MOD_ROWS, tn), lambda l, j: (l, 0, j)),
        compiler_params=_cparams("parallel", "parallel"),
        name="ada_mod",
    )(cc, ada_w, ada_b.reshape(depth, 1, n))


def _stream_specs(xs, tm, d, nl):
    n_lat = nl // tm
    if len(xs) == 1:
        return [pl.BlockSpec((tm, d), lambda i: (i, 0))]
    return [pl.BlockSpec((tm, d), lambda i: (jnp.minimum(i, n_lat - 1), 0)),
            pl.BlockSpec((tm, d), lambda i: (jnp.maximum(i - n_lat, 0), 0))]


def _stream_tile(x_refs, n_lat):
    if len(x_refs) == 1:
        return x_refs[0][...]
    return jnp.where(pl.program_id(0) < n_lat, x_refs[0][...], x_refs[1][...])


def _norm_mod_kernel(*refs, shift_idx, n_lat):
    *x_refs, g_ref, mod_ref, o_ref = refs
    y = _rms(_stream_tile(x_refs, n_lat)) * g_ref[...]
    shift = mod_ref[0, shift_idx:shift_idx + 1, :]
    scale = mod_ref[0, shift_idx + 1:shift_idx + 2, :]
    o_ref[...] = (y * (1.0 + scale) + shift).astype(BF16)


def _norm_mod(xs, g, mod, shift_idx, dims):
    d = xs[0].shape[1]
    t = dims["nl"] + dims["nc"]
    tm = TOK_TILE
    seg = functools.partial(_seg_of_tile, tm=tm, n_lat=dims["nl"], seq=dims["seq"], batch=dims["batch"])
    return pl.pallas_call(
        functools.partial(_norm_mod_kernel, shift_idx=shift_idx, n_lat=dims["nl"] // tm),
        out_shape=jax.ShapeDtypeStruct((t, d), BF16),
        grid=(t // tm,),
        in_specs=_stream_specs(xs, tm, d, dims["nl"]) + [
            pl.BlockSpec((1, d), lambda i: (0, 0)),
            pl.BlockSpec((1, 6, d), lambda i: (seg(i), 0, 0))],
        out_specs=pl.BlockSpec((tm, d), lambda i: (i, 0)),
        compiler_params=_cparams("parallel"),
        name="norm_mod",
    )(*xs, g.reshape(1, d), mod)


W_TILE = 512


def _shifted_rows(w0, w1, shift):
    return jnp.concatenate([w0[shift:], w1[:shift]], axis=0)


def _proj_kernel(*refs, shift):
    if shift:
        x_ref, w0_ref, w1_ref, o_ref, wb_ref = refs
    else:
        x_ref, w0_ref, o_ref, wb_ref = refs

    @pl.when(pl.program_id(1) == 0)
    def _():
        w = _shifted_rows(w0_ref[...], w1_ref[...], shift) if shift else w0_ref[...]
        wb_ref[...] = w.T.astype(BF16)

    o_ref[...] = jnp.dot(x_ref[...], wb_ref[...], preferred_element_type=F32).astype(o_ref.dtype)


def _proj(x, w_t, layer, col0, n_cols, out_dtype, tm, name):
    m, k = x.shape
    tn = W_TILE
    cb0, shift = col0 // tn, col0 % tn
    assert shift % SUBLANES == 0
    w_specs = [pl.BlockSpec((None, tn, k), lambda j, i: (layer, cb0 + j, 0))]
    if shift:
        w_specs.append(pl.BlockSpec((None, tn, k), lambda j, i: (layer, cb0 + j + 1, 0)))
    return pl.pallas_call(
        functools.partial(_proj_kernel, shift=shift),
        out_shape=jax.ShapeDtypeStruct((m, n_cols), out_dtype),
        grid=(n_cols // tn, m // tm),
        in_specs=[pl.BlockSpec((tm, k), lambda j, i: (i, 0))] + w_specs,
        out_specs=pl.BlockSpec((tm, tn), lambda j, i: (i, j)),
        scratch_shapes=[pltpu.VMEM((k, tn), BF16)],
        compiler_params=_cparams("parallel", "arbitrary"),
        name=name,
    )(x, *([w_t] * len(w_specs)))


def _cast_kernel(w_ref, o_ref):
    o_ref[...] = w_ref[...].astype(BF16)


def _cast_bf16(w, layer):
    _, r, n = w.shape
    tr = _pick_tile(r, (512, 256, 128, 8))
    return pl.pallas_call(
        _cast_kernel,
        out_shape=jax.ShapeDtypeStruct((r, n), BF16),
        grid=(r // tr,),
        in_specs=[pl.BlockSpec((None, tr, n), lambda i: (layer, i, 0))],
        out_specs=pl.BlockSpec((tr, n), lambda i: (i, 0)),
        compiler_params=_cparams("parallel"),
        name="cast_bf16",
    )(w)


def _rope_tables(seq, ctx_len):
    n_rows = seq // GRID_W
    row = jnp.repeat(jnp.arange(n_rows, dtype=F32), GRID_W)
    col = jnp.tile(jnp.arange(GRID_W, dtype=F32), n_rows)
    half = DA_HEAD_DIM // 2
    inv_freq = ROPE_THETA ** (-jnp.arange(0, half, 2, dtype=F32) / half)
    ar = row[:, None] * inv_freq[None, :]
    ac = col[:, None] * inv_freq[None, :]
    ang = jnp.concatenate([ar, ar, ac, ac], axis=-1)
    ang = jnp.concatenate([ang, jnp.zeros((ctx_len, DA_HEAD_DIM), F32)], axis=0)
    cos = jnp.tile(jnp.cos(ang), (1, 2))
    sin = jnp.tile(jnp.sin(ang), (1, 2))
    first = (jnp.arange(LANES) % half) < (half // 2)
    sin_a = jnp.where(first[None, :], -sin, 0.0)
    sin_b = jnp.where(first[None, :], 0.0, sin)
    return cos, sin_a, sin_b


ATTN_STAGE_CHUNKS = 4
ATTN_ROW_BLOCK = 64


def _rope_rows(x, cos, sin_a, sin_b):
    quarter = DA_HEAD_DIM // 4
    return x * cos + pltpu.roll(x, LANES - quarter, 1) * sin_a + pltpu.roll(x, quarter, 1) * sin_b


def _attn_kernel(*refs, tq, tk, n_lat_k, lam_init):
    if n_lat_k:
        lam_ref, subln_ref, q_ref, kc_ref, vc_ref, kl_ref, vl_ref, cos_ref, sa_ref, sb_ref, o_ref = refs[:11]
        kr_ref = refs[-8]
    else:
        lam_ref, subln_ref, q_ref, kc_ref, vc_ref, o_ref = refs[:6]
    qs_ref, s_ref, p_ref, m_ref, a_ref, acc_ref, vx_ref = refs[-7:]
    rb = ATTN_ROW_BLOCK
    n_ctx = kc_ref.shape[0]
    n_lat = n_lat_k * tk
    scale = DA_HEAD_DIM ** -0.5

    @pl.when(pl.program_id(2) == 0)
    def _():
        if n_lat_k:
            vx_ref[0:n_lat, 0:LANES] = vl_ref[...]
            for c in range(n_lat_k):
                rows = slice(c * tk, (c + 1) * tk)
                kr_ref[rows, :] = _rope_rows(kl_ref[rows, :].astype(F32), cos_ref[rows, :], sa_ref[rows, :],
                                             sb_ref[rows, :]).astype(BF16)
        vx_ref[n_lat:, 0:LANES] = vc_ref[...]
        vx_ref[:, LANES:2 * LANES] = jnp.ones((vx_ref.shape[0], LANES), BF16)

    qf = q_ref[...].astype(F32)
    if n_lat_k:
        qrows = pl.ds(pl.multiple_of(pl.program_id(2) * tq, tq), tq)
        qf = _rope_rows(qf, cos_ref[qrows, :], sa_ref[qrows, :], sb_ref[qrows, :])
    q = (qf * scale).astype(BF16)
    lane = lax.broadcasted_iota(jnp.int32, q.shape, 1)
    zero = jnp.zeros_like(q)
    qs_ref[0:tq, :] = jnp.where(lane < DA_HEAD_DIM, q, zero)
    qs_ref[tq:2 * tq, :] = jnp.where(lane >= DA_HEAD_DIM, q, zero)
    m_ref[...] = jnp.full(m_ref.shape, -1e30, F32)
    acc_ref[...] = jnp.zeros(acc_ref.shape, F32)

    def key_rows(t, nk):
        return pl.ds(t * tk if isinstance(t, int) else pl.multiple_of(t * tk, tk), nk)

    def keys(t):
        if isinstance(t, int) and t == n_lat_k:
            return kc_ref[...], n_ctx
        return kr_ref[key_rows(t, tk), :], tk

    def width(t):
        return n_ctx if isinstance(t, int) and t == n_lat_k else tk

    def stage(t_next, t_cur, t_prev, buf):
        other = 1 - buf
        if t_next is not None:
            k_next, nk_next = keys(t_next)
        if t_prev is not None:
            nk_prev = width(t_prev)
            vx = vx_ref[key_rows(t_prev, nk_prev), :]
        rows_c = 2 * tq // ATTN_STAGE_CHUNKS
        for c in range(ATTN_STAGE_CHUNKS):
            rc = slice(c * rows_c, (c + 1) * rows_c)
            if t_next is not None:
                s_ref[other, rc, 0:nk_next] = lax.dot_general(qs_ref[rc, :], k_next, (((1,), (1,)), ((), ())),
                                                              preferred_element_type=F32)
            if t_cur is not None:
                nk = width(t_cur)
                reps = nk // LANES
                for r in range(c * rows_c // rb, (c + 1) * rows_c // rb):
                    rows = slice(r * rb, (r + 1) * rb)
                    m_prev = m_ref[rows, :]
                    m_new = jnp.maximum(m_prev, jnp.max(s_ref[buf, rows, 0:nk], axis=-1, keepdims=True))
                    a_ref[buf, rows, :] = jnp.exp(m_prev - m_new)
                    m_ref[rows, :] = m_new
                    p = jnp.exp(s_ref[buf, rows, 0:nk] - jnp.concatenate([m_new] * reps, axis=1))
                    p_ref[buf, rows, 0:nk] = p.astype(BF16)
            if t_prev is not None:
                alpha = a_ref[other, rc, :]
                pv = jnp.dot(p_ref[other, rc, 0:nk_prev], vx, preferred_element_type=F32)
                acc_ref[rc, :] = jnp.concatenate([alpha, alpha], axis=1) * acc_ref[rc, :] + pv

    last = n_lat_k
    loop_pairs = (n_lat_k - 2) // 2 if (n_lat_k >= 4 and n_lat_k % 2 == 0) else 0
    stage(0, None, None, 1)
    t = 0
    while t <= last:
        if t == 1 and loop_pairs:
            def body(i, carry):
                stage(2 + 2 * i, 1 + 2 * i, 2 * i, 1)
                stage(3 + 2 * i, 2 + 2 * i, 1 + 2 * i, 0)
                return carry
            lax.fori_loop(0, loop_pairs, body, 0)
            t += 2 * loop_pairs
            continue
        stage(t + 1 if t < last else None, t, t - 1 if t >= 1 else None, t % 2)
        t += 1
    stage(None, None, last, (last + 1) % 2)

    lf = lam_ref[...]
    lam = (jnp.exp(jnp.sum(lf[0:1] * lf[1:2], axis=-1, keepdims=True))
           - jnp.exp(jnp.sum(lf[2:3] * lf[3:4], axis=-1, keepdims=True)) + lam_init)
    o = acc_ref[:, 0:LANES] / acc_ref[:, LANES:2 * LANES]
    out = o[:tq] - lam * o[tq:]
    y = _rms(out) * subln_ref[...] * (1.0 - lam_init)
    o_ref[...] = y.astype(BF16)


def _attn_call(p, tables, da_lambda, da_subln, lam_init, dims, *, tq, q_row0, n_q, rows, with_lat):
    batch, seq, ctx_len, nl = dims["batch"], dims["seq"], dims["ctx"], dims["nl"]
    tk = 512
    k_col, v_col = DA_WIDTH // LANES, V_OFF // LANES
    q_map = lambda b, h, i: (q_row0 // tq + b * n_q + i, h)
    o_map = lambda b, h, i: (b * n_q + i, h)
    kv_lat = lambda off: pl.BlockSpec((seq, LANES), lambda b, h, i: (b, off + h))
    kv_ctx = lambda off: pl.BlockSpec((ctx_len, LANES), lambda b, h, i: (nl // ctx_len + b, off + h))
    in_specs = [pl.BlockSpec((4, DA_HEAD_DIM), lambda b, h, i: (0, 0)),
                pl.BlockSpec((1, DA_V_DIM), lambda b, h, i: (0, 0)),
                pl.BlockSpec((tq, LANES), q_map),
                kv_ctx(k_col), kv_ctx(v_col)]
    args = [da_lambda, da_subln.reshape(1, DA_V_DIM), p, p, p]
    n_keys = ctx_len + (seq if with_lat else 0)
    scratch = [pltpu.VMEM((2 * tq, LANES), BF16), pltpu.VMEM((2, 2 * tq, tk), F32), pltpu.VMEM((2, 2 * tq, tk), BF16),
               pltpu.VMEM((2 * tq, LANES), F32), pltpu.VMEM((2, 2 * tq, LANES), F32),
               pltpu.VMEM((2 * tq, 2 * LANES), F32), pltpu.VMEM((n_keys, 2 * LANES), BF16)]
    if with_lat:
        tab = pl.BlockSpec((seq, LANES), lambda b, h, i: (0, 0))
        in_specs += [kv_lat(k_col), kv_lat(v_col), tab, tab, tab]
        args += [p, p, *tables]
        scratch = [pltpu.VMEM((seq, LANES), BF16)] + scratch
    kern = functools.partial(_attn_kernel, tq=tq, tk=tk, n_lat_k=seq // tk if with_lat else 0, lam_init=lam_init)
    return pl.pallas_call(
        kern,
        out_shape=jax.ShapeDtypeStruct((rows, DA_WIDTH), BF16),
        grid=(batch, DA_HEADS, n_q),
        in_specs=in_specs,
        out_specs=pl.BlockSpec((tq, LANES), o_map),
        scratch_shapes=scratch,
        compiler_params=_cparams("parallel", "parallel", "arbitrary"),
        name="diff_attn" if with_lat else "diff_attn_ctx",
    )(*args)


def _attention(p, tables, da_lambda, da_subln, lam_init, with_ctx, dims):
    batch, seq, ctx_len, nl = dims["batch"], dims["seq"], dims["ctx"], dims["nl"]
    tq = _pick_tile(seq, (1024, 512, TOK_TILE))
    y = _attn_call(p, tables, da_lambda, da_subln, lam_init, dims, tq=tq, q_row0=0, n_q=seq // tq, rows=nl,
                   with_lat=True)
    if with_ctx:
        tqc = TOK_TILE
        y_ctx = _attn_call(p, tables, da_lambda, da_subln, lam_init, dims, tq=tqc, q_row0=nl, n_q=ctx_len // tqc,
                           rows=batch * ctx_len, with_lat=False)
        y = jnp.concatenate([y, y_ctx], axis=0)
    return y


def _softplus(x):
    return jnp.maximum(x, 0.0) + jnp.log(1.0 + jnp.exp(-jnp.abs(x)))


def _ssd_dt_kernel(h_ref, w_ref, b_ref, alog_ref, dd_ref):
    raw = lax.dot_general(h_ref[...], w_ref[...].astype(BF16), (((1,), (1,)), ((), ())), preferred_element_type=F32)
    n = 2 * SSD_HEADS
    dt = _softplus(raw + b_ref[...])
    lane = lax.broadcasted_iota(jnp.int32, dt.shape, 1)
    dta = pltpu.roll(dt, n, 1) * (-jnp.exp(alog_ref[...]))
    dd = jnp.where(lane < n, dt, jnp.where(lane < 2 * n, dta, 0.0))
    dd_ref[...] = dd


def _ssd_dt(h, w_in, layer, bias2, alog2):
    t, d = h.shape
    tm = TOK_TILE
    assert REF_DT_OFF % LANES == 0
    return pl.pallas_call(
        _ssd_dt_kernel,
        out_shape=jax.ShapeDtypeStruct((t, LANES), F32),
        grid=(t // tm,),
        in_specs=[pl.BlockSpec((tm, d), lambda i: (i, 0)),
                  pl.BlockSpec((None, LANES, d), lambda i: (layer, REF_DT_OFF // LANES, 0)),
                  pl.BlockSpec((1, LANES), lambda i: (0, 0)),
                  pl.BlockSpec((1, LANES), lambda i: (0, 0))],
        out_specs=pl.BlockSpec((tm, LANES), lambda i: (i, 0)),
        compiler_params=_cparams("parallel"),
        name="ssd_dt",
    )(h, w_in, bias2, alog2)


def _conv_kernel(prev_ref, x_ref, next_ref, w_ref, b_ref, o_ref, *, tm, n_lat_tiles, lat_per_seq, ctx_per_seq):
    i = pl.program_id(0)
    is_lat = i < n_lat_tiles
    pos = jnp.where(is_lat, i % lat_per_seq, (i - n_lat_tiles) % ctx_per_seq)
    per = jnp.where(is_lat, lat_per_seq, ctx_per_seq)
    keep_prev = (pos > 0).astype(F32)
    keep_next = (pos < per - 1).astype(F32)
    ext = jnp.concatenate([prev_ref[...].astype(F32) * keep_prev, x_ref[...].astype(F32),
                           next_ref[...].astype(F32) * keep_next], axis=0)
    n = tm + 2 * SUBLANES
    w = w_ref[...]
    acc = jnp.zeros((tm, ext.shape[1]), F32) + b_ref[...]
    for k in range(SSD_CONV):
        shift = (SSD_CONV // 2 - k) % n
        rolled = ext if shift == 0 else pltpu.roll(ext, shift, 0)
        acc = acc + rolled[SUBLANES:SUBLANES + tm] * w[k:k + 1, :]
    o_ref[...] = _silu(acc).astype(BF16)


def _ssd_conv(p, conv_w, conv_b, dims):
    t = p.shape[0]
    tm = TOK_TILE
    cb = 512
    col0 = XBC_OFF // cb
    r8 = tm // SUBLANES
    last8 = t // SUBLANES - 1
    kern = functools.partial(_conv_kernel, tm=tm, n_lat_tiles=dims["nl"] // tm, lat_per_seq=dims["seq"] // tm,
                             ctx_per_seq=dims["ctx"] // tm)
    return pl.pallas_call(
        kern,
        out_shape=jax.ShapeDtypeStruct((t, SSD_XBC), BF16),
        grid=(t // tm, SSD_XBC // cb),
        in_specs=[pl.BlockSpec((SUBLANES, cb), lambda i, j: (jnp.maximum(i * r8 - 1, 0), col0 + j)),
                  pl.BlockSpec((tm, cb), lambda i, j: (i, col0 + j)),
                  pl.BlockSpec((SUBLANES, cb), lambda i, j: (jnp.minimum((i + 1) * r8, last8), col0 + j)),
                  pl.BlockSpec((SSD_CONV, cb), lambda i, j: (0, j)),
                  pl.BlockSpec((1, cb), lambda i, j: (0, j))],
        out_specs=pl.BlockSpec((tm, cb), lambda i, j: (i, j)),
        compiler_params=_cparams("parallel", "parallel"),
        name="ssd_conv",
    )(p, p, p, conv_w, conv_b.reshape(1, SSD_XBC))


def _split3(x):
    hi = x.astype(BF16)
    rest = x - hi.astype(F32)
    mid = rest.astype(BF16)
    return hi, mid, (rest - mid.astype(F32)).astype(BF16)


def _ssd_scan_kernel(*refs, direction, final, ncc, write_ctx):
    if final:
        xbc_ref, dd_ref, yf_ref, z_ref, dskip_ref, norm_ref, o_ref, h_ref = refs
    else:
        xbc_ref, dd_ref, o_ref, h_ref = refs
    j = pl.program_id(1)
    q = SSD_CHUNK
    gw = SSD_HPG * SSD_HEAD_DIM

    @pl.when(j == 0)
    def _():
        h_ref[...] = jnp.zeros_like(h_ref)

    xbc = xbc_ref[...]
    xs = xbc[:, :SSD_WIDTH].astype(F32)
    bm = xbc[:, SSD_WIDTH:SSD_WIDTH + SSD_GROUPS * SSD_STATE]
    cm = xbc[:, SSD_WIDTH + SSD_GROUPS * SSD_STATE:]
    dd = dd_ref[...]

    ii =lax.broadcasted_iota(jnp.int32, (q, q), 0)
    jj = lax.broadcasted_iota(jnp.int32, (q, q), 1)
    if direction == 0:
        mask = jj <= ii
        last = q - 1
    else:
        mask = jj >= ii
        last = 0
    tri = jnp.where(mask, 1.0, 0.0).astype(BF16)
    ac = sum(jnp.dot(tri, piece, preferred_element_type=F32) for piece in _split3(dd))
    ac_t = ac.T

    er = lax.broadcasted_iota(jnp.int32, (LANES, SSD_WIDTH), 0)
    ec = lax.broadcasted_iota(jnp.int32, (LANES, SSD_WIDTH), 1) // SSD_HEAD_DIM
    dt_col = direction * SSD_HEADS
    ac_col = 2 * SSD_HEADS + direction * SSD_HEADS
    e_dt = jnp.where(er == ec + dt_col, 1.0, 0.0).astype(BF16)
    e_ac = jnp.where(er == ec + ac_col, 1.0, 0.0).astype(BF16)
    dt_exp = sum(jnp.dot(piece, e_dt, preferred_element_type=F32) for piece in _split3(dd))
    ac_exp = sum(jnp.dot(piece, e_ac, preferred_element_type=F32) for piece in _split3(ac))
    ac_last = ac_exp[last:last + 1, :]
    eac = jnp.exp(ac_exp)
    dec_end = jnp.exp(ac_last - ac_exp)
    chunk_dec = jnp.exp(ac_last)
    xdt = xs * dt_exp
    xdt_b = xdt.astype(BF16)
    xde_b = (xdt * dec_end).astype(BF16)

    lane = lax.broadcasted_iota(jnp.int32, (q, LANES), 1)
    left = lane < SSD_HEAD_DIM
    zero_b = jnp.zeros((q, LANES), BF16)
    pieces = []
    for g in range(SSD_GROUPS):
        bg = bm[:, g * SSD_STATE:(g + 1) * SSD_STATE]
        cg = cm[:, g * SSD_STATE:(g + 1) * SSD_STATE]
        cb = lax.dot_general(cg, bg, (((1,), (1,)), ((), ())), preferred_element_type=F32)
        h_t = h_ref[g]
        y_off = jnp.dot(cg, h_t.astype(BF16), preferred_element_type=F32) * eac[:, g * gw:(g + 1) * gw]
        bg_t = bg.astype(F32).T.astype(BF16)
        s_t = jnp.dot(bg_t, xde_b[:, g * gw:(g + 1) * gw], preferred_element_type=F32)
        h_ref[g] = h_t * chunk_dec[:, g * gw:(g + 1) * gw] + s_t
        for kp in range(SSD_HPG // 2):
            k0 = g * SSD_HPG + 2 * kp
            ms = []
            for k in (k0, k0 + 1):
                c = ac_col + k
                seg = ac[:, c:c + 1] - ac_t[c:c + 1, :]
                ms.append((cb * jnp.exp(jnp.where(mask, seg, -1e30))).astype(BF16))
            xp = xdt_b[:, k0 * SSD_HEAD_DIM:k0 * SSD_HEAD_DIM + LANES]
            y_diag = jnp.dot(jnp.concatenate(ms, axis=1),
                             jnp.concatenate([jnp.where(left, xp, zero_b), jnp.where(left, zero_b, xp)], axis=0),
                             preferred_element_type=F32)
            pieces.append(y_diag + y_off[:, kp * LANES:(kp + 1) * LANES])
    y = jnp.concatenate(pieces, axis=1)

    def emit():
        if final:
            yt = y + yf_ref[...] + dskip_ref[...] * xs
            yt = yt * _silu(z_ref[...].astype(F32))
            o_ref[...] = (_rms(yt) * norm_ref[...]).astype(BF16)
        else:
            o_ref[...] = y

    if write_ctx:
        emit()
    else:
        pl.when(j >= ncc)(emit)


def _ssd_scan(xbc, dd, direction, dims, with_ctx, final_args=None):
    batch, seq, ctx_len, nl = dims["batch"], dims["seq"], dims["ctx"], dims["nl"]
    q = SSD_CHUNK
    ncl, ncc = seq // q, ctx_len // q
    final = final_args is not None
    rows = nl + (batch * ctx_len if with_ctx else 0)

    def chunk(b, j):
        jc = j if direction == 0 else ncc - 1 - j
        jl = (j - ncc) if direction == 0 else ncl - 1 - (j - ncc)
        return jnp.where(j < ncc, nl // q + b * ncc + jc, b * ncl + jl)

    def out_chunk(b, j):
        if with_ctx:
            return chunk(b, j)
        return chunk(b, jnp.maximum(j, ncc))

    in_specs = [pl.BlockSpec((q, SSD_XBC), lambda b, j: (chunk(b, j), 0)),
                pl.BlockSpec((q, LANES), lambda b, j: (chunk(b, j), 0))]
    args = [xbc, dd]
    if final:
        yf, p, dskip, norm = final_args
        in_specs += [pl.BlockSpec((q, SSD_WIDTH), lambda b, j: (out_chunk(b, j), 0)),
                     pl.BlockSpec((q, SSD_WIDTH), lambda b, j: (chunk(b, j), Z_OFF // SSD_WIDTH)),
                     pl.BlockSpec((1, SSD_WIDTH), lambda b, j: (0, 0)),
                     pl.BlockSpec((1, SSD_WIDTH), lambda b, j: (0, 0))]
        args += [yf, p, dskip, norm]
    return pl.pallas_call(
        functools.partial(_ssd_scan_kernel, direction=direction, final=final, ncc=ncc, write_ctx=with_ctx),
        out_shape=jax.ShapeDtypeStruct((rows, SSD_WIDTH), BF16 if final else F32),
        grid=(batch, ncc + ncl),
        in_specs=in_specs,
        out_specs=pl.BlockSpec((q, SSD_WIDTH), lambda b, j: (out_chunk(b, j), 0)),
        scratch_shapes=[pltpu.VMEM((SSD_GROUPS, SSD_STATE, SSD_HPG * SSD_HEAD_DIM), F32)],
        compiler_params=_cparams("parallel", "arbitrary"),
        name="ssd_scan_bwd" if direction else "ssd_scan_fwd",
    )(*args)


S5_BLOCK_GROUPS = LANES // S5_GROUP
S5_BLOCKS = S5_GROUPS // S5_BLOCK_GROUPS
S5_BS = S5_BLOCK_GROUPS * S5_STATE
S5_LAG_ROWS = 24


def _expm1(x):
    poly = 1.0 + x / 10.0
    for n in range(9, 1, -1):
        poly = 1.0 + (x / n) * poly
    return jnp.where(jnp.abs(x) < 0.35, x * poly, jnp.exp(x) - 1.0)


def _s5_params(lam_re, lam_im, log_step, b_re, b_im, c_re, c_im, s5_d):
    nb, gl = S5_BLOCKS, S5_BLOCK_GROUPS
    rows = [lam_re[0], lam_im[0], lam_re[1], lam_im[1],
            jnp.repeat(log_step[0], S5_STATE), jnp.repeat(log_step[1], S5_STATE)]
    rows = [r.reshape(nb, S5_BS) for r in rows] + [jnp.zeros((nb, S5_BS), F32)] * 2
    lam_rows = jnp.stack(rows, axis=1)
    eye = jnp.eye(gl, dtype=F32)

    def bd_in(b):
        return jnp.einsum("jgpe,gh->jgehp", b.reshape(nb, gl, S5_STATE, S5_GROUP), eye).reshape(nb, LANES, S5_BS)

    def bd_out(c):
        return jnp.einsum("jgfp,gh->jgfhp", c.reshape(nb, gl, S5_GROUP, S5_STATE), eye).reshape(nb, LANES, S5_BS)

    b_bd = jnp.stack([bd_in(b_re), bd_in(b_im)], axis=1)
    ct_bd = jnp.stack([bd_out(c_re), bd_out(c_im)], axis=1)
    return lam_rows, b_bd, ct_bd, s5_d.reshape(nb, 1, LANES)


def _gelu_tanh(x):
    return 0.5 * x * (1.0 + jnp.tanh(math.sqrt(2.0 / math.pi) * (x + 0.044715 * x * x * x)))


def _s5_kernel(u_ref, lam_ref, b_ref, ct_ref, d_ref, o_ref,
               x_ref, w_ref, m_ref, s_ref, y_ref, taps_ref, pw_ref, bb_ref, *, batch, ncl, ncc):
    tc = S5_CHUNK
    nch = x_ref.shape[0]
    sw = S5_BS
    nt = (((1,), (1,)), ((), ()))

    for s in range(tc):
        x_ref[:, s * LANES:(s + 1) * LANES] = u_ref[pl.ds(s, nch, stride=tc), :].astype(BF16)

    rows = lam_ref[0]
    kk = lax.broadcasted_iota(jnp.int32, (S5_LAG_ROWS, sw), 0).astype(F32)
    b_re, b_im = b_ref[0, 0], b_ref[0, 1]
    for d in range(2):
        l_re, l_im = rows[2 * d:2 * d + 1], rows[2 * d + 1:2 * d + 2]
        delta = jnp.exp(rows[4 + d:5 + d])
        lr, li = l_re * delta, l_im * delta
        mag = jnp.exp(kk * lr)
        pw_ref[2 * d] = mag * jnp.cos(kk * li)
        pw_ref[2 * d + 1] = mag * jnp.sin(kk * li)
        xr = _expm1(lr) * jnp.cos(li) - 2.0 * jnp.sin(0.5 * li) ** 2
        xi = jnp.exp(lr) * jnp.sin(li)
        den = l_re * l_re + l_im * l_im
        co_re = (xr * l_re + xi * l_im) / den
        co_im = (xi * l_re - xr * l_im) / den
        bb_ref[2 * d] = co_re * b_re - co_im * b_im
        bb_ref[2 * d + 1] = co_re * b_im + co_im * b_re

    def power(d, lag):
        return pw_ref[2 * d, lag:lag + 1, :], pw_ref[2 * d + 1, lag:lag + 1, :]

    ct_re, ct_im = ct_ref[0, 0], ct_ref[0, 1]
    ct_re_b, ct_im_b = ct_re.astype(BF16), ct_im.astype(BF16)
    for d in range(2):
        tp = []
        for part in range(2):
            q = 2 * d + part
            for s in range(tc):
                pr, pi = power(d, tc - 1 - s if d == 0 else s)
                if part == 0:
                    tile = pr * bb_ref[2 * d] - pi * bb_ref[2 * d + 1]
                else:
                    tile = pr * bb_ref[2 * d + 1] + pi * bb_ref[2 * d]
                w_ref[q, s * LANES:(s + 1) * LANES, :] = tile.astype(BF16)
            w = w_ref[q]
            s_ref[:, q * sw:(q + 1) * sw] = jnp.dot(x_ref[...], w, preferred_element_type=F32)
            tp.append(lax.dot_general(w, ct_re_b if part == 0 else ct_im_b, nt, preferred_element_type=F32))
        taps_ref[d] = tp[0] - tp[1]

    ri = lax.broadcasted_iota(jnp.int32, (LANES, LANES), 0)
    ci = lax.broadcasted_iota(jnp.int32, (LANES, LANES), 1)
    skip = jnp.where(ri == ci, d_ref[0], 0.0)
    for s in range(tc):
        for t in range(tc):
            if t >= s:
                lag_tile = tc - 1 - (t - s)
                tile = taps_ref[0, lag_tile * LANES:(lag_tile + 1) * LANES, :]
                if t == s:
                    tile = tile + taps_ref[1, 0:LANES, :] + skip
            else:
                tile = taps_ref[1, (s - t) * LANES:(s - t + 1) * LANES, :]
            m_ref[s * LANES:(s + 1) * LANES, t * LANES:(t + 1) * LANES] = tile.astype(BF16)

    af_re, af_im = power(0, tc)
    ab_re, ab_im = power(1, tc)

    def advance(row, h, dir_off, ar, ai):
        h_re, h_im = h
        s_re = s_ref[pl.ds(row, 1), dir_off:dir_off + sw]
        s_im = s_ref[pl.ds(row, 1), dir_off + sw:dir_off + 2 * sw]
        s_ref[pl.ds(row, 1), dir_off:dir_off + sw] = h_re
        s_ref[pl.ds(row, 1), dir_off + sw:dir_off + 2 * sw] = h_im
        return ar * h_re - ai * h_im + s_re, ar * h_im + ai * h_re + s_im

    def sweep(first_chunk, n):
        def body(i, c):
            return tuple((advance(first_chunk(b) + i, c[b][0], 0, af_re, af_im),
                          advance(first_chunk(b) + n - 1 - i, c[b][1], 2 * sw, ab_re, ab_im)) for b in range(batch))
        return body

    zero = jnp.zeros((1, sw), F32)
    c = lax.fori_loop(0, ncc, sweep(lambda b: batch * ncl + b * ncc, ncc), (((zero, zero), (zero, zero)),) * batch)
    lax.fori_loop(0, ncl, sweep(lambda b: b * ncl, ncl), c)

    y = jnp.dot(x_ref[...], m_ref[...], preferred_element_type=F32)
    for q in range(4):
        d, part = divmod(q, 2)
        for t in range(tc):
            pr, pi = power(d, t + 1 if d == 0 else tc - t)
            tile = ct_re * pr - ct_im * pi if part == 0 else -(ct_re * pi + ct_im * pr)
            w_ref[q, t * LANES:(t + 1) * LANES, :] = tile.astype(BF16)
        h = s_ref[:, q * sw:(q + 1) * sw].astype(BF16)
        y = y + lax.dot_general(h, w_ref[q], nt, preferred_element_type=F32)
    y_ref[...] = y
    for t in range(tc):
        o_ref[pl.ds(t, nch, stride=tc), :] = _gelu_tanh(y_ref[:, t * LANES:(t + 1) * LANES])


def _s5(u, params, dims):
    lam_rows, b_bd, ct_bd, d_skip = params
    t = u.shape[0]
    nch = t // S5_CHUNK
    ncl, ncc = dims["seq"] // S5_CHUNK, dims["ctx"] // S5_CHUNK
    width = S5_CHUNK * LANES
    pspec = pl.BlockSpec((1, 2, LANES, S5_BS), lambda i: (i, 0, 0, 0))
    return pl.pallas_call(
        functools.partial(_s5_kernel, batch=dims["batch"], ncl=ncl, ncc=ncc),
        out_shape=jax.ShapeDtypeStruct((t, S5_WIDTH), F32),
        grid=(S5_BLOCKS,),
        in_specs=[pl.BlockSpec((t, LANES), lambda i: (0, i)),
                  pl.BlockSpec((1, SUBLANES, S5_BS), lambda i: (i, 0, 0)), pspec, pspec,
                  pl.BlockSpec((1, 1, LANES), lambda i: (i, 0, 0))],
        out_specs=pl.BlockSpec((t, LANES), lambda i: (0, i)),
        scratch_shapes=[pltpu.VMEM((nch, width), BF16),
                        pltpu.VMEM((4, width, S5_BS), BF16),
                        pltpu.VMEM((width, width), BF16),
                        pltpu.VMEM((nch, 4 * S5_BS), F32),
                        pltpu.VMEM((nch, width), F32),
                        pltpu.VMEM((2, width, LANES), F32),
                        pltpu.VMEM((4, S5_LAG_ROWS, S5_BS), F32),
                        pltpu.VMEM((4, LANES, S5_BS), F32)],
        compiler_params=_cparams("parallel"),
        name="s5_scan",
    )(u, lam_rows, b_bd, ct_bd, d_skip)


def _glu_kernel(x_ref, wa_ref, wb_ref, ba_ref, bb_ref, o_ref, wab_ref, wbb_ref):
    @pl.when(pl.program_id(1) == 0)
    def _():
        wab_ref[...] = wa_ref[...].astype(BF16)
        wbb_ref[...] = wb_ref[...].astype(BF16)

    x = x_ref[...].astype(BF16)
    a = jnp.dot(x, wab_ref[...], preferred_element_type=F32) + ba_ref[...]
    b = jnp.dot(x, wbb_ref[...], preferred_element_type=F32) + bb_ref[...]
    o_ref[...] = (a * _sigmoid(b)).astype(BF16)


def _glu(x, w, bias, layer, rows, tm):
    k = x.shape[1]
    tn = W_TILE
    nb = S5_WIDTH // tn
    return pl.pallas_call(
        _glu_kernel,
        out_shape=jax.ShapeDtypeStruct((rows, S5_WIDTH), BF16),
        grid=(nb, rows // tm),
        in_specs=[pl.BlockSpec((tm, k), lambda j, i: (i, 0)),
                  pl.BlockSpec((None, k, tn), lambda j, i: (layer, 0, j)),
                  pl.BlockSpec((None, k, tn), lambda j, i: (layer, 0, nb + j)),
                  pl.BlockSpec((None, 1, tn), lambda j, i: (layer, 0, j)),
                  pl.BlockSpec((None, 1, tn), lambda j, i: (layer, 0, nb + j))],
        out_specs=pl.BlockSpec((tm, tn), lambda j, i: (i, j)),
        scratch_shapes=[pltpu.VMEM((k, tn), BF16), pltpu.VMEM((k, tn), BF16)],
        compiler_params=_cparams("parallel", "arbitrary"),
        name="s5_glu",
    )(x, w, w, bias, bias)


def _merge_kernel(ya_ref, yb_ref, yc_ref, wa_ref, wb_ref, wc_ref, ga_ref, gb_ref, gc_ref, o_ref, wbf_ref):
    @pl.when(pl.program_id(1) == 0)
    def _():
        for n, w_ref in enumerate((wa_ref, wb_ref, wc_ref)):
            wbf_ref[n] = w_ref[0].astype(BF16)

    acc = None
    for n, (y_ref, g_ref) in enumerate(((ya_ref, ga_ref), (yb_ref, gb_ref), (yc_ref, gc_ref))):
        br = jnp.dot(y_ref[...], wbf_ref[n], preferred_element_type=F32)
        term = _sigmoid(g_ref[...].astype(F32)) * br
        acc = term if acc is None else acc + term
    o_ref[...] = acc.astype(BF16)


def _merge(ya, yb, yc, w_branch, layer, gates, rows, tm):
    k = ya.shape[1]
    d = w_branch.shape[3]
    tn = W_TILE
    gstep = d // tn
    yspec = pl.BlockSpec((tm, k), lambda j, i: (i, 0))
    wspec = lambda n: pl.BlockSpec((None, 1, k, tn), lambda j, i: (layer, n, 0, j))
    gspec = lambda n: pl.BlockSpec((tm, tn), lambda j, i: (i, n * gstep + j))
    return pl.pallas_call(
        _merge_kernel,
        out_shape=jax.ShapeDtypeStruct((rows, d), BF16),
        grid=(d // tn, rows // tm),
        in_specs=[yspec, yspec, yspec, wspec(0), wspec(1), wspec(2), gspec(0), gspec(1), gspec(2)],
        out_specs=pl.BlockSpec((tm, tn), lambda j, i: (i, j)),
        scratch_shapes=[pltpu.VMEM((N_BRANCH, k, tn), BF16)],
        compiler_params=_cparams("parallel", "arbitrary"),
        name="branch_merge",
    )(ya, yb, yc, w_branch, w_branch, w_branch, gates, gates, gates)


def _out_proj_kernel(*refs, n_lat):
    g_ref, w_ref, *x_refs, mod_ref, npost_ref, npre_ref, xo_ref, ho_ref = refs
    o = jnp.dot(g_ref[...], w_ref[...], preferred_element_type=F32)
    gate = mod_ref[0, 2:3, :]
    xn = _stream_tile(x_refs, n_lat) + gate * (_rms(o) * npost_ref[...])
    xo_ref[...] = xn
    shift = mod_ref[0, 3:4, :]
    scale = mod_ref[0, 4:5, :]
    ho_ref[...] = (_rms(xn) * npre_ref[...] * (1.0 + scale) + shift).astype(BF16)


def _out_proj(g, w_out, xs, mod, npost, npre, dims):
    r, d = g.shape
    tm = TOK_TILE
    seg = functools.partial(_seg_of_tile, tm=tm, n_lat=dims["nl"], seq=dims["seq"], batch=dims["batch"])
    row = pl.BlockSpec((tm, d), lambda i: (i, 0))
    vec = pl.BlockSpec((1, d), lambda i: (0, 0))
    return pl.pallas_call(
        functools.partial(_out_proj_kernel, n_lat=dims["nl"] // tm),
        out_shape=(jax.ShapeDtypeStruct((r, d), F32), jax.ShapeDtypeStruct((r, d), BF16)),
        grid=(r // tm,),
        in_specs=[row, pl.BlockSpec((d, d), lambda i: (0, 0))]
        + _stream_specs(xs, tm, d, dims["nl"])
        + [pl.BlockSpec((1, 6, d), lambda i: (seg(i), 0, 0)), vec, vec],
        out_specs=(row, row),
        compiler_params=_cparams("parallel"),
        name="out_proj",
    )(g, w_out, *xs, mod, npost.reshape(1, d), npre.reshape(1, d))


def _ffn_up_kernel(h_ref, wg_ref, wu_ref, o_ref, wgb_ref, wub_ref):
    @pl.when(pl.program_id(1) == 0)
    def _():
        wgb_ref[...] = wg_ref[...].astype(BF16)
        wub_ref[...] = wu_ref[...].astype(BF16)

    h = h_ref[...]
    a = jnp.dot(h, wgb_ref[...], preferred_element_type=F32)
    b = jnp.dot(h, wub_ref[...], preferred_element_type=F32)
    o_ref[...] = (_silu(a) * b).astype(BF16)


def _ffn_up(h, wg, wu, layer, tm):
    r, d = h.shape
    f = wg.shape[2]
    tn = W_TILE
    wspec = pl.BlockSpec((None, d, tn), lambda j, i: (layer, 0, j))
    return pl.pallas_call(
        _ffn_up_kernel,
        out_shape=jax.ShapeDtypeStruct((r, f), BF16),
        grid=(f // tn, r // tm),
        in_specs=[pl.BlockSpec((tm, d), lambda j, i: (i, 0)), wspec, wspec],
        out_specs=pl.BlockSpec((tm, tn), lambda j, i: (i, j)),
        scratch_shapes=[pltpu.VMEM((d, tn), BF16), pltpu.VMEM((d, tn), BF16)],
        compiler_params=_cparams("parallel", "arbitrary"),
        name="ffn_up",
    )(h, wg, wu)


def _ffn_down_kernel(a_ref, w_ref, x_ref, mod_ref, npost_ref, o_ref, acc_ref):
    k = pl.program_id(1)

    @pl.when(k == 0)
    def _():
        acc_ref[...] = jnp.zeros_like(acc_ref)

    acc_ref[...] += jnp.dot(a_ref[...], w_ref[...], preferred_element_type=F32)

    @pl.when(k == pl.num_programs(1) - 1)
    def _():
        gate = mod_ref[0, 5:6, :]
        o_ref[...] = x_ref[...] + gate * (_rms(acc_ref[...]) * npost_ref[...])


def _ffn_down(act, wd, x, mod, npost, dims):
    r, f = act.shape
    d = wd.shape[1]
    tm = TOK_TILE * 2
    nk = 4
    tk = f // nk
    seg = functools.partial(_seg_of_tile, tm=tm, n_lat=dims["nl"], seq=dims["seq"], batch=dims["batch"])
    row = pl.BlockSpec((tm, d), lambda i, k: (i, 0))
    return pl.pallas_call(
        _ffn_down_kernel,
        out_shape=jax.ShapeDtypeStruct((r, d), F32),
        grid=(r // tm, nk),
        in_specs=[pl.BlockSpec((tm, tk), lambda i, k: (i, k)),
                  pl.BlockSpec((tk, d), lambda i, k: (k, 0)),
                  row,
                  pl.BlockSpec((1, 6, d), lambda i, k: (seg(i), 0, 0)),
                  pl.BlockSpec((1, d), lambda i, k: (0, 0))],
        out_specs=row,
        scratch_shapes=[pltpu.VMEM((tm, d), F32)],
        compiler_params=_cparams("parallel", "arbitrary"),
        name="ffn_down",
    )(act, wd, x, mod, npost.reshape(1, d))


def _pick_tile(rows, cands):
    for c in cands:
        if rows % c == 0:
            return c
    raise ValueError(f"no tile in {cands} divides {rows}")


def kernel(x, c, ctx, c_ctx, ada_w, ada_b, norm_mix_pre, norm_mix_post, norm_ffn_pre, norm_ffn_post, w_in, da_lambda, da_subln, ssd_conv_w, ssd_conv_b, ssd_dt_bias, ssd_a_log, ssd_d, ssd_norm, s5_lam_re, s5_lam_im, s5_log_step, s5_b_re, s5_b_im, s5_c_re, s5_c_im, s5_d, s5_glu_w, s5_glu_b, w_branch, w_out, ffn_w_gate, ffn_w_up, ffn_w_down):
    batch, seq, d = x.shape
    ctx_len = ctx.shape[1]
    depth = ada_w.shape[0]
    nl, nc = batch * seq, batch * ctx_len
    dims = dict(batch=batch, seq=seq, ctx=ctx_len, nl=nl, nc=nc)
    assert batch < MOD_ROWS and seq % TOK_TILE == 0 and ctx_len % TOK_TILE == 0 and seq % ctx_len == 0
    assert nc % (2 * TOK_TILE) == 0 and seq % GRID_W == 0

    xs = (x.reshape(nl, d), ctx.reshape(nc, d))
    cc =jnp.concatenate([c, c_ctx[None], jnp.zeros((MOD_ROWS - batch - 1, d), F32)], axis=0)
    mod_all = _ada(cc, ada_w, ada_b).reshape(depth, MOD_ROWS, 6, d)
    rope_tabs = _rope_tables(seq, ctx_len)
    n_dt = 2 * SSD_HEADS
    pad = lambda v, before: jnp.concatenate(
        [jnp.zeros((1, before), F32), v.reshape(1, n_dt), jnp.zeros((1, LANES - n_dt - before), F32)], axis=1)
    tm_all = _pick_tile(nl + nc, (2176, 1088, 512, 256))
    w_in_t = jnp.swapaxes(w_in, 1, 2)

    for l in range(depth):
        last = l == depth - 1
        with_ctx = not last
        lam_init = 0.8 - 0.6 * math.exp(-0.3 * l)
        mod = mod_all[l]
        rows = nl + nc if with_ctx else nl
        tm = _pick_tile(rows, (1088, 1024, 512))

        h = _norm_mod(xs, norm_mix_pre[l], mod, 0, dims)
        p = _proj(h, w_in_t, l, 0, REF_DT_OFF, BF16, tm_all, "in_proj_main")
        u = _proj(h, w_in_t, l, REF_U_OFF, S5_WIDTH, F32, tm_all, "in_proj_u")
        gates = _proj(h, w_in_t, l, REF_U_OFF + S5_WIDTH, N_BRANCH * d, BF16, tm_all, "in_proj_gates")

        y_attn = _attention(p, rope_tabs, da_lambda[l], da_subln[l], lam_init, with_ctx, dims)

        dd = _ssd_dt(h, w_in_t, l, pad(ssd_dt_bias[l], 0), pad(ssd_a_log[l], n_dt))
        xbc = _ssd_conv(p, ssd_conv_w[l], ssd_conv_b[l], dims)
        y_f = _ssd_scan(xbc, dd, 0, dims, with_ctx)
        dskip = jnp.repeat(ssd_d[l], SSD_HEAD_DIM).reshape(1, SSD_WIDTH)
        y_ssd = _ssd_scan(xbc, dd, 1, dims, with_ctx,
                          final_args=(y_f, p, dskip, ssd_norm[l].reshape(1, SSD_WIDTH)))

        s5p = _s5_params(s5_lam_re[l], s5_lam_im[l], s5_log_step[l], s5_b_re[l], s5_b_im[l],
                         s5_c_re[l], s5_c_im[l], s5_d[l])
        yg = _s5(u, s5p, dims)
        y_s5 = _glu(yg, s5_glu_w, s5_glu_b.reshape(depth, 1, 2 * S5_WIDTH), l, rows, tm)

        g = _merge(y_attn, y_ssd, y_s5, w_branch, l, gates, rows, tm)
        xt, h2 = _out_proj(g, _cast_bf16(w_out, l), xs, mod, norm_mix_post[l], norm_ffn_pre[l], dims)
        act = _ffn_up(h2, ffn_w_gate, ffn_w_up, l, tm)
        xt = _ffn_down(act, _cast_bf16(ffn_w_down, l), xt, mod, norm_ffn_post[l], dims)
        xs = (xt,)

    return xt[:nl].reshape(batch, seq, d)
```

```python
import functools
import math

import jax
import jax.numpy as jnp
from jax import lax
from jax.experimental import pallas as pl
from jax.experimental.pallas import tpu as pltpu

F32 = jnp.float32
BF16 = jnp.bfloat16
HIGHEST = lax.Precision.HIGHEST

GRID_W = 64
N_BRANCH = 3
DA_HEADS = 8
DA_HEAD_DIM = 64
DA_V_DIM = 2 * DA_HEAD_DIM
DA_WIDTH = DA_HEADS * DA_V_DIM
ROPE_THETA = 10000.0
SSD_HEADS = 16
SSD_HEAD_DIM = 64
SSD_GROUPS = 2
SSD_HPG = SSD_HEADS // SSD_GROUPS
SSD_STATE = 128
SSD_WIDTH = SSD_HEADS * SSD_HEAD_DIM
SSD_XBC = SSD_WIDTH + 2 * SSD_GROUPS * SSD_STATE
SSD_CONV = 5
SSD_CHUNK = 128
S5_GROUP = 16
S5_GROUPS = 64
S5_WIDTH = S5_GROUPS * S5_GROUP
S5_STATE = 64
S5_CHUNK = 16
RMS_EPS = 1e-6

V_OFF = 2 * DA_WIDTH
Z_OFF = 3 * DA_WIDTH
XBC_OFF = 4 * DA_WIDTH
REF_DT_OFF = XBC_OFF + SSD_XBC
REF_U_OFF = REF_DT_OFF + 2 * SSD_HEADS

LANES = 128
SUBLANES = 8
VMEM_LIMIT_BYTES = 52 * 1024 * 1024
MOD_ROWS = 8

TOK_TILE = 256


def _cparams(*sem):
    return pltpu.CompilerParams(dimension_semantics=sem, vmem_limit_bytes=VMEM_LIMIT_BYTES)


def _rms(x):
    return x * lax.rsqrt(jnp.mean(x * x, axis=-1, keepdims=True) + RMS_EPS)


def _sigmoid(x):
    return 1.0 / (1.0 + jnp.exp(-x))


def _silu(x):
    return x * _sigmoid(x)


def _seg_of_tile(i, tm, n_lat, seq, batch):
    return jnp.where(i < n_lat // tm, i // (seq // tm), batch)


def _ada_kernel(c_ref, w_ref, b_ref, o_ref):
    c = c_ref[...]
    o_ref[0] = jnp.dot(_silu(c), w_ref[0], precision=HIGHEST, preferred_element_type=F32) + b_ref[0]


def _ada(cc, ada_w, ada_b):
    depth, d, n = ada_w.shape
    tn = 2048
    return pl.pallas_call(
        _ada_kernel,
        out_shape=jax.ShapeDtypeStruct((depth, MOD_ROWS, n), F32),
        grid=(depth, n // tn),
        in_specs=[pl.BlockSpec((MOD_ROWS, d), lambda l, j: (0, 0)),
                  pl.BlockSpec((1, d, tn), lambda l, j: (l, 0, j)),
                  pl.BlockSpec((1, 1, tn), lambda l, j: (l, 0, j))],
        out_specs=pl.BlockSpec((1, MOD_ROWS, tn), lambda l, j: (l, 0, j)),
        compiler_params=_cparams("parallel", "parallel"),
        name="ada_mod",
    )(cc, ada_w, ada_b.reshape(depth, 1, n))


def _stream_specs(xs, tm, d, nl):
    n_lat = nl // tm
    if len(xs) == 1:
        return [pl.BlockSpec((tm, d), lambda i: (i, 0))]
    return [pl.BlockSpec((tm, d), lambda i: (jnp.minimum(i, n_lat - 1), 0)),
            pl.BlockSpec((tm, d), lambda i: (jnp.maximum(i - n_lat, 0), 0))]


def _stream_tile(x_refs, n_lat):
    if len(x_refs) == 1:
        return x_refs[0][...]
    return jnp.where(pl.program_id(0) < n_lat, x_refs[0][...], x_refs[1][...])


def _norm_mod_kernel(*refs, shift_idx, n_lat):
    *x_refs, g_ref, mod_ref, o_ref = refs
    y = _rms(_stream_tile(x_refs, n_lat)) * g_ref[...]
    shift = mod_ref[0, shift_idx:shift_idx + 1, :]
    scale = mod_ref[0, shift_idx + 1:shift_idx + 2, :]
    o_ref[...] = (y * (1.0 + scale) + shift).astype(BF16)


def _norm_mod(xs, g, mod, shift_idx, dims):
    d = xs[0].shape[1]
    t = dims["nl"] + dims["nc"]
    tm = TOK_TILE
    seg = functools.partial(_seg_of_tile, tm=tm, n_lat=dims["nl"], seq=dims["seq"], batch=dims["batch"])
    return pl.pallas_call(
        functools.partial(_norm_mod_kernel, shift_idx=shift_idx, n_lat=dims["nl"] // tm),
        out_shape=jax.ShapeDtypeStruct((t, d), BF16),
        grid=(t // tm,),
        in_specs=_stream_specs(xs, tm, d, dims["nl"]) + [
            pl.BlockSpec((1, d), lambda i: (0, 0)),
            pl.BlockSpec((1, 6, d), lambda i: (seg(i), 0, 0))],
        out_specs=pl.BlockSpec((tm, d), lambda i: (i, 0)),
        compiler_params=_cparams("parallel"),
        name="norm_mod",
    )(*xs, g.reshape(1, d), mod)


W_TILE = 512


def _shifted_rows(w0, w1, shift):
    return jnp.concatenate([w0[shift:], w1[:shift]], axis=0)


def _proj_kernel(*refs, shift):
    if shift:
        x_ref, w0_ref, w1_ref, o_ref, wb_ref = refs
    else:
        x_ref, w0_ref, o_ref, wb_ref = refs

    @pl.when(pl.program_id(1) == 0)
    def _():
        w = _shifted_rows(w0_ref[...], w1_ref[...], shift) if shift else w0_ref[...]
        wb_ref[...] = w.T.astype(BF16)

    o_ref[...] = jnp.dot(x_ref[...], wb_ref[...], preferred_element_type=F32).astype(o_ref.dtype)


def _proj(x, w_t, layer, col0, n_cols, out_dtype, tm, name):
    m, k = x.shape
    tn = W_TILE
    cb0, shift = col0 // tn, col0 % tn
    assert shift % SUBLANES == 0
    w_specs = [pl.BlockSpec((None, tn, k), lambda j, i: (layer, cb0 + j, 0))]
    if shift:
        w_specs.append(pl.BlockSpec((None, tn, k), lambda j, i: (layer, cb0 + j + 1, 0)))
    return pl.pallas_call(
        functools.partial(_proj_kernel, shift=shift),
        out_shape=jax.ShapeDtypeStruct((m, n_cols), out_dtype),
        grid=(n_cols // tn, m // tm),
        in_specs=[pl.BlockSpec((tm, k), lambda j, i: (i, 0))] + w_specs,
        out_specs=pl.BlockSpec((tm, tn), lambda j, i: (i, j)),
        scratch_shapes=[pltpu.VMEM((k, tn), BF16)],
        compiler_params=_cparams("parallel", "arbitrary"),
        name=name,
    )(x, *([w_t] * len(w_specs)))


def _cast_kernel(w_ref, o_ref):
    o_ref[...] = w_ref[...].astype(BF16)


def _cast_bf16(w, layer):
    _, r, n = w.shape
    tr = _pick_tile(r, (512, 256, 128, 8))
    return pl.pallas_call(
        _cast_kernel,
        out_shape=jax.ShapeDtypeStruct((r, n), BF16),
        grid=(r // tr,),
        in_specs=[pl.BlockSpec((None, tr, n), lambda i: (layer, i, 0))],
        out_specs=pl.BlockSpec((tr, n), lambda i: (i, 0)),
        compiler_params=_cparams("parallel"),
        name="cast_bf16",
    )(w)


def _rope_tables(seq, ctx_len):
    n_rows = seq // GRID_W
    row = jnp.repeat(jnp.arange(n_rows, dtype=F32), GRID_W)
    col = jnp.tile(jnp.arange(GRID_W, dtype=F32), n_rows)
    half = DA_HEAD_DIM // 2
    inv_freq = ROPE_THETA ** (-jnp.arange(0, half, 2, dtype=F32) / half)
    ar = row[:, None] * inv_freq[None, :]
    ac = col[:, None] * inv_freq[None, :]
    ang = jnp.concatenate([ar, ar, ac, ac], axis=-1)
    ang = jnp.concatenate([ang, jnp.zeros((ctx_len, DA_HEAD_DIM), F32)], axis=0)
    cos = jnp.tile(jnp.cos(ang), (1, 2))
    sin = jnp.tile(jnp.sin(ang), (1, 2))
    first = (jnp.arange(LANES) % half) < (half // 2)
    sin_a = jnp.where(first[None, :], -sin, 0.0)
    sin_b = jnp.where(first[None, :], 0.0, sin)
    return cos, sin_a, sin_b


ATTN_STAGE_CHUNKS = 4
ATTN_ROW_BLOCK = 64


def _rope_rows(x, cos, sin_a, sin_b):
    quarter = DA_HEAD_DIM // 4
    return x * cos + pltpu.roll(x, LANES - quarter, 1) * sin_a + pltpu.roll(x, quarter, 1) * sin_b


def _attn_kernel(*refs, tq, tk, n_lat_k, lam_init):
    if n_lat_k:
        lam_ref, subln_ref, q_ref, kc_ref, vc_ref, kl_ref, vl_ref, cos_ref, sa_ref, sb_ref, o_ref = refs[:11]
        kr_ref = refs[-8]
    else:
        lam_ref, subln_ref, q_ref, kc_ref, vc_ref, o_ref = refs[:6]
    qs_ref, s_ref, p_ref, m_ref, a_ref, acc_ref, vx_ref = refs[-7:]
    rb = ATTN_ROW_BLOCK
    n_ctx = kc_ref.shape[0]
    n_lat = n_lat_k * tk
    scale = DA_HEAD_DIM ** -0.5

    @pl.when(pl.program_id(2) == 0)
    def _():
        if n_lat_k:
            vx_ref[0:n_lat, 0:LANES] = vl_ref[...]
            for c in range(n_lat_k):
                rows = slice(c * tk, (c + 1) * tk)
                kr_ref[rows, :] = _rope_rows(kl_ref[rows, :].astype(F32), cos_ref[rows, :], sa_ref[rows, :],
                                             sb_ref[rows, :]).astype(BF16)
        vx_ref[n_lat:, 0:LANES] = vc_ref[...]
        vx_ref[:, LANES:2 * LANES] = jnp.ones((vx_ref.shape[0], LANES), BF16)

    qf = q_ref[...].astype(F32)
    if n_lat_k:
        qrows = pl.ds(pl.multiple_of(pl.program_id(2) * tq, tq), tq)
        qf = _rope_rows(qf, cos_ref[qrows, :], sa_ref[qrows, :], sb_ref[qrows, :])
    q = (qf * scale).astype(BF16)
    lane = lax.broadcasted_iota(jnp.int32, q.shape, 1)
    zero = jnp.zeros_like(q)
    qs_ref[0:tq, :] = jnp.where(lane < DA_HEAD_DIM, q, zero)
    qs_ref[tq:2 * tq, :] = jnp.where(lane >= DA_HEAD_DIM, q, zero)
    m_ref[...] = jnp.full(m_ref.shape, -1e30, F32)
    acc_ref[...] = jnp.zeros(acc_ref.shape, F32)

    def key_rows(t, nk):
        return pl.ds(t * tk if isinstance(t, int) else pl.multiple_of(t * tk, tk), nk)

    def keys(t):
        if isinstance(t, int) and t == n_lat_k:
            return kc_ref[...], n_ctx
        return kr_ref[key_rows(t, tk), :], tk

    def width(t):
        return n_ctx if isinstance(t, int) and t == n_lat_k else tk

    def stage(t_next, t_cur, t_prev, buf):
        other = 1 - buf
        if t_next is not None:
            k_next, nk_next = keys(t_next)
        if t_prev is not None:
            nk_prev = width(t_prev)
            vx = vx_ref[key_rows(t_prev, nk_prev), :]
        rows_c = 2 * tq // ATTN_STAGE_CHUNKS
        for c in range(ATTN_STAGE_CHUNKS):
            rc = slice(c * rows_c, (c + 1) * rows_c)
            if t_next is not None:
                s_ref[other, rc, 0:nk_next] = lax.dot_general(qs_ref[rc, :], k_next, (((1,), (1,)), ((), ())),
                                                              preferred_element_type=F32)
            if t_cur is not None:
                nk = width(t_cur)
                reps = nk // LANES
                for r in range(c * rows_c // rb, (c + 1) * rows_c // rb):
                    rows = slice(r * rb, (r + 1) * rb)
                    m_prev = m_ref[rows, :]
                    m_new = jnp.maximum(m_prev, jnp.max(s_ref[buf, rows, 0:nk], axis=-1, keepdims=True))
                    a_ref[buf, rows, :] = jnp.exp(m_prev - m_new)
                    m_ref[rows, :] = m_new
                    p = jnp.exp(s_ref[buf, rows, 0:nk] - jnp.concatenate([m_new] * reps, axis=1))
                    p_ref[buf, rows, 0:nk] = p.astype(BF16)
            if t_prev is not None:
                alpha = a_ref[other, rc, :]
                pv = jnp.dot(p_ref[other, rc, 0:nk_prev], vx, preferred_element_type=F32)
                acc_ref[rc, :] = jnp.concatenate([alpha, alpha], axis=1) * acc_ref[rc, :] + pv

    last = n_lat_k
    loop_pairs = (n_lat_k - 2) // 2 if (n_lat_k >= 4 and n_lat_k % 2 == 0) else 0
    stage(0, None, None, 1)
    t = 0
    while t <= last:
        if t == 1 and loop_pairs:
            def body(i, carry):
                stage(2 + 2 * i, 1 + 2 * i, 2 * i, 1)
                stage(3 + 2 * i, 2 + 2 * i, 1 + 2 * i, 0)
                return carry
            lax.fori_loop(0, loop_pairs, body, 0)
            t += 2 * loop_pairs
            continue
        stage(t + 1 if t < last else None, t, t - 1 if t >= 1 else None, t % 2)
        t += 1
    stage(None, None, last, (last + 1) % 2)

    lf = lam_ref[...]
    lam = (jnp.exp(jnp.sum(lf[0:1] * lf[1:2], axis=-1, keepdims=True))
           - jnp.exp(jnp.sum(lf[2:3] * lf[3:4], axis=-1, keepdims=True)) + lam_init)
    o = acc_ref[:, 0:LANES] / acc_ref[:, LANES:2 * LANES]
    out = o[:tq] - lam * o[tq:]
    y = _rms(out) * subln_ref[...] * (1.0 - lam_init)
    o_ref[...] = y.astype(BF16)


def _attn_call(p, tables, da_lambda, da_subln, lam_init, dims, *, tq, q_row0, n_q, rows, with_lat):
    batch, seq, ctx_len, nl = dims["batch"], dims["seq"], dims["ctx"], dims["nl"]
    tk = 512
    k_col, v_col = DA_WIDTH // LANES, V_OFF // LANES
    q_map = lambda b, h, i: (q_row0 // tq + b * n_q + i, h)
    o_map = lambda b, h, i: (b * n_q + i, h)
    kv_lat = lambda off: pl.BlockSpec((seq, LANES), lambda b, h, i: (b, off + h))
    kv_ctx = lambda off: pl.BlockSpec((ctx_len, LANES), lambda b, h, i: (nl // ctx_len + b, off + h))
    in_specs = [pl.BlockSpec((4, DA_HEAD_DIM), lambda b, h, i: (0, 0)),
                pl.BlockSpec((1, DA_V_DIM), lambda b, h, i: (0, 0)),
                pl.BlockSpec((tq, LANES), q_map),
                kv_ctx(k_col), kv_ctx(v_col)]
    args = [da_lambda, da_subln.reshape(1, DA_V_DIM), p, p, p]
    n_keys = ctx_len + (seq if with_lat else 0)
    scratch = [pltpu.VMEM((2 * tq, LANES), BF16), pltpu.VMEM((2, 2 * tq, tk), F32), pltpu.VMEM((2, 2 * tq, tk), BF16),
               pltpu.VMEM((2 * tq, LANES), F32), pltpu.VMEM((2, 2 * tq, LANES), F32),
               pltpu.VMEM((2 * tq, 2 * LANES), F32), pltpu.VMEM((n_keys, 2 * LANES), BF16)]
    if with_lat:
        tab = pl.BlockSpec((seq, LANES), lambda b, h, i: (0, 0))
        in_specs += [kv_lat(k_col), kv_lat(v_col), tab, tab, tab]
        args += [p, p, *tables]
        scratch = [pltpu.VMEM((seq, LANES), BF16)] + scratch
    kern = functools.partial(_attn_kernel, tq=tq, tk=tk, n_lat_k=seq // tk if with_lat else 0, lam_init=lam_init)
    return pl.pallas_call(
        kern,
        out_shape=jax.ShapeDtypeStruct((rows, DA_WIDTH), BF16),
        grid=(batch, DA_HEADS, n_q),
        in_specs=in_specs,
        out_specs=pl.BlockSpec((tq, LANES), o_map),
        scratch_shapes=scratch,
        compiler_params=_cparams("parallel", "parallel", "arbitrary"),
        name="diff_attn" if with_lat else "diff_attn_ctx",
    )(*args)


def _attention(p, tables, da_lambda, da_subln, lam_init, with_ctx, dims):
    batch, seq, ctx_len, nl = dims["batch"], dims["seq"], dims["ctx"], dims["nl"]
    tq = _pick_tile(seq, (1024, 512, TOK_TILE))
    y = _attn_call(p, tables, da_lambda, da_subln, lam_init, dims, tq=tq, q_row0=0, n_q=seq // tq, rows=nl,
                   with_lat=True)
    if with_ctx:
        tqc = TOK_TILE
        y_ctx = _attn_call(p, tables, da_lambda, da_subln, lam_init, dims, tq=tqc, q_row0=nl, n_q=ctx_len // tqc,
                           rows=batch * ctx_len, with_lat=False)
        y = jnp.concatenate([y, y_ctx], axis=0)
    return y


def _softplus(x):
    return jnp.maximum(x, 0.0) + jnp.log(1.0 + jnp.exp(-jnp.abs(x)))


def _dt_lanes(h, w, bias, alog):
    raw = lax.dot_general(h, w.astype(BF16), (((1,), (1,)), ((), ())), preferred_element_type=F32)
    n = 2 * SSD_HEADS
    dt = _softplus(raw + bias)
    lane = lax.broadcasted_iota(jnp.int32, dt.shape, 1)
    dta = pltpu.roll(dt, n, 1) * (-jnp.exp(alog))
    return jnp.where(lane < n, dt, jnp.where(lane < 2 * n, dta, 0.0))


def _conv_kernel(prev_ref, x_ref, next_ref, w_ref, b_ref, h_ref, wdt_ref, bdt_ref, alog_ref, o_ref, dd_ref,
                 *, tm, n_lat_tiles, lat_per_seq, ctx_per_seq):
    @pl.when(pl.program_id(1) == 0)
    def _():
        dd_ref[...] = _dt_lanes(h_ref[...], wdt_ref[...], bdt_ref[...], alog_ref[...])

    i = pl.program_id(0)
    is_lat = i < n_lat_tiles
    pos = jnp.where(is_lat, i % lat_per_seq, (i - n_lat_tiles) % ctx_per_seq)
    per = jnp.where(is_lat, lat_per_seq, ctx_per_seq)
    keep_prev = (pos > 0).astype(F32)
    keep_next = (pos < per - 1).astype(F32)
    ext = jnp.concatenate([prev_ref[...].astype(F32) * keep_prev, x_ref[...].astype(F32),
                           next_ref[...].astype(F32) * keep_next], axis=0)
    n = tm + 2 * SUBLANES
    w = w_ref[...]
    acc = jnp.zeros((tm, ext.shape[1]), F32) + b_ref[...]
    for k in range(SSD_CONV):
        shift = (SSD_CONV // 2 - k) % n
        rolled = ext if shift == 0 else pltpu.roll(ext, shift, 0)
        acc = acc + rolled[SUBLANES:SUBLANES + tm] * w[k:k + 1, :]
    o_ref[...] = _silu(acc).astype(BF16)


def _ssd_prep(p, conv_w, conv_b, h, w_in_t, layer, bias2, alog2, dims):
    t = p.shape[0]
    d = h.shape[1]
    tm = TOK_TILE
    cb = 512
    col0 = XBC_OFF // cb
    r8 = tm // SUBLANES
    last8 = t // SUBLANES - 1
    assert REF_DT_OFF % LANES == 0
    kern = functools.partial(_conv_kernel, tm=tm, n_lat_tiles=dims["nl"] // tm, lat_per_seq=dims["seq"] // tm,
                             ctx_per_seq=dims["ctx"] // tm)
    return pl.pallas_call(
        kern,
        out_shape=(jax.ShapeDtypeStruct((t, SSD_XBC), BF16), jax.ShapeDtypeStruct((t, LANES), F32)),
        grid=(t // tm, SSD_XBC // cb),
        in_specs=[pl.BlockSpec((SUBLANES, cb), lambda i, j: (jnp.maximum(i * r8 - 1, 0), col0 + j)),
                  pl.BlockSpec((tm, cb), lambda i, j: (i, col0 + j)),
                  pl.BlockSpec((SUBLANES, cb), lambda i, j: (jnp.minimum((i + 1) * r8, last8), col0 + j)),
                  pl.BlockSpec((SSD_CONV, cb), lambda i, j: (0, j)),
                  pl.BlockSpec((1, cb), lambda i, j: (0, j)),
                  pl.BlockSpec((tm, d), lambda i, j: (i, 0)),
                  pl.BlockSpec((None, LANES, d), lambda i, j: (layer, REF_DT_OFF // LANES, 0)),
                  pl.BlockSpec((1, LANES), lambda i, j: (0, 0)),
                  pl.BlockSpec((1, LANES), lambda i, j: (0, 0))],
        out_specs=(pl.BlockSpec((tm, cb), lambda i, j: (i, j)), pl.BlockSpec((tm, LANES), lambda i, j: (i, 0))),
        compiler_params=_cparams("parallel", "arbitrary"),
        name="ssd_prep",
    )(p, p, p, conv_w, conv_b.reshape(1, SSD_XBC), h, w_in_t, bias2, alog2)


def _split3(x):
    hi = x.astype(BF16)
    rest = x - hi.astype(F32)
    mid = rest.astype(BF16)
    return hi, mid, (rest - mid.astype(F32)).astype(BF16)


def _ssd_scan_kernel(*refs, direction, final, ncc, write_ctx):
    if final:
        xbc_ref, dd_ref, yf_ref, z_ref, dskip_ref, norm_ref, o_ref, h_ref = refs
    else:
        xbc_ref, dd_ref, o_ref, h_ref = refs
    j = pl.program_id(1)
    q = SSD_CHUNK
    gw = SSD_HPG * SSD_HEAD_DIM

    @pl.when(j == 0)
    def _():
        h_ref[...] = jnp.zeros_like(h_ref)

    xbc = xbc_ref[...]
    xs = xbc[:, :SSD_WIDTH].astype(F32)
    bm = xbc[:, SSD_WIDTH:SSD_WIDTH + SSD_GROUPS * SSD_STATE]
    cm = xbc[:, SSD_WIDTH + SSD_GROUPS * SSD_STATE:]
    dd = dd_ref[...]

    ii =lax.broadcasted_iota(jnp.int32, (q, q), 0)
    jj = lax.broadcasted_iota(jnp.int32, (q, q), 1)
    if direction == 0:
        mask = jj <= ii
        last = q - 1
    else:
        mask = jj >= ii
        last = 0
    tri = jnp.where(mask, 1.0, 0.0).astype(BF16)
    ac = sum(jnp.dot(tri, piece, preferred_element_type=F32) for piece in _split3(dd))
    ac_t = ac.T

    er = lax.broadcasted_iota(jnp.int32, (LANES, SSD_WIDTH), 0)
    ec = lax.broadcasted_iota(jnp.int32, (LANES, SSD_WIDTH), 1) // SSD_HEAD_DIM
    dt_col = direction * SSD_HEADS
    ac_col = 2 * SSD_HEADS + direction * SSD_HEADS
    e_dt = jnp.where(er == ec + dt_col, 1.0, 0.0).astype(BF16)
    e_ac = jnp.where(er == ec + ac_col, 1.0, 0.0).astype(BF16)
    dt_exp = sum(jnp.dot(piece, e_dt, preferred_element_type=F32) for piece in _split3(dd))
    ac_exp = sum(jnp.dot(piece, e_ac, preferred_element_type=F32) for piece in _split3(ac))
    ac_last = ac_exp[last:last + 1, :]
    eac = jnp.exp(ac_exp)
    dec_end = jnp.exp(ac_last - ac_exp)
    chunk_dec = jnp.exp(ac_last)
    xdt = xs * dt_exp
    xdt_b = xdt.astype(BF16)
    xde_b = (xdt * dec_end).astype(BF16)

    lane = lax.broadcasted_iota(jnp.int32, (q, LANES), 1)
    left = lane < SSD_HEAD_DIM
    zero_b = jnp.zeros((q, LANES), BF16)
    pieces = []
    for g in range(SSD_GROUPS):
        bg = bm[:, g * SSD_STATE:(g + 1) * SSD_STATE]
        cg = cm[:, g * SSD_STATE:(g + 1) * SSD_STATE]
        cb = lax.dot_general(cg, bg, (((1,), (1,)), ((), ())), preferred_element_type=F32)
        h_t = h_ref[g]
        y_off = jnp.dot(cg, h_t.astype(BF16), preferred_element_type=F32) * eac[:, g * gw:(g + 1) * gw]
        bg_t = bg.astype(F32).T.astype(BF16)
        s_t = jnp.dot(bg_t, xde_b[:, g * gw:(g + 1) * gw], preferred_element_type=F32)
        h_ref[g] = h_t * chunk_dec[:, g * gw:(g + 1) * gw] + s_t
        for kp in range(SSD_HPG // 2):
            k0 = g * SSD_HPG + 2 * kp
            ms = []
            for k in (k0, k0 + 1):
                c = ac_col + k
                seg = ac[:, c:c + 1] - ac_t[c:c + 1, :]
                ms.append((cb * jnp.exp(jnp.where(mask, seg, -1e30))).astype(BF16))
            xp = xdt_b[:, k0 * SSD_HEAD_DIM:k0 * SSD_HEAD_DIM + LANES]
            y_diag = jnp.dot(jnp.concatenate(ms, axis=1),
                             jnp.concatenate([jnp.where(left, xp, zero_b), jnp.where(left, zero_b, xp)], axis=0),
                             preferred_element_type=F32)
            pieces.append(y_diag + y_off[:, kp * LANES:(kp + 1) * LANES])
    y = jnp.concatenate(pieces, axis=1)

    def emit():
        if final:
            yt = y + yf_ref[...] + dskip_ref[...] * xs
            yt = yt * _silu(z_ref[...].astype(F32))
            o_ref[...] = (_rms(yt) * norm_ref[...]).astype(BF16)
        else:
            o_ref[...] = y

    if write_ctx:
        emit()
    else:
        pl.when(j >= ncc)(emit)


def _ssd_scan(xbc, dd, direction, dims, with_ctx, final_args=None):
    batch, seq, ctx_len, nl = dims["batch"], dims["seq"], dims["ctx"], dims["nl"]
    q = SSD_CHUNK
    ncl, ncc = seq // q, ctx_len // q
    final = final_args is not None
    rows = nl + (batch * ctx_len if with_ctx else 0)

    def chunk(b, j):
        jc = j if direction == 0 else ncc - 1 - j
        jl = (j - ncc) if direction == 0 else ncl - 1 - (j - ncc)
        return jnp.where(j < ncc, nl // q + b * ncc + jc, b * ncl + jl)

    def out_chunk(b, j):
        if with_ctx:
            return chunk(b, j)
        return chunk(b, jnp.maximum(j, ncc))

    in_specs = [pl.BlockSpec((q, SSD_XBC), lambda b, j: (chunk(b, j), 0)),
                pl.BlockSpec((q, LANES), lambda b, j: (chunk(b, j), 0))]
    args = [xbc, dd]
    if final:
        yf, p, dskip, norm = final_args
        in_specs += [pl.BlockSpec((q, SSD_WIDTH), lambda b, j: (out_chunk(b, j), 0)),
                     pl.BlockSpec((q, SSD_WIDTH), lambda b, j: (chunk(b, j), Z_OFF // SSD_WIDTH)),
                     pl.BlockSpec((1, SSD_WIDTH), lambda b, j: (0, 0)),
                     pl.BlockSpec((1, SSD_WIDTH), lambda b, j: (0, 0))]
        args += [yf, p, dskip, norm]
    return pl.pallas_call(
        functools.partial(_ssd_scan_kernel, direction=direction, final=final, ncc=ncc, write_ctx=with_ctx),
        out_shape=jax.ShapeDtypeStruct((rows, SSD_WIDTH), BF16 if final else F32),
        grid=(batch, ncc + ncl),
        in_specs=in_specs,
        out_specs=pl.BlockSpec((q, SSD_WIDTH), lambda b, j: (out_chunk(b, j), 0)),
        scratch_shapes=[pltpu.VMEM((SSD_GROUPS, SSD_STATE, SSD_HPG * SSD_HEAD_DIM), F32)],
        compiler_params=_cparams("parallel", "arbitrary"),
        name="ssd_scan_bwd" if direction else "ssd_scan_fwd",
    )(*args)


S5_BLOCK_GROUPS = LANES // S5_GROUP
S5_BLOCKS = S5_GROUPS // S5_BLOCK_GROUPS
S5_BS = S5_BLOCK_GROUPS * S5_STATE
S5_LAG_ROWS = 24


def _expm1(x):
    poly = 1.0 + x / 10.0
    for n in range(9, 1, -1):
        poly = 1.0 + (x / n) * poly
    return jnp.where(jnp.abs(x) < 0.35, x * poly, jnp.exp(x) - 1.0)


def _s5_params(lam_re, lam_im, log_step, b_re, b_im, c_re, c_im, s5_d):
    nb, gl = S5_BLOCKS, S5_BLOCK_GROUPS
    rows = [lam_re[0], lam_im[0], lam_re[1], lam_im[1],
            jnp.repeat(log_step[0], S5_STATE), jnp.repeat(log_step[1], S5_STATE)]
    rows = [r.reshape(nb, S5_BS) for r in rows] + [jnp.zeros((nb, S5_BS), F32)] * 2
    lam_rows = jnp.stack(rows, axis=1)
    eye = jnp.eye(gl, dtype=F32)

    def bd_in(b):
        return jnp.einsum("jgpe,gh->jgehp", b.reshape(nb, gl, S5_STATE, S5_GROUP), eye).reshape(nb, LANES, S5_BS)

    def bd_out(c):
        return jnp.einsum("jgfp,gh->jgfhp", c.reshape(nb, gl, S5_GROUP, S5_STATE), eye).reshape(nb, LANES, S5_BS)

    b_bd = jnp.stack([bd_in(b_re), bd_in(b_im)], axis=1)
    ct_bd = jnp.stack([bd_out(c_re), bd_out(c_im)], axis=1)
    return lam_rows, b_bd, ct_bd, s5_d.reshape(nb, 1, LANES)


def _gelu_tanh(x):
    return 0.5 * x * (1.0 + jnp.tanh(math.sqrt(2.0 / math.pi) * (x + 0.044715 * x * x * x)))


def _s5_kernel(u_ref, lam_ref, b_ref, ct_ref, d_ref, o_ref,
               x_ref, w_ref, m_ref, s_ref, y_ref, taps_ref, pw_ref, bb_ref, *, batch, ncl, ncc):
    tc = S5_CHUNK
    nch = x_ref.shape[0]
    sw = S5_BS
    nt = (((1,), (1,)), ((), ()))

    for s in range(tc):
        x_ref[:, s * LANES:(s + 1) * LANES] = u_ref[pl.ds(s, nch, stride=tc), :].astype(BF16)

    rows = lam_ref[0]
    kk = lax.broadcasted_iota(jnp.int32, (S5_LAG_ROWS, sw), 0).astype(F32)
    b_re, b_im = b_ref[0, 0], b_ref[0, 1]
    for d in range(2):
        l_re, l_im = rows[2 * d:2 * d + 1], rows[2 * d + 1:2 * d + 2]
        delta = jnp.exp(rows[4 + d:5 + d])
        lr, li = l_re * delta, l_im * delta
        mag = jnp.exp(kk * lr)
        pw_ref[2 * d] = mag * jnp.cos(kk * li)
        pw_ref[2 * d + 1] = mag * jnp.sin(kk * li)
        xr = _expm1(lr) * jnp.cos(li) - 2.0 * jnp.sin(0.5 * li) ** 2
        xi = jnp.exp(lr) * jnp.sin(li)
        den = l_re * l_re + l_im * l_im
        co_re = (xr * l_re + xi * l_im) / den
        co_im = (xi * l_re - xr * l_im) / den
        bb_ref[2 * d] = co_re * b_re - co_im * b_im
        bb_ref[2 * d + 1] = co_re * b_im + co_im * b_re

    def power(d, lag):
        return pw_ref[2 * d, lag:lag + 1, :], pw_ref[2 * d + 1, lag:lag + 1, :]

    ct_re, ct_im = ct_ref[0, 0], ct_ref[0, 1]
    ct_re_b, ct_im_b = ct_re.astype(BF16), ct_im.astype(BF16)
    for d in range(2):
        tp = []
        for part in range(2):
            q = 2 * d + part
            for s in range(tc):
                pr, pi = power(d, tc - 1 - s if d == 0 else s)
                if part == 0:
                    tile = pr * bb_ref[2 * d] - pi * bb_ref[2 * d + 1]
                else:
                    tile = pr * bb_ref[2 * d + 1] + pi * bb_ref[2 * d]
                w_ref[q, s * LANES:(s + 1) * LANES, :] = tile.astype(BF16)
            w = w_ref[q]
            s_ref[:, q * sw:(q + 1) * sw] = jnp.dot(x_ref[...], w, preferred_element_type=F32)
            tp.append(lax.dot_general(w, ct_re_b if part == 0 else ct_im_b, nt, preferred_element_type=F32))
        taps_ref[d] = tp[0] - tp[1]

    ri = lax.broadcasted_iota(jnp.int32, (LANES, LANES), 0)
    ci = lax.broadcasted_iota(jnp.int32, (LANES, LANES), 1)
    skip = jnp.where(ri == ci, d_ref[0], 0.0)
    for s in range(tc):
        for t in range(tc):
            if t >= s:
                lag_tile = tc - 1 - (t - s)
                tile = taps_ref[0, lag_tile * LANES:(lag_tile + 1) * LANES, :]
                if t == s:
                    tile = tile + taps_ref[1, 0:LANES, :] + skip
            else:
                tile = taps_ref[1, (s - t) * LANES:(s - t + 1) * LANES, :]
            m_ref[s * LANES:(s + 1) * LANES, t * LANES:(t + 1) * LANES] = tile.astype(BF16)

    af_re, af_im = power(0, tc)
    ab_re, ab_im = power(1, tc)

    def advance(row, h, dir_off, ar, ai):
        h_re, h_im = h
        s_re = s_ref[pl.ds(row, 1), dir_off:dir_off + sw]
        s_im = s_ref[pl.ds(row, 1), dir_off + sw:dir_off + 2 * sw]
        s_ref[pl.ds(row, 1), dir_off:dir_off + sw] = h_re
        s_ref[pl.ds(row, 1), dir_off + sw:dir_off + 2 * sw] = h_im
        return ar * h_re - ai * h_im + s_re, ar * h_im + ai * h_re + s_im

    def sweep(first_chunk, n):
        def body(i, c):
            return tuple((advance(first_chunk(b) + i, c[b][0], 0, af_re, af_im),
                          advance(first_chunk(b) + n - 1 - i, c[b][1], 2 * sw, ab_re, ab_im)) for b in range(batch))
        return body

    zero = jnp.zeros((1, sw), F32)
    c = lax.fori_loop(0, ncc, sweep(lambda b: batch * ncl + b * ncc, ncc), (((zero, zero), (zero, zero)),) * batch)
    lax.fori_loop(0, ncl, sweep(lambda b: b * ncl, ncl), c)

    y = jnp.dot(x_ref[...], m_ref[...], preferred_element_type=F32)
    for q in range(4):
        d, part = divmod(q, 2)
        for t in range(tc):
            pr, pi = power(d, t + 1 if d == 0 else tc - t)
            tile = ct_re * pr - ct_im * pi if part == 0 else -(ct_re * pi + ct_im * pr)
            w_ref[q, t * LANES:(t + 1) * LANES, :] = tile.astype(BF16)
        h = s_ref[:, q * sw:(q + 1) * sw].astype(BF16)
        y = y + lax.dot_general(h, w_ref[q], nt, preferred_element_type=F32)
    y_ref[...] = y
    for t in range(tc):
        o_ref[pl.ds(t, nch, stride=tc), :] = _gelu_tanh(y_ref[:, t * LANES:(t + 1) * LANES])


def _s5(u, params, dims):
    lam_rows, b_bd, ct_bd, d_skip = params
    t = u.shape[0]
    nch = t // S5_CHUNK
    ncl, ncc = dims["seq"] // S5_CHUNK, dims["ctx"] // S5_CHUNK
    width = S5_CHUNK * LANES
    pspec = pl.BlockSpec((1, 2, LANES, S5_BS), lambda i: (i, 0, 0, 0))
    return pl.pallas_call(
        functools.partial(_s5_kernel, batch=dims["batch"], ncl=ncl, ncc=ncc),
        out_shape=jax.ShapeDtypeStruct((t, S5_WIDTH), F32),
        grid=(S5_BLOCKS,),
        in_specs=[pl.BlockSpec((t, LANES), lambda i: (0, i)),
                  pl.BlockSpec((1, SUBLANES, S5_BS), lambda i: (i, 0, 0)), pspec, pspec,
                  pl.BlockSpec((1, 1, LANES), lambda i: (i, 0, 0))],
        out_specs=pl.BlockSpec((t, LANES), lambda i: (0, i)),
        scratch_shapes=[pltpu.VMEM((nch, width), BF16),
                        pltpu.VMEM((4, width, S5_BS), BF16),
                        pltpu.VMEM((width, width), BF16),
                        pltpu.VMEM((nch, 4 * S5_BS), F32),
                        pltpu.VMEM((nch, width), F32),
                        pltpu.VMEM((2, width, LANES), F32),
                        pltpu.VMEM((4, S5_LAG_ROWS, S5_BS), F32),
                        pltpu.VMEM((4, LANES, S5_BS), F32)],
        compiler_params=_cparams("parallel"),
        name="s5_scan",
    )(u, lam_rows, b_bd, ct_bd, d_skip)


def _glu_kernel(x_ref, wa_ref, wb_ref, ba_ref, bb_ref, o_ref, wab_ref, wbb_ref):
    @pl.when(pl.program_id(1) == 0)
    def _():
        wab_ref[...] = wa_ref[...].astype(BF16)
        wbb_ref[...] = wb_ref[...].astype(BF16)

    x = x_ref[...].astype(BF16)
    a = jnp.dot(x, wab_ref[...], preferred_element_type=F32) + ba_ref[...]
    b = jnp.dot(x, wbb_ref[...], preferred_element_type=F32) + bb_ref[...]
    o_ref[...] = (a * _sigmoid(b)).astype(BF16)


def _glu(x, w, bias, layer, rows, tm):
    k = x.shape[1]
    tn = W_TILE
    nb = S5_WIDTH // tn
    return pl.pallas_call(
        _glu_kernel,
        out_shape=jax.ShapeDtypeStruct((rows, S5_WIDTH), BF16),
        grid=(nb, rows // tm),
        in_specs=[pl.BlockSpec((tm, k), lambda j, i: (i, 0)),
                  pl.BlockSpec((None, k, tn), lambda j, i: (layer, 0, j)),
                  pl.BlockSpec((None, k, tn), lambda j, i: (layer, 0, nb + j)),
                  pl.BlockSpec((None, 1, tn), lambda j, i: (layer, 0, j)),
                  pl.BlockSpec((None, 1, tn), lambda j, i: (layer, 0, nb + j))],
        out_specs=pl.BlockSpec((tm, tn), lambda j, i: (i, j)),
        scratch_shapes=[pltpu.VMEM((k, tn), BF16), pltpu.VMEM((k, tn), BF16)],
        compiler_params=_cparams("parallel", "arbitrary"),
        name="s5_glu",
    )(x, w, w, bias, bias)


def _merge_kernel(ya_ref, yb_ref, yc_ref, wa_ref, wb_ref, wc_ref, ga_ref, gb_ref, gc_ref, o_ref, wbf_ref):
    @pl.when(pl.program_id(1) == 0)
    def _():
        for n, w_ref in enumerate((wa_ref, wb_ref, wc_ref)):
            wbf_ref[n] = w_ref[0].astype(BF16)

    acc = None
    for n, (y_ref, g_ref) in enumerate(((ya_ref, ga_ref), (yb_ref, gb_ref), (yc_ref, gc_ref))):
        br = jnp.dot(y_ref[...], wbf_ref[n], preferred_element_type=F32)
        term = _sigmoid(g_ref[...].astype(F32)) * br
        acc = term if acc is None else acc + term
    o_ref[...] = acc.astype(BF16)


def _merge(ya, yb, yc, w_branch, layer, gates, rows, tm):
    k = ya.shape[1]
    d = w_branch.shape[3]
    tn = W_TILE
    gstep = d // tn
    yspec = pl.BlockSpec((tm, k), lambda j, i: (i, 0))
    wspec = lambda n: pl.BlockSpec((None, 1, k, tn), lambda j, i: (layer, n, 0, j))
    gspec = lambda n: pl.BlockSpec((tm, tn), lambda j, i: (i, n * gstep + j))
    return pl.pallas_call(
        _merge_kernel,
        out_shape=jax.ShapeDtypeStruct((rows, d), BF16),
        grid=(d // tn, rows // tm),
        in_specs=[yspec, yspec, yspec, wspec(0), wspec(1), wspec(2), gspec(0), gspec(1), gspec(2)],
        out_specs=pl.BlockSpec((tm, tn), lambda j, i: (i, j)),
        scratch_shapes=[pltpu.VMEM((N_BRANCH, k, tn), BF16)],
        compiler_params=_cparams("parallel", "arbitrary"),
        name="branch_merge",
    )(ya, yb, yc, w_branch, w_branch, w_branch, gates, gates, gates)


def _out_proj_kernel(*refs, n_lat):
    g_ref, w_ref, *x_refs, mod_ref, npost_ref, npre_ref, xo_ref, ho_ref = refs
    o = jnp.dot(g_ref[...], w_ref[...], preferred_element_type=F32)
    gate = mod_ref[0, 2:3, :]
    xn = _stream_tile(x_refs, n_lat) + gate * (_rms(o) * npost_ref[...])
    xo_ref[...] = xn
    shift = mod_ref[0, 3:4, :]
    scale = mod_ref[0, 4:5, :]
    ho_ref[...] = (_rms(xn) * npre_ref[...] * (1.0 + scale) + shift).astype(BF16)


def _out_proj(g, w_out, xs, mod, npost, npre, dims):
    r, d = g.shape
    tm = TOK_TILE
    seg = functools.partial(_seg_of_tile, tm=tm, n_lat=dims["nl"], seq=dims["seq"], batch=dims["batch"])
    row = pl.BlockSpec((tm, d), lambda i: (i, 0))
    vec = pl.BlockSpec((1, d), lambda i: (0, 0))
    return pl.pallas_call(
        functools.partial(_out_proj_kernel, n_lat=dims["nl"] // tm),
        out_shape=(jax.ShapeDtypeStruct((r, d), F32), jax.ShapeDtypeStruct((r, d), BF16)),
        grid=(r // tm,),
        in_specs=[row, pl.BlockSpec((d, d), lambda i: (0, 0))]
        + _stream_specs(xs, tm, d, dims["nl"])
        + [pl.BlockSpec((1, 6, d), lambda i: (seg(i), 0, 0)), vec, vec],
        out_specs=(row, row),
        compiler_params=_cparams("parallel"),
        name="out_proj",
    )(g, w_out, *xs, mod, npost.reshape(1, d), npre.reshape(1, d))


def _ffn_up_kernel(h_ref, wg_ref, wu_ref, o_ref, wgb_ref, wub_ref):
    @pl.when(pl.program_id(1) == 0)
    def _():
        wgb_ref[...] = wg_ref[...].astype(BF16)
        wub_ref[...] = wu_ref[...].astype(BF16)

    h = h_ref[...]
    a = jnp.dot(h, wgb_ref[...], preferred_element_type=F32)
    b = jnp.dot(h, wub_ref[...], preferred_element_type=F32)
    o_ref[...] = (_silu(a) * b).astype(BF16)


def _ffn_up(h, wg, wu, layer, tm):
    r, d = h.shape
    f = wg.shape[2]
    tn = W_TILE
    wspec = pl.BlockSpec((None, d, tn), lambda j, i: (layer, 0, j))
    return pl.pallas_call(
        _ffn_up_kernel,
        out_shape=jax.ShapeDtypeStruct((r, f), BF16),
        grid=(f // tn, r // tm),
        in_specs=[pl.BlockSpec((tm, d), lambda j, i: (i, 0)), wspec, wspec],
        out_specs=pl.BlockSpec((tm, tn), lambda j, i: (i, j)),
        scratch_shapes=[pltpu.VMEM((d, tn), BF16), pltpu.VMEM((d, tn), BF16)],
        compiler_params=_cparams("parallel", "arbitrary"),
        name="ffn_up",
    )(h, wg, wu)


def _ffn_down_kernel(a_ref, w_ref, x_ref, mod_ref, npost_ref, o_ref, acc_ref):
    k = pl.program_id(1)

    @pl.when(k == 0)
    def _():
        acc_ref[...] = jnp.zeros_like(acc_ref)

    acc_ref[...] += jnp.dot(a_ref[...], w_ref[...], preferred_element_type=F32)

    @pl.when(k == pl.num_programs(1) - 1)
    def _():
        gate = mod_ref[0, 5:6, :]
        o_ref[...] = x_ref[...] + gate * (_rms(acc_ref[...]) * npost_ref[...])


def _ffn_down(act, wd, x, mod, npost, dims):
    r, f = act.shape
    d = wd.shape[1]
    big = r == dims["nl"] and dims["seq"] % 1024 == 0 and f % (8 * LANES) == 0
    tm, nk = (1024, 8) if big else (2 * TOK_TILE, 4)
    tk = f // nk
    seg = functools.partial(_seg_of_tile, tm=tm, n_lat=dims["nl"], seq=dims["seq"], batch=dims["batch"])
    row = pl.BlockSpec((tm, d), lambda i, k: (i, 0))
    return pl.pallas_call(
        _ffn_down_kernel,
        out_shape=jax.ShapeDtypeStruct((r, d), F32),
        grid=(r // tm, nk),
        in_specs=[pl.BlockSpec((tm, tk), lambda i, k: (i, k)),
                  pl.BlockSpec((tk, d), lambda i, k: (k, 0)),
                  row,
                  pl.BlockSpec((1, 6, d), lambda i, k: (seg(i), 0, 0)),
                  pl.BlockSpec((1, d), lambda i, k: (0, 0))],
        out_specs=row,
        scratch_shapes=[pltpu.VMEM((tm, d), F32)],
        compiler_params=_cparams("parallel", "arbitrary"),
        name="ffn_down",
    )(act, wd, x, mod, npost.reshape(1, d))


def _pick_tile(rows, cands):
    for c in cands:
        if rows % c == 0:
            return c
    raise ValueError(f"no tile in {cands} divides {rows}")


def kernel(x, c, ctx, c_ctx, ada_w, ada_b, norm_mix_pre, norm_mix_post, norm_ffn_pre, norm_ffn_post, w_in, da_lambda, da_subln, ssd_conv_w, ssd_conv_b, ssd_dt_bias, ssd_a_log, ssd_d, ssd_norm, s5_lam_re, s5_lam_im, s5_log_step, s5_b_re, s5_b_im, s5_c_re, s5_c_im, s5_d, s5_glu_w, s5_glu_b, w_branch, w_out, ffn_w_gate, ffn_w_up, ffn_w_down):
    batch, seq, d = x.shape
    ctx_len = ctx.shape[1]
    depth = ada_w.shape[0]
    nl, nc = batch * seq, batch * ctx_len
    dims = dict(batch=batch, seq=seq, ctx=ctx_len, nl=nl, nc=nc)
    assert batch < MOD_ROWS and seq % TOK_TILE == 0 and ctx_len % TOK_TILE == 0 and seq % ctx_len == 0
    assert nc % (2 * TOK_TILE) == 0 and seq % GRID_W == 0

    xs = (x.reshape(nl, d), ctx.reshape(nc, d))
    cc =jnp.concatenate([c, c_ctx[None], jnp.zeros((MOD_ROWS - batch - 1, d), F32)], axis=0)
    mod_all = _ada(cc, ada_w, ada_b).reshape(depth, MOD_ROWS, 6, d)
    rope_tabs = _rope_tables(seq, ctx_len)
    n_dt = 2 * SSD_HEADS
    pad = lambda v, before: jnp.concatenate(
        [jnp.zeros((1, before), F32), v.reshape(1, n_dt), jnp.zeros((1, LANES - n_dt - before), F32)], axis=1)
    tm_all = _pick_tile(nl + nc, (2176, 1088, 512, 256))
    w_in_t = jnp.swapaxes(w_in, 1, 2)

    for l in range(depth):
        last = l == depth - 1
        with_ctx = not last
        lam_init = 0.8 - 0.6 * math.exp(-0.3 * l)
        mod = mod_all[l]
        rows = nl + nc if with_ctx else nl
        tm = _pick_tile(rows, (1088, 1024, 512))

        h = _norm_mod(xs, norm_mix_pre[l], mod, 0, dims)
        p = _proj(h, w_in_t, l, 0, REF_DT_OFF, BF16, tm_all, "in_proj_main")
        u = _proj(h, w_in_t, l, REF_U_OFF, S5_WIDTH, F32, tm_all, "in_proj_u")
        gates = _proj(h, w_in_t, l, REF_U_OFF + S5_WIDTH, N_BRANCH * d, BF16, tm_all, "in_proj_gates")

        y_attn = _attention(p, rope_tabs, da_lambda[l], da_subln[l], lam_init, with_ctx, dims)

        xbc, dd = _ssd_prep(p, ssd_conv_w[l], ssd_conv_b[l], h, w_in_t, l,
                            pad(ssd_dt_bias[l], 0), pad(ssd_a_log[l], n_dt), dims)
        y_f = _ssd_scan(xbc, dd, 0, dims, with_ctx)
        dskip = jnp.repeat(ssd_d[l], SSD_HEAD_DIM).reshape(1, SSD_WIDTH)
        y_ssd = _ssd_scan(xbc, dd, 1, dims, with_ctx,
                          final_args=(y_f, p, dskip, ssd_norm[l].reshape(1, SSD_WIDTH)))

        s5p = _s5_params(s5_lam_re[l], s5_lam_im[l], s5_log_step[l], s5_b_re[l], s5_b_im[l],
                         s5_c_re[l], s5_c_im[l], s5_d[l])
        yg = _s5(u, s5p, dims)
        y_s5 = _glu(yg, s5_glu_w, s5_glu_b.reshape(depth, 1, 2 * S5_WIDTH), l, rows, tm)

        g = _merge(y_attn, y_ssd, y_s5, w_branch, l, gates, rows, tm)
        xt, h2 = _out_proj(g, _cast_bf16(w_out, l), xs, mod, norm_mix_post[l], norm_ffn_pre[l], dims)
        act = _ffn_up(h2, ffn_w_gate, ffn_w_up, l, tm)
        xt = _ffn_down(act, _cast_bf16(ffn_w_down, l), xt, mod, norm_ffn_post[l], dims)
        xs = (xt,)

    return xt[:nl].reshape(batch, seq, d)
```

```python
import functools
import math

import jax
import jax.numpy as jnp
from jax import lax
from jax.experimental import pallas as pl
from jax.experimental.pallas import tpu as pltpu

F32 = jnp.float32
BF16 = jnp.bfloat16
HIGHEST = lax.Precision.HIGHEST

GRID_W = 64
N_BRANCH = 3
DA_HEADS = 8
DA_HEAD_DIM = 64
DA_V_DIM = 2 * DA_HEAD_DIM
DA_WIDTH = DA_HEADS * DA_V_DIM
ROPE_THETA = 10000.0
SSD_HEADS = 16
SSD_HEAD_DIM = 64
SSD_GROUPS = 2
SSD_HPG = SSD_HEADS // SSD_GROUPS
SSD_STATE = 128
SSD_WIDTH = SSD_HEADS * SSD_HEAD_DIM
SSD_XBC = SSD_WIDTH + 2 * SSD_GROUPS * SSD_STATE
SSD_CONV = 5
SSD_CHUNK = 128
S5_GROUP = 16
S5_GROUPS = 64
S5_WIDTH = S5_GROUPS * S5_GROUP
S5_STATE = 64
S5_CHUNK = 16
RMS_EPS = 1e-6

V_OFF = 2 * DA_WIDTH
Z_OFF = 3 * DA_WIDTH
XBC_OFF = 4 * DA_WIDTH
REF_DT_OFF = XBC_OFF + SSD_XBC
REF_U_OFF = REF_DT_OFF + 2 * SSD_HEADS

LANES = 128
SUBLANES = 8
VMEM_LIMIT_BYTES = 52 * 1024 * 1024
MOD_ROWS = 8

TOK_TILE = 256


def _cparams(*sem):
    return pltpu.CompilerParams(dimension_semantics=sem, vmem_limit_bytes=VMEM_LIMIT_BYTES)


def _rms(x):
    return x * lax.rsqrt(jnp.mean(x * x, axis=-1, keepdims=True) + RMS_EPS)


def _sigmoid(x):
    return 1.0 / (1.0 + jnp.exp(-x))


def _silu(x):
    return x * _sigmoid(x)


def _seg_of_tile(i, tm, n_lat, seq, batch):
    return jnp.where(i < n_lat // tm, i // (seq // tm), batch)


def _ada_kernel(c_ref, w_ref, b_ref, o_ref):
    c = c_ref[...]
    o_ref[0] = jnp.dot(_silu(c), w_ref[0], precision=HIGHEST, preferred_element_type=F32) + b_ref[0]


def _ada(cc, ada_w, ada_b):
    depth, d, n = ada_w.shape
    tn = 2048
    return pl.pallas_call(
        _ada_kernel,
        out_shape=jax.ShapeDtypeStruct((depth, MOD_ROWS, n), F32),
        grid=(depth, n // tn),
        in_specs=[pl.BlockSpec((MOD_ROWS, d), lambda l, j: (0, 0)),
                  pl.BlockSpec((1, d, tn), lambda l, j: (l, 0, j)),
                  pl.BlockSpec((1, 1, tn), lambda l, j: (l, 0, j))],
        out_specs=pl.BlockSpec((1, MOD_ROWS, tn), lambda l, j: (l, 0, j)),
        compiler_params=_cparams("parallel", "parallel"),
        name="ada_mod",
    )(cc, ada_w, ada_b.reshape(depth, 1, n))


def _stream_specs(xs, tm, d, nl):
    n_lat = nl // tm
    if len(xs) == 1:
        return [pl.BlockSpec((tm, d), lambda i: (i, 0))]
    return [pl.BlockSpec((tm, d), lambda i: (jnp.minimum(i, n_lat - 1), 0)),
            pl.BlockSpec((tm, d), lambda i: (jnp.maximum(i - n_lat, 0), 0))]


def _stream_tile(x_refs, n_lat):
    if len(x_refs) == 1:
        return x_refs[0][...]
    return jnp.where(pl.program_id(0) < n_lat, x_refs[0][...], x_refs[1][...])


def _norm_mod_kernel(*refs, shift_idx, n_lat):
    *x_refs, g_ref, mod_ref, o_ref = refs
    y = _rms(_stream_tile(x_refs, n_lat)) * g_ref[...]
    shift = mod_ref[0, shift_idx:shift_idx + 1, :]
    scale = mod_ref[0, shift_idx + 1:shift_idx + 2, :]
    o_ref[...] = (y * (1.0 + scale) + shift).astype(BF16)


def _norm_mod(xs, g, mod, shift_idx, dims):
    d = xs[0].shape[1]
    t = dims["nl"] + dims["nc"]
    tm = TOK_TILE
    seg = functools.partial(_seg_of_tile, tm=tm, n_lat=dims["nl"], seq=dims["seq"], batch=dims["batch"])
    return pl.pallas_call(
        functools.partial(_norm_mod_kernel, shift_idx=shift_idx, n_lat=dims["nl"] // tm),
        out_shape=jax.ShapeDtypeStruct((t, d), BF16),
        grid=(t // tm,),
        in_specs=_stream_specs(xs, tm, d, dims["nl"]) + [
            pl.BlockSpec((1, d), lambda i: (0, 0)),
            pl.BlockSpec((1, 6, d), lambda i: (seg(i), 0, 0))],
        out_specs=pl.BlockSpec((tm, d), lambda i: (i, 0)),
        compiler_params=_cparams("parallel"),
        name="norm_mod",
    )(*xs, g.reshape(1, d), mod)


W_TILE = 512


def _shifted_rows(w0, w1, shift):
    return jnp.concatenate([w0[shift:], w1[:shift]], axis=0)


def _proj_kernel(*refs, shift):
    if shift:
        x_ref, w0_ref, w1_ref, o_ref, wb_ref = refs
    else:
        x_ref, w0_ref, o_ref, wb_ref = refs

    @pl.when(pl.program_id(1) == 0)
    def _():
        w = _shifted_rows(w0_ref[...], w1_ref[...], shift) if shift else w0_ref[...]
        wb_ref[...] = w.T.astype(BF16)

    o_ref[...] = jnp.dot(x_ref[...], wb_ref[...], preferred_element_type=F32).astype(o_ref.dtype)


def _proj(x, w_t, layer, col0, n_cols, out_dtype, tm, name):
    m, k = x.shape
    tn = W_TILE
    cb0, shift = col0 // tn, col0 % tn
    assert shift % SUBLANES == 0
    w_specs = [pl.BlockSpec((None, tn, k), lambda j, i: (layer, cb0 + j, 0))]
    if shift:
        w_specs.append(pl.BlockSpec((None, tn, k), lambda j, i: (layer, cb0 + j + 1, 0)))
    return pl.pallas_call(
        functools.partial(_proj_kernel, shift=shift),
        out_shape=jax.ShapeDtypeStruct((m, n_cols), out_dtype),
        grid=(n_cols // tn, m // tm),
        in_specs=[pl.BlockSpec((tm, k), lambda j, i: (i, 0))] + w_specs,
        out_specs=pl.BlockSpec((tm, tn), lambda j, i: (i, j)),
        scratch_shapes=[pltpu.VMEM((k, tn), BF16)],
        compiler_params=_cparams("parallel", "arbitrary"),
        name=name,
    )(x, *([w_t] * len(w_specs)))


def _cast_kernel(w_ref, o_ref):
    o_ref[...] = w_ref[...].astype(BF16)


def _cast_bf16(w, layer):
    _, r, n = w.shape
    tr = _pick_tile(r, (512, 256, 128, 8))
    return pl.pallas_call(
        _cast_kernel,
        out_shape=jax.ShapeDtypeStruct((r, n), BF16),
        grid=(r // tr,),
        in_specs=[pl.BlockSpec((None, tr, n), lambda i: (layer, i, 0))],
        out_specs=pl.BlockSpec((tr, n), lambda i: (i, 0)),
        compiler_params=_cparams("parallel"),
        name="cast_bf16",
    )(w)


def _rope_tables(seq, ctx_len):
    n_rows = seq // GRID_W
    row = jnp.repeat(jnp.arange(n_rows, dtype=F32), GRID_W)
    col = jnp.tile(jnp.arange(GRID_W, dtype=F32), n_rows)
    half = DA_HEAD_DIM // 2
    inv_freq = ROPE_THETA ** (-jnp.arange(0, half, 2, dtype=F32) / half)
    ar = row[:, None] * inv_freq[None, :]
    ac = col[:, None] * inv_freq[None, :]
    ang = jnp.concatenate([ar, ar, ac, ac], axis=-1)
    ang = jnp.concatenate([ang, jnp.zeros((ctx_len, DA_HEAD_DIM), F32)], axis=0)
    cos = jnp.tile(jnp.cos(ang), (1, 2))
    sin = jnp.tile(jnp.sin(ang), (1, 2))
    first = (jnp.arange(LANES) % half) < (half // 2)
    sin_a = jnp.where(first[None, :], -sin, 0.0)
    sin_b = jnp.where(first[None, :], 0.0, sin)
    return cos, sin_a, sin_b


ATTN_STAGE_CHUNKS = 4
ATTN_ROW_BLOCK = 64


def _rope_rows(x, cos, sin_a, sin_b):
    quarter = DA_HEAD_DIM // 4
    return x * cos + pltpu.roll(x, LANES - quarter, 1) * sin_a + pltpu.roll(x, quarter, 1) * sin_b


def _attn_kernel(*refs, tq, tk, n_lat_k, lam_init):
    if n_lat_k:
        lam_ref, subln_ref, q_ref, kc_ref, vc_ref, kl_ref, vl_ref, cos_ref, sa_ref, sb_ref, o_ref = refs[:11]
        kr_ref = refs[-8]
    else:
        lam_ref, subln_ref, q_ref, kc_ref, vc_ref, o_ref = refs[:6]
    qs_ref, s_ref, p_ref, m_ref, a_ref, acc_ref, vx_ref = refs[-7:]
    rb = ATTN_ROW_BLOCK
    n_ctx = kc_ref.shape[0]
    n_lat = n_lat_k * tk
    scale = DA_HEAD_DIM ** -0.5

    @pl.when(pl.program_id(2) == 0)
    def _():
        if n_lat_k:
            vx_ref[0:n_lat, 0:LANES] = vl_ref[...]
            for c in range(n_lat_k):
                rows = slice(c * tk, (c + 1) * tk)
                kr_ref[rows, :] = _rope_rows(kl_ref[rows, :].astype(F32), cos_ref[rows, :], sa_ref[rows, :],
                                             sb_ref[rows, :]).astype(BF16)
        vx_ref[n_lat:, 0:LANES] = vc_ref[...]
        vx_ref[:, LANES:2 * LANES] = jnp.ones((vx_ref.shape[0], LANES), BF16)

    qf = q_ref[...].astype(F32)
    if n_lat_k:
        qrows = pl.ds(pl.multiple_of(pl.program_id(2) * tq, tq), tq)
        qf = _rope_rows(qf, cos_ref[qrows, :], sa_ref[qrows, :], sb_ref[qrows, :])
    q = (qf * scale).astype(BF16)
    lane = lax.broadcasted_iota(jnp.int32, q.shape, 1)
    zero = jnp.zeros_like(q)
    qs_ref[0:tq, :] = jnp.where(lane < DA_HEAD_DIM, q, zero)
    qs_ref[tq:2 * tq, :] = jnp.where(lane >= DA_HEAD_DIM, q, zero)
    m_ref[...] = jnp.full(m_ref.shape, -1e30, F32)
    acc_ref[...] = jnp.zeros(acc_ref.shape, F32)

    def key_rows(t, nk):
        return pl.ds(t * tk if isinstance(t, int) else pl.multiple_of(t * tk, tk), nk)

    def keys(t):
        if isinstance(t, int) and t == n_lat_k:
            return kc_ref[...], n_ctx
        return kr_ref[key_rows(t, tk), :], tk

    def width(t):
        return n_ctx if isinstance(t, int) and t == n_lat_k else tk

    def stage(t_next, t_cur, t_prev, buf):
        other = 1 - buf
        if t_next is not None:
            k_next, nk_next = keys(t_next)
        if t_prev is not None:
            nk_prev = width(t_prev)
            vx = vx_ref[key_rows(t_prev, nk_prev), :]
        rows_c = 2 * tq // ATTN_STAGE_CHUNKS
        for c in range(ATTN_STAGE_CHUNKS):
            rc = slice(c * rows_c, (c + 1) * rows_c)
            if t_next is not None:
                s_ref[other, rc, 0:nk_next] = lax.dot_general(qs_ref[rc, :], k_next, (((1,), (1,)), ((), ())),
                                                              preferred_element_type=F32)
            if t_cur is not None:
                nk = width(t_cur)
                reps = nk // LANES
                for r in range(c * rows_c // rb, (c + 1) * rows_c // rb):
                    rows = slice(r * rb, (r + 1) * rb)
                    m_prev = m_ref[rows, :]
                    m_new = jnp.maximum(m_prev, jnp.max(s_ref[buf, rows, 0:nk], axis=-1, keepdims=True))
                    a_ref[buf, rows, :] = jnp.exp(m_prev - m_new)
                    m_ref[rows, :] = m_new
                    p = jnp.exp(s_ref[buf, rows, 0:nk] - jnp.concatenate([m_new] * reps, axis=1))
                    p_ref[buf, rows, 0:nk] = p.astype(BF16)
            if t_prev is not None:
                alpha = a_ref[other, rc, :]
                pv = jnp.dot(p_ref[other, rc, 0:nk_prev], vx, preferred_element_type=F32)
                acc_ref[rc, :] = jnp.concatenate([alpha, alpha], axis=1) * acc_ref[rc, :] + pv

    last = n_lat_k
    loop_pairs = (n_lat_k - 2) // 2 if (n_lat_k >= 4 and n_lat_k % 2 == 0) else 0
    stage(0, None, None, 1)
    t = 0
    while t <= last:
        if t == 1 and loop_pairs:
            def body(i, carry):
                stage(2 + 2 * i, 1 + 2 * i, 2 * i, 1)
                stage(3 + 2 * i, 2 + 2 * i, 1 + 2 * i, 0)
                return carry
            lax.fori_loop(0, loop_pairs, body, 0)
            t += 2 * loop_pairs
            continue
        stage(t + 1 if t < last else None, t, t - 1 if t >= 1 else None, t % 2)
        t += 1
    stage(None, None, last, (last + 1) % 2)

    lf = lam_ref[...]
    lam = (jnp.exp(jnp.sum(lf[0:1] * lf[1:2], axis=-1, keepdims=True))
           - jnp.exp(jnp.sum(lf[2:3] * lf[3:4], axis=-1, keepdims=True)) + lam_init)
    o = acc_ref[:, 0:LANES] / acc_ref[:, LANES:2 * LANES]
    out = o[:tq] - lam * o[tq:]
    y = _rms(out) * subln_ref[...] * (1.0 - lam_init)
    o_ref[...] = y.astype(BF16)


def _attn_call(p, tables, da_lambda, da_subln, lam_init, dims, *, tq, q_row0, n_q, rows, with_lat):
    batch, seq, ctx_len, nl = dims["batch"], dims["seq"], dims["ctx"], dims["nl"]
    tk = 512
    k_col, v_col = DA_WIDTH // LANES, V_OFF // LANES
    q_map = lambda b, h, i: (q_row0 // tq + b * n_q + i, h)
    o_map = lambda b, h, i: (b * n_q + i, h)
    kv_lat = lambda off: pl.BlockSpec((seq, LANES), lambda b, h, i: (b, off + h))
    kv_ctx = lambda off: pl.BlockSpec((ctx_len, LANES), lambda b, h, i: (nl // ctx_len + b, off + h))
    in_specs = [pl.BlockSpec((4, DA_HEAD_DIM), lambda b, h, i: (0, 0)),
                pl.BlockSpec((1, DA_V_DIM), lambda b, h, i: (0, 0)),
                pl.BlockSpec((tq, LANES), q_map),
                kv_ctx(k_col), kv_ctx(v_col)]
    args = [da_lambda, da_subln.reshape(1, DA_V_DIM), p, p, p]
    n_keys = ctx_len + (seq if with_lat else 0)
    scratch = [pltpu.VMEM((2 * tq, LANES), BF16), pltpu.VMEM((2, 2 * tq, tk), F32), pltpu.VMEM((2, 2 * tq, tk), BF16),
               pltpu.VMEM((2 * tq, LANES), F32), pltpu.VMEM((2, 2 * tq, LANES), F32),
               pltpu.VMEM((2 * tq, 2 * LANES), F32), pltpu.VMEM((n_keys, 2 * LANES), BF16)]
    if with_lat:
        tab = pl.BlockSpec((seq, LANES), lambda b, h, i: (0, 0))
        in_specs += [kv_lat(k_col), kv_lat(v_col), tab, tab, tab]
        args += [p, p, *tables]
        scratch = [pltpu.VMEM((seq, LANES), BF16)] + scratch
    kern = functools.partial(_attn_kernel, tq=tq, tk=tk, n_lat_k=seq // tk if with_lat else 0, lam_init=lam_init)
    return pl.pallas_call(
        kern,
        out_shape=jax.ShapeDtypeStruct((rows, DA_WIDTH), BF16),
        grid=(batch, DA_HEADS, n_q),
        in_specs=in_specs,
        out_specs=pl.BlockSpec((tq, LANES), o_map),
        scratch_shapes=scratch,
        compiler_params=_cparams("parallel", "parallel", "arbitrary"),
        name="diff_attn" if with_lat else "diff_attn_ctx",
    )(*args)


def _attention(p, tables, da_lambda, da_subln, lam_init, with_ctx, dims):
    batch, seq, ctx_len, nl = dims["batch"], dims["seq"], dims["ctx"], dims["nl"]
    tq = _pick_tile(seq, (1024, 512, TOK_TILE))
    y = _attn_call(p, tables, da_lambda, da_subln, lam_init, dims, tq=tq, q_row0=0, n_q=seq // tq, rows=nl,
                   with_lat=True)
    if with_ctx:
        tqc = TOK_TILE
        y_ctx = _attn_call(p, tables, da_lambda, da_subln, lam_init, dims, tq=tqc, q_row0=nl, n_q=ctx_len // tqc,
                           rows=batch * ctx_len, with_lat=False)
        y = jnp.concatenate([y, y_ctx], axis=0)
    return y


def _softplus(x):
    return jnp.maximum(x, 0.0) + jnp.log(1.0 + jnp.exp(-jnp.abs(x)))


def _dt_lanes(h, w, bias, alog):
    raw = lax.dot_general(h, w.astype(BF16), (((1,), (1,)), ((), ())), preferred_element_type=F32)
    n = 2 * SSD_HEADS
    dt = _softplus(raw + bias)
    lane = lax.broadcasted_iota(jnp.int32, dt.shape, 1)
    dta = pltpu.roll(dt, n, 1) * (-jnp.exp(alog))
    return jnp.where(lane < n, dt, jnp.where(lane < 2 * n, dta, 0.0))


def _conv_kernel(prev_ref, x_ref, next_ref, w_ref, b_ref, h_ref, wdt_ref, bdt_ref, alog_ref, o_ref, dd_ref,
                 *, tm, n_lat_tiles, lat_per_seq, ctx_per_seq):
    @pl.when(pl.program_id(1) == 0)
    def _():
        dd_ref[...] = _dt_lanes(h_ref[...], wdt_ref[...], bdt_ref[...], alog_ref[...])

    i = pl.program_id(0)
    is_lat = i < n_lat_tiles
    pos = jnp.where(is_lat, i % lat_per_seq, (i - n_lat_tiles) % ctx_per_seq)
    per = jnp.where(is_lat, lat_per_seq, ctx_per_seq)
    keep_prev = (pos > 0).astype(F32)
    keep_next = (pos < per - 1).astype(F32)
    ext = jnp.concatenate([prev_ref[...].astype(F32) * keep_prev, x_ref[...].astype(F32),
                           next_ref[...].astype(F32) * keep_next], axis=0)
    n = tm + 2 * SUBLANES
    w = w_ref[...]
    acc = jnp.zeros((tm, ext.shape[1]), F32) + b_ref[...]
    for k in range(SSD_CONV):
        shift = (SSD_CONV // 2 - k) % n
        rolled = ext if shift == 0 else pltpu.roll(ext, shift, 0)
        acc = acc + rolled[SUBLANES:SUBLANES + tm] * w[k:k + 1, :]
    o_ref[...] = _silu(acc).astype(BF16)


def _ssd_prep(p, conv_w, conv_b, h, w_in_t, layer, bias2, alog2, dims):
    t = p.shape[0]
    d = h.shape[1]
    tm = TOK_TILE
    cb = 512
    col0 = XBC_OFF // cb
    r8 = tm // SUBLANES
    last8 = t // SUBLANES - 1
    assert REF_DT_OFF % LANES == 0
    kern = functools.partial(_conv_kernel, tm=tm, n_lat_tiles=dims["nl"] // tm, lat_per_seq=dims["seq"] // tm,
                             ctx_per_seq=dims["ctx"] // tm)
    return pl.pallas_call(
        kern,
        out_shape=(jax.ShapeDtypeStruct((t, SSD_XBC), BF16), jax.ShapeDtypeStruct((t, LANES), F32)),
        grid=(t // tm, SSD_XBC // cb),
        in_specs=[pl.BlockSpec((SUBLANES, cb), lambda i, j: (jnp.maximum(i * r8 - 1, 0), col0 + j)),
                  pl.BlockSpec((tm, cb), lambda i, j: (i, col0 + j)),
                  pl.BlockSpec((SUBLANES, cb), lambda i, j: (jnp.minimum((i + 1) * r8, last8), col0 + j)),
                  pl.BlockSpec((SSD_CONV, cb), lambda i, j: (0, j)),
                  pl.BlockSpec((1, cb), lambda i, j: (0, j)),
                  pl.BlockSpec((tm, d), lambda i, j: (i, 0)),
                  pl.BlockSpec((None, LANES, d), lambda i, j: (layer, REF_DT_OFF // LANES, 0)),
                  pl.BlockSpec((1, LANES), lambda i, j: (0, 0)),
                  pl.BlockSpec((1, LANES), lambda i, j: (0, 0))],
        out_specs=(pl.BlockSpec((tm, cb), lambda i, j: (i, j)), pl.BlockSpec((tm, LANES), lambda i, j: (i, 0))),
        compiler_params=_cparams("parallel", "arbitrary"),
        name="ssd_prep",
    )(p, p, p, conv_w, conv_b.reshape(1, SSD_XBC), h, w_in_t, bias2, alog2)


def _split3(x):
    hi = x.astype(BF16)
    rest = x - hi.astype(F32)
    mid = rest.astype(BF16)
    return hi, mid, (rest - mid.astype(F32)).astype(BF16)


def _ssd_scan_kernel(*refs, direction, final, ncc, write_ctx):
    if final:
        xbc_ref, dd_ref, yf_ref, z_ref, dskip_ref, norm_ref, o_ref, h_ref = refs
    else:
        xbc_ref, dd_ref, o_ref, h_ref = refs
    j = pl.program_id(1)
    q = SSD_CHUNK
    gw = SSD_HPG * SSD_HEAD_DIM

    @pl.when(j == 0)
    def _():
        h_ref[...] = jnp.zeros_like(h_ref)

    xbc = xbc_ref[...]
    xs = xbc[:, :SSD_WIDTH].astype(F32)
    bm = xbc[:, SSD_WIDTH:SSD_WIDTH + SSD_GROUPS * SSD_STATE]
    cm = xbc[:, SSD_WIDTH + SSD_GROUPS * SSD_STATE:]
    dd = dd_ref[...]

    ii =lax.broadcasted_iota(jnp.int32, (q, q), 0)
    jj = lax.broadcasted_iota(jnp.int32, (q, q), 1)
    if direction == 0:
        mask = jj <= ii
        last = q - 1
    else:
        mask = jj >= ii
        last = 0
    tri = jnp.where(mask, 1.0, 0.0).astype(BF16)
    ac = sum(jnp.dot(tri, piece, preferred_element_type=F32) for piece in _split3(dd))
    ac_t = ac.T

    er = lax.broadcasted_iota(jnp.int32, (LANES, SSD_WIDTH), 0)
    ec = lax.broadcasted_iota(jnp.int32, (LANES, SSD_WIDTH), 1) // SSD_HEAD_DIM
    dt_col = direction * SSD_HEADS
    ac_col = 2 * SSD_HEADS + direction * SSD_HEADS
    e_dt = jnp.where(er == ec + dt_col, 1.0, 0.0).astype(BF16)
    e_ac = jnp.where(er == ec + ac_col, 1.0, 0.0).astype(BF16)
    dt_exp = sum(jnp.dot(piece, e_dt, preferred_element_type=F32) for piece in _split3(dd))
    ac_exp = sum(jnp.dot(piece, e_ac, preferred_element_type=F32) for piece in _split3(ac))
    ac_last = ac_exp[last:last + 1, :]
    eac = jnp.exp(ac_exp)
    dec_end = jnp.exp(ac_last - ac_exp)
    chunk_dec = jnp.exp(ac_last)
    xdt = xs * dt_exp
    xdt_b = xdt.astype(BF16)
    xde_b = (xdt * dec_end).astype(BF16)

    lane = lax.broadcasted_iota(jnp.int32, (q, LANES), 1)
    left = lane < SSD_HEAD_DIM
    zero_b = jnp.zeros((q, LANES), BF16)
    pieces = []
    for g in range(SSD_GROUPS):
        bg = bm[:, g * SSD_STATE:(g + 1) * SSD_STATE]
        cg = cm[:, g * SSD_STATE:(g + 1) * SSD_STATE]
        cb = lax.dot_general(cg, bg, (((1,), (1,)), ((), ())), preferred_element_type=F32)
        h_t = h_ref[g]
        y_off = jnp.dot(cg, h_t.astype(BF16), preferred_element_type=F32) * eac[:, g * gw:(g + 1) * gw]
        bg_t = bg.astype(F32).T.astype(BF16)
        s_t = jnp.dot(bg_t, xde_b[:, g * gw:(g + 1) * gw], preferred_element_type=F32)
        h_ref[g] = h_t * chunk_dec[:, g * gw:(g + 1) * gw] + s_t
        for kp in range(SSD_HPG // 2):
            k0 = g * SSD_HPG + 2 * kp
            ms = []
            for k in (k0, k0 + 1):
                c = ac_col + k
                seg = ac[:, c:c + 1] - ac_t[c:c + 1, :]
                ms.append((cb * jnp.exp(jnp.where(mask, seg, -1e30))).astype(BF16))
            xp = xdt_b[:, k0 * SSD_HEAD_DIM:k0 * SSD_HEAD_DIM + LANES]
            y_diag = jnp.dot(jnp.concatenate(ms, axis=1),
                             jnp.concatenate([jnp.where(left, xp, zero_b), jnp.where(left, zero_b, xp)], axis=0),
                             preferred_element_type=F32)
            pieces.append(y_diag + y_off[:, kp * LANES:(kp + 1) * LANES])
    y = jnp.concatenate(pieces, axis=1)

    def emit():
        if final:
            yt = y + yf_ref[...] + dskip_ref[...] * xs
            yt = yt * _silu(z_ref[...].astype(F32))
            o_ref[...] = (_rms(yt) * norm_ref[...]).astype(BF16)
        else:
            o_ref[...] = y

    if write_ctx:
        emit()
    else:
        pl.when(j >= ncc)(emit)


def _ssd_scan(xbc, dd, direction, dims, with_ctx, final_args=None):
    batch, seq, ctx_len, nl = dims["batch"], dims["seq"], dims["ctx"], dims["nl"]
    q = SSD_CHUNK
    ncl, ncc = seq // q, ctx_len // q
    final = final_args is not None
    rows = nl + (batch * ctx_len if with_ctx else 0)

    def chunk(b, j):
        jc = j if direction == 0 else ncc - 1 - j
        jl = (j - ncc) if direction == 0 else ncl - 1 - (j - ncc)
        return jnp.where(j < ncc, nl // q + b * ncc + jc, b * ncl + jl)

    def out_chunk(b, j):
        if with_ctx:
            return chunk(b, j)
        return chunk(b, jnp.maximum(j, ncc))

    in_specs = [pl.BlockSpec((q, SSD_XBC), lambda b, j: (chunk(b, j), 0)),
                pl.BlockSpec((q, LANES), lambda b, j: (chunk(b, j), 0))]
    args = [xbc, dd]
    if final:
        yf, p, dskip, norm = final_args
        in_specs += [pl.BlockSpec((q, SSD_WIDTH), lambda b, j: (out_chunk(b, j), 0)),
                     pl.BlockSpec((q, SSD_WIDTH), lambda b, j: (chunk(b, j), Z_OFF // SSD_WIDTH)),
                     pl.BlockSpec((1, SSD_WIDTH), lambda b, j: (0, 0)),
                     pl.BlockSpec((1, SSD_WIDTH), lambda b, j: (0, 0))]
        args += [yf, p, dskip, norm]
    return pl.pallas_call(
        functools.partial(_ssd_scan_kernel, direction=direction, final=final, ncc=ncc, write_ctx=with_ctx),
        out_shape=jax.ShapeDtypeStruct((rows, SSD_WIDTH), BF16 if final else F32),
        grid=(batch, ncc + ncl),
        in_specs=in_specs,
        out_specs=pl.BlockSpec((q, SSD_WIDTH), lambda b, j: (out_chunk(b, j), 0)),
        scratch_shapes=[pltpu.VMEM((SSD_GROUPS, SSD_STATE, SSD_HPG * SSD_HEAD_DIM), F32)],
        compiler_params=_cparams("parallel", "arbitrary"),
        name="ssd_scan_bwd" if direction else "ssd_scan_fwd",
    )(*args)


S5_BLOCK_GROUPS = LANES // S5_GROUP
S5_BLOCKS = S5_GROUPS // S5_BLOCK_GROUPS
S5_BS = S5_BLOCK_GROUPS * S5_STATE
S5_LAG_ROWS = 24


def _expm1(x):
    poly = 1.0 + x / 10.0
    for n in range(9, 1, -1):
        poly = 1.0 + (x / n) * poly
    return jnp.where(jnp.abs(x) < 0.35, x * poly, jnp.exp(x) - 1.0)


def _s5_params(lam_re, lam_im, log_step, b_re, b_im, c_re, c_im, s5_d):
    nb, gl = S5_BLOCKS, S5_BLOCK_GROUPS
    rows = [lam_re[0], lam_im[0], lam_re[1], lam_im[1],
            jnp.repeat(log_step[0], S5_STATE), jnp.repeat(log_step[1], S5_STATE)]
    rows = [r.reshape(nb, S5_BS) for r in rows] + [jnp.zeros((nb, S5_BS), F32)] * 2
    lam_rows = jnp.stack(rows, axis=1)
    eye = jnp.eye(gl, dtype=F32)

    def bd_in(b):
        return jnp.einsum("jgpe,gh->jgehp", b.reshape(nb, gl, S5_STATE, S5_GROUP), eye).reshape(nb, LANES, S5_BS)

    def bd_out(c):
        return jnp.einsum("jgfp,gh->jgfhp", c.reshape(nb, gl, S5_GROUP, S5_STATE), eye).reshape(nb, LANES, S5_BS)

    b_bd = jnp.stack([bd_in(b_re), bd_in(b_im)], axis=1)
    ct_bd = jnp.stack([bd_out(c_re), bd_out(c_im)], axis=1)
    return lam_rows, b_bd, ct_bd, s5_d.reshape(nb, 1, LANES)


def _gelu_tanh(x):
    return 0.5 * x * (1.0 + jnp.tanh(math.sqrt(2.0 / math.pi) * (x + 0.044715 * x * x * x)))


def _s5_kernel(u_ref, lam_ref, b_ref, ct_ref, d_ref, o_ref,
               x_ref, w_ref, m_ref, s_ref, y_ref, taps_ref, pw_ref, bb_ref, *, batch, ncl, ncc):
    tc = S5_CHUNK
    nch = x_ref.shape[0]
    sw = S5_BS
    nt = (((1,), (1,)), ((), ()))

    for s in range(tc):
        x_ref[:, s * LANES:(s + 1) * LANES] = u_ref[pl.ds(s, nch, stride=tc), :].astype(BF16)

    rows = lam_ref[0]
    kk = lax.broadcasted_iota(jnp.int32, (S5_LAG_ROWS, sw), 0).astype(F32)
    b_re, b_im = b_ref[0, 0], b_ref[0, 1]
    for d in range(2):
        l_re, l_im = rows[2 * d:2 * d + 1], rows[2 * d + 1:2 * d + 2]
        delta = jnp.exp(rows[4 + d:5 + d])
        lr, li = l_re * delta, l_im * delta
        mag = jnp.exp(kk * lr)
        pw_ref[2 * d] = mag * jnp.cos(kk * li)
        pw_ref[2 * d + 1] = mag * jnp.sin(kk * li)
        xr = _expm1(lr) * jnp.cos(li) - 2.0 * jnp.sin(0.5 * li) ** 2
        xi = jnp.exp(lr) * jnp.sin(li)
        den = l_re * l_re + l_im * l_im
        co_re = (xr * l_re + xi * l_im) / den
        co_im = (xi * l_re - xr * l_im) / den
        bb_ref[2 * d] = co_re * b_re - co_im * b_im
        bb_ref[2 * d + 1] = co_re * b_im + co_im * b_re

    def power(d, lag):
        return pw_ref[2 * d, lag:lag + 1, :], pw_ref[2 * d + 1, lag:lag + 1, :]

    ct_re, ct_im = ct_ref[0, 0], ct_ref[0, 1]
    ct_re_b, ct_im_b = ct_re.astype(BF16), ct_im.astype(BF16)
    for d in range(2):
        tp = []
        for part in range(2):
            q = 2 * d + part
            for s in range(tc):
                pr, pi = power(d, tc - 1 - s if d == 0 else s)
                if part == 0:
                    tile = pr * bb_ref[2 * d] - pi * bb_ref[2 * d + 1]
                else:
                    tile = pr * bb_ref[2 * d + 1] + pi * bb_ref[2 * d]
                w_ref[q, s * LANES:(s + 1) * LANES, :] = tile.astype(BF16)
            w = w_ref[q]
            s_ref[:, q * sw:(q + 1) * sw] = jnp.dot(x_ref[...], w, preferred_element_type=F32)
            tp.append(lax.dot_general(w, ct_re_b if part == 0 else ct_im_b, nt, preferred_element_type=F32))
        taps_ref[d] = tp[0] - tp[1]

    ri = lax.broadcasted_iota(jnp.int32, (LANES, LANES), 0)
    ci = lax.broadcasted_iota(jnp.int32, (LANES, LANES), 1)
    skip = jnp.where(ri == ci, d_ref[0], 0.0)
    for s in range(tc):
        for t in range(tc):
            if t >= s:
                lag_tile = tc - 1 - (t - s)
                tile = taps_ref[0, lag_tile * LANES:(lag_tile + 1) * LANES, :]
                if t == s:
                    tile = tile + taps_ref[1, 0:LANES, :] + skip
            else:
                tile = taps_ref[1, (s - t) * LANES:(s - t + 1) * LANES, :]
            m_ref[s * LANES:(s + 1) * LANES, t * LANES:(t + 1) * LANES] = tile.astype(BF16)

    af_re, af_im = power(0, tc)
    ab_re, ab_im = power(1, tc)

    def advance(row, h, dir_off, ar, ai):
        h_re, h_im = h
        s_re = s_ref[pl.ds(row, 1), dir_off:dir_off + sw]
        s_im = s_ref[pl.ds(row, 1), dir_off + sw:dir_off + 2 * sw]
        s_ref[pl.ds(row, 1), dir_off:dir_off + sw] = h_re
        s_ref[pl.ds(row, 1), dir_off + sw:dir_off + 2 * sw] = h_im
        return ar * h_re - ai * h_im + s_re, ar * h_im + ai * h_re + s_im

    def sweep(first_chunk, n):
        def body(i, c):
            return tuple((advance(first_chunk(b) + i, c[b][0], 0, af_re, af_im),
                          advance(first_chunk(b) + n - 1 - i, c[b][1], 2 * sw, ab_re, ab_im)) for b in range(batch))
        return body

    zero = jnp.zeros((1, sw), F32)
    c = lax.fori_loop(0, ncc, sweep(lambda b: batch * ncl + b * ncc, ncc), (((zero, zero), (zero, zero)),) * batch)
    lax.fori_loop(0, ncl, sweep(lambda b: b * ncl, ncl), c)

    y = jnp.dot(x_ref[...], m_ref[...], preferred_element_type=F32)
    for q in range(4):
        d, part = divmod(q, 2)
        for t in range(tc):
            pr, pi = power(d, t + 1 if d == 0 else tc - t)
            tile = ct_re * pr - ct_im * pi if part == 0 else -(ct_re * pi + ct_im * pr)
            w_ref[q, t * LANES:(t + 1) * LANES, :] = tile.astype(BF16)
        h = s_ref[:, q * sw:(q + 1) * sw].astype(BF16)
        y = y + lax.dot_general(h, w_ref[q], nt, preferred_element_type=F32)
    y_ref[...] = y
    for t in range(tc):
        o_ref[pl.ds(t, nch, stride=tc), :] = _gelu_tanh(y_ref[:, t * LANES:(t + 1) * LANES])


def _s5(u, params, dims):
    lam_rows, b_bd, ct_bd, d_skip = params
    t = u.shape[0]
    nch = t // S5_CHUNK
    ncl, ncc = dims["seq"] // S5_CHUNK, dims["ctx"] // S5_CHUNK
    width = S5_CHUNK * LANES
    pspec = pl.BlockSpec((1, 2, LANES, S5_BS), lambda i: (i, 0, 0, 0))
    return pl.pallas_call(
        functools.partial(_s5_kernel, batch=dims["batch"], ncl=ncl, ncc=ncc),
        out_shape=jax.ShapeDtypeStruct((t, S5_WIDTH), F32),
        grid=(S5_BLOCKS,),
        in_specs=[pl.BlockSpec((t, LANES), lambda i: (0, i)),
                  pl.BlockSpec((1, SUBLANES, S5_BS), lambda i: (i, 0, 0)), pspec, pspec,
                  pl.BlockSpec((1, 1, LANES), lambda i: (i, 0, 0))],
        out_specs=pl.BlockSpec((t, LANES), lambda i: (0, i)),
        scratch_shapes=[pltpu.VMEM((nch, width), BF16),
                        pltpu.VMEM((4, width, S5_BS), BF16),
                        pltpu.VMEM((width, width), BF16),
                        pltpu.VMEM((nch, 4 * S5_BS), F32),
                        pltpu.VMEM((nch, width), F32),
                        pltpu.VMEM((2, width, LANES), F32),
                        pltpu.VMEM((4, S5_LAG_ROWS, S5_BS), F32),
                        pltpu.VMEM((4, LANES, S5_BS), F32)],
        compiler_params=_cparams("parallel"),
        name="s5_scan",
    )(u, lam_rows, b_bd, ct_bd, d_skip)


def _glu_kernel(x_ref, wa_ref, wb_ref, ba_ref, bb_ref, o_ref, wab_ref, wbb_ref):
    @pl.when(pl.program_id(1) == 0)
    def _():
        wab_ref[...] = wa_ref[...].astype(BF16)
        wbb_ref[...] = wb_ref[...].astype(BF16)

    x = x_ref[...].astype(BF16)
    a = jnp.dot(x, wab_ref[...], preferred_element_type=F32) + ba_ref[...]
    b = jnp.dot(x, wbb_ref[...], preferred_element_type=F32) + bb_ref[...]
    o_ref[...] = (a * _sigmoid(b)).astype(BF16)


def _glu(x, w, bias, layer, rows, tm):
    k = x.shape[1]
    tn = W_TILE
    nb = S5_WIDTH // tn
    return pl.pallas_call(
        _glu_kernel,
        out_shape=jax.ShapeDtypeStruct((rows, S5_WIDTH), BF16),
        grid=(nb, rows // tm),
        in_specs=[pl.BlockSpec((tm, k), lambda j, i: (i, 0)),
                  pl.BlockSpec((None, k, tn), lambda j, i: (layer, 0, j)),
                  pl.BlockSpec((None, k, tn), lambda j, i: (layer, 0, nb + j)),
                  pl.BlockSpec((None, 1, tn), lambda j, i: (layer, 0, j)),
                  pl.BlockSpec((None, 1, tn), lambda j, i: (layer, 0, nb + j))],
        out_specs=pl.BlockSpec((tm, tn), lambda j, i: (i, j)),
        scratch_shapes=[pltpu.VMEM((k, tn), BF16), pltpu.VMEM((k, tn), BF16)],
        compiler_params=_cparams("parallel", "arbitrary"),
        name="s5_glu",
    )(x, w, w, bias, bias)


def _merge_kernel(ya_ref, yb_ref, yc_ref, wa_ref, wb_ref, wc_ref, ga_ref, gb_ref, gc_ref, o_ref, wbf_ref):
    @pl.when(pl.program_id(1) == 0)
    def _():
        for n, w_ref in enumerate((wa_ref, wb_ref, wc_ref)):
            wbf_ref[n] = w_ref[0].astype(BF16)

    acc = None
    for n, (y_ref, g_ref) in enumerate(((ya_ref, ga_ref), (yb_ref, gb_ref), (yc_ref, gc_ref))):
        br = jnp.dot(y_ref[...], wbf_ref[n], preferred_element_type=F32)
        term = _sigmoid(g_ref[...].astype(F32)) * br
        acc = term if acc is None else acc + term
    o_ref[...] = acc.astype(BF16)


def _merge(ya, yb, yc, w_branch, layer, gates, rows, tm):
    k = ya.shape[1]
    d = w_branch.shape[3]
    tn = W_TILE
    gstep = d // tn
    yspec = pl.BlockSpec((tm, k), lambda j, i: (i, 0))
    wspec = lambda n: pl.BlockSpec((None, 1, k, tn), lambda j, i: (layer, n, 0, j))
    gspec = lambda n: pl.BlockSpec((tm, tn), lambda j, i: (i, n * gstep + j))
    return pl.pallas_call(
        _merge_kernel,
        out_shape=jax.ShapeDtypeStruct((rows, d), BF16),
        grid=(d // tn, rows // tm),
        in_specs=[yspec, yspec, yspec, wspec(0), wspec(1), wspec(2), gspec(0), gspec(1), gspec(2)],
        out_specs=pl.BlockSpec((tm, tn), lambda j, i: (i, j)),
        scratch_shapes=[pltpu.VMEM((N_BRANCH, k, tn), BF16)],
        compiler_params=_cparams("parallel", "arbitrary"),
        name="branch_merge",
    )(ya, yb, yc, w_branch, w_branch, w_branch, gates, gates, gates)


def _out_proj_kernel(*refs, n_lat):
    g_ref, w_ref, *x_refs, mod_ref, npost_ref, npre_ref, xo_ref, ho_ref = refs
    o = jnp.dot(g_ref[...], w_ref[...], preferred_element_type=F32)
    gate = mod_ref[0, 2:3, :]
    xn = _stream_tile(x_refs, n_lat) + gate * (_rms(o) * npost_ref[...])
    xo_ref[...] = xn
    shift = mod_ref[0, 3:4, :]
    scale = mod_ref[0, 4:5, :]
    ho_ref[...] = (_rms(xn) * npre_ref[...] * (1.0 + scale) + shift).astype(BF16)


def _out_proj(g, w_out, xs, mod, npost, npre, dims):
    r, d = g.shape
    tm = TOK_TILE * (2 if len(xs) == 1 else 1)
    seg = functools.partial(_seg_of_tile, tm=tm, n_lat=dims["nl"], seq=dims["seq"], batch=dims["batch"])
    row = pl.BlockSpec((tm, d), lambda i: (i, 0))
    vec = pl.BlockSpec((1, d), lambda i: (0, 0))
    return pl.pallas_call(
        functools.partial(_out_proj_kernel, n_lat=dims["nl"] // tm),
        out_shape=(jax.ShapeDtypeStruct((r, d), F32), jax.ShapeDtypeStruct((r, d), BF16)),
        grid=(r // tm,),
        in_specs=[row, pl.BlockSpec((d, d), lambda i: (0, 0))]
        + _stream_specs(xs, tm, d, dims["nl"])
        + [pl.BlockSpec((1, 6, d), lambda i: (seg(i), 0, 0)), vec, vec],
        out_specs=(row, row),
        compiler_params=_cparams("parallel"),
        name="out_proj",
    )(g, w_out, *xs, mod, npost.reshape(1, d), npre.reshape(1, d))


def _ffn_up_kernel(h_ref, wg_ref, wu_ref, o_ref, wgb_ref, wub_ref):
    @pl.when(pl.program_id(1) == 0)
    def _():
        wgb_ref[...] = wg_ref[...].astype(BF16)
        wub_ref[...] = wu_ref[...].astype(BF16)

    h = h_ref[...]
    a = jnp.dot(h, wgb_ref[...], preferred_element_type=F32)
    b = jnp.dot(h, wub_ref[...], preferred_element_type=F32)
    o_ref[...] = (_silu(a) * b).astype(BF16)


def _ffn_up(h, wg, wu, layer, tm):
    r, d = h.shape
    f = wg.shape[2]
    tn = W_TILE
    wspec = pl.BlockSpec((None, d, tn), lambda j, i: (layer, 0, j))
    return pl.pallas_call(
        _ffn_up_kernel,
        out_shape=jax.ShapeDtypeStruct((r, f), BF16),
        grid=(f // tn, r // tm),
        in_specs=[pl.BlockSpec((tm, d), lambda j, i: (i, 0)), wspec, wspec],
        out_specs=pl.BlockSpec((tm, tn), lambda j, i: (i, j)),
        scratch_shapes=[pltpu.VMEM((d, tn), BF16), pltpu.VMEM((d, tn), BF16)],
        compiler_params=_cparams("parallel", "arbitrary"),
        name="ffn_up",
    )(h, wg, wu)


def _ffn_down_kernel(a_ref, w_ref, x_ref, mod_ref, npost_ref, o_ref, acc_ref):
    k = pl.program_id(1)

    @pl.when(k == 0)
    def _():
        acc_ref[...] = jnp.zeros_like(acc_ref)

    acc_ref[...] += jnp.dot(a_ref[...], w_ref[...], preferred_element_type=F32)

    @pl.when(k == pl.num_programs(1) - 1)
    def _():
        gate = mod_ref[0, 5:6, :]
        o_ref[...] = x_ref[...] + gate * (_rms(acc_ref[...]) * npost_ref[...])


def _ffn_down(act, wd, x, mod, npost, dims):
    r, f = act.shape
    d = wd.shape[1]
    big = r == dims["nl"] and dims["seq"] % 1024 == 0 and f % (8 * LANES) == 0
    tm, nk = (1024, 8) if big else (2 * TOK_TILE, 4)
    tk = f // nk
    seg = functools.partial(_seg_of_tile, tm=tm, n_lat=dims["nl"], seq=dims["seq"], batch=dims["batch"])
    row = pl.BlockSpec((tm, d), lambda i, k: (i, 0))
    return pl.pallas_call(
        _ffn_down_kernel,
        out_shape=jax.ShapeDtypeStruct((r, d), F32),
        grid=(r // tm, nk),
        in_specs=[pl.BlockSpec((tm, tk), lambda i, k: (i, k)),
                  pl.BlockSpec((tk, d), lambda i, k: (k, 0)),
                  row,
                  pl.BlockSpec((1, 6, d), lambda i, k: (seg(i), 0, 0)),
                  pl.BlockSpec((1, d), lambda i, k: (0, 0))],
        out_specs=row,
        scratch_shapes=[pltpu.VMEM((tm, d), F32)],
        compiler_params=_cparams("parallel", "arbitrary"),
        name="ffn_down",
    )(act, wd, x, mod, npost.reshape(1, d))


def _pick_tile(rows, cands):
    for c in cands:
        if rows % c == 0:
            return c
    raise ValueError(f"no tile in {cands} divides {rows}")


def kernel(x, c, ctx, c_ctx, ada_w, ada_b, norm_mix_pre, norm_mix_post, norm_ffn_pre, norm_ffn_post, w_in, da_lambda, da_subln, ssd_conv_w, ssd_conv_b, ssd_dt_bias, ssd_a_log, ssd_d, ssd_norm, s5_lam_re, s5_lam_im, s5_log_step, s5_b_re, s5_b_im, s5_c_re, s5_c_im, s5_d, s5_glu_w, s5_glu_b, w_branch, w_out, ffn_w_gate, ffn_w_up, ffn_w_down):
    batch, seq, d = x.shape
    ctx_len = ctx.shape[1]
    depth = ada_w.shape[0]
    nl, nc = batch * seq, batch * ctx_len
    dims = dict(batch=batch, seq=seq, ctx=ctx_len, nl=nl, nc=nc)
    assert batch < MOD_ROWS and seq % TOK_TILE == 0 and ctx_len % TOK_TILE == 0 and seq % ctx_len == 0
    assert nc % (2 * TOK_TILE) == 0 and seq % GRID_W == 0

    xs = (x.reshape(nl, d), ctx.reshape(nc, d))
    cc =jnp.concatenate([c, c_ctx[None], jnp.zeros((MOD_ROWS - batch - 1, d), F32)], axis=0)
    mod_all = _ada(cc, ada_w, ada_b).reshape(depth, MOD_ROWS, 6, d)
    rope_tabs = _rope_tables(seq, ctx_len)
    n_dt = 2 * SSD_HEADS
    pad = lambda v, before: jnp.concatenate(
        [jnp.zeros((1, before), F32), v.reshape(1, n_dt), jnp.zeros((1, LANES - n_dt - before), F32)], axis=1)
    tm_all = _pick_tile(nl + nc, (2176, 1088, 512, 256))
    w_in_t = jnp.swapaxes(w_in, 1, 2)

    for l in range(depth):
        last = l == depth - 1
        with_ctx = not last
        lam_init = 0.8 - 0.6 * math.exp(-0.3 * l)
        mod = mod_all[l]
        rows = nl + nc if with_ctx else nl
        tm = _pick_tile(rows, (1088, 1024, 512))

        h = _norm_mod(xs, norm_mix_pre[l], mod, 0, dims)
        p = _proj(h, w_in_t, l, 0, REF_DT_OFF, BF16, tm_all, "in_proj_main")
        u = _proj(h, w_in_t, l, REF_U_OFF, S5_WIDTH, F32, tm_all, "in_proj_u")
        gates = _proj(h, w_in_t, l, REF_U_OFF + S5_WIDTH, N_BRANCH * d, BF16, tm_all, "in_proj_gates")

        y_attn = _attention(p, rope_tabs, da_lambda[l], da_subln[l], lam_init, with_ctx, dims)

        xbc, dd = _ssd_prep(p, ssd_conv_w[l], ssd_conv_b[l], h, w_in_t, l,
                            pad(ssd_dt_bias[l], 0), pad(ssd_a_log[l], n_dt), dims)
        y_f = _ssd_scan(xbc, dd, 0, dims, with_ctx)
        dskip = jnp.repeat(ssd_d[l], SSD_HEAD_DIM).reshape(1, SSD_WIDTH)
        y_ssd = _ssd_scan(xbc, dd, 1, dims, with_ctx,
                          final_args=(y_f, p, dskip, ssd_norm[l].reshape(1, SSD_WIDTH)))

        s5p = _s5_params(s5_lam_re[l], s5_lam_im[l], s5_log_step[l], s5_b_re[l], s5_b_im[l],
                         s5_c_re[l], s5_c_im[l], s5_d[l])
        yg = _s5(u, s5p, dims)
        y_s5 = _glu(yg, s5_glu_w, s5_glu_b.reshape(depth, 1, 2 * S5_WIDTH), l, rows, tm)

        g = _merge(y_attn, y_ssd, y_s5, w_branch, l, gates, rows, tm)
        xt, h2 = _out_proj(g, _cast_bf16(w_out, l), xs, mod, norm_mix_post[l], norm_ffn_pre[l], dims)
        act = _ffn_up(h2, ffn_w_gate, ffn_w_up, l, tm)
        xt = _ffn_down(act, _cast_bf16(ffn_w_down, l), xt, mod, norm_ffn_post[l], dims)
        xs = (xt,)

    return xt[:nl].reshape(batch, seq, d)
```

```python
import functools
import math

import jax
import jax.numpy as jnp
from jax import lax
from jax.experimental import pallas as pl
from jax.experimental.pallas import tpu as pltpu

F32 = jnp.float32
BF16 = jnp.bfloat16
HIGHEST = lax.Precision.HIGHEST

GRID_W = 64
N_BRANCH = 3
DA_HEADS = 8
DA_HEAD_DIM = 64
DA_V_DIM = 2 * DA_HEAD_DIM
DA_WIDTH = DA_HEADS * DA_V_DIM
ROPE_THETA = 10000.0
SSD_HEADS = 16
SSD_HEAD_DIM = 64
SSD_GROUPS = 2
SSD_HPG = SSD_HEADS // SSD_GROUPS
SSD_STATE = 128
SSD_WIDTH = SSD_HEADS * SSD_HEAD_DIM
SSD_XBC = SSD_WIDTH + 2 * SSD_GROUPS * SSD_STATE
SSD_CONV = 5
SSD_CHUNK = 128
S5_GROUP = 16
S5_GROUPS = 64
S5_WIDTH = S5_GROUPS * S5_GROUP
S5_STATE = 64
S5_CHUNK = 16
RMS_EPS = 1e-6

V_OFF = 2 * DA_WIDTH
Z_OFF = 3 * DA_WIDTH
XBC_OFF = 4 * DA_WIDTH
REF_DT_OFF = XBC_OFF + SSD_XBC
REF_U_OFF = REF_DT_OFF + 2 * SSD_HEADS

LANES = 128
SUBLANES = 8
VMEM_LIMIT_BYTES = 52 * 1024 * 1024
MOD_ROWS = 8

TOK_TILE = 256


def _cparams(*sem):
    return pltpu.CompilerParams(dimension_semantics=sem, vmem_limit_bytes=VMEM_LIMIT_BYTES)


def _rms(x):
    return x * lax.rsqrt(jnp.mean(x * x, axis=-1, keepdims=True) + RMS_EPS)


def _sigmoid(x):
    return 1.0 / (1.0 + jnp.exp(-x))


def _silu(x):
    return x * _sigmoid(x)


def _seg_of_tile(i, tm, n_lat, seq, batch):
    return jnp.where(i < n_lat // tm, i // (seq // tm), batch)


def _ada_kernel(c_ref, w_ref, b_ref, o_ref):
    c = c_ref[...]
    o_ref[0] = jnp.dot(_silu(c), w_ref[0], precision=HIGHEST, preferred_element_type=F32) + b_ref[0]


def _ada(cc, ada_w, ada_b):
    depth, d, n = ada_w.shape
    tn = 2048
    return pl.pallas_call(
        _ada_kernel,
        out_shape=jax.ShapeDtypeStruct((depth, MOD_ROWS, n), F32),
        grid=(depth, n // tn),
        in_specs=[pl.BlockSpec((MOD_ROWS, d), lambda l, j: (0, 0)),
                  pl.BlockSpec((1, d, tn), lambda l, j: (l, 0, j)),
                  pl.BlockSpec((1, 1, tn), lambda l, j: (l, 0, j))],
        out_specs=pl.BlockSpec((1, MOD_ROWS, tn), lambda l, j: (l, 0, j)),
        compiler_params=_cparams("parallel", "parallel"),
        name="ada_mod",
    )(cc, ada_w, ada_b.reshape(depth, 1, n))


def _stream_specs(xs, tm, d, nl):
    n_lat = nl // tm
    if len(xs) == 1:
        return [pl.BlockSpec((tm, d), lambda i: (i, 0))]
    return [pl.BlockSpec((tm, d), lambda i: (jnp.minimum(i, n_lat - 1), 0)),
            pl.BlockSpec((tm, d), lambda i: (jnp.maximum(i - n_lat, 0), 0))]


def _stream_tile(x_refs, n_lat):
    if len(x_refs) == 1:
        return x_refs[0][...]
    return jnp.where(pl.program_id(0) < n_lat, x_refs[0][...], x_refs[1][...])


def _norm_mod_kernel(*refs, shift_idx, n_lat):
    *x_refs, g_ref, mod_ref, o_ref = refs
    y = _rms(_stream_tile(x_refs, n_lat)) * g_ref[...]
    shift = mod_ref[0, shift_idx:shift_idx + 1, :]
    scale = mod_ref[0, shift_idx + 1:shift_idx + 2, :]
    o_ref[...] = (y * (1.0 + scale) + shift).astype(BF16)


def _norm_mod(xs, g, mod, shift_idx, dims):
    d = xs[0].shape[1]
    t = dims["nl"] + dims["nc"]
    tm = 2 * TOK_TILE
    seg = functools.partial(_seg_of_tile, tm=tm, n_lat=dims["nl"], seq=dims["seq"], batch=dims["batch"])
    return pl.pallas_call(
        functools.partial(_norm_mod_kernel, shift_idx=shift_idx, n_lat=dims["nl"] // tm),
        out_shape=jax.ShapeDtypeStruct((t, d), BF16),
        grid=(t // tm,),
        in_specs=_stream_specs(xs, tm, d, dims["nl"]) + [
            pl.BlockSpec((1, d), lambda i: (0, 0)),
            pl.BlockSpec((1, 6, d), lambda i: (seg(i), 0, 0))],
        out_specs=pl.BlockSpec((tm, d), lambda i: (i, 0)),
        compiler_params=_cparams("parallel"),
        name="norm_mod",
    )(*xs, g.reshape(1, d), mod)


W_TILE = 512


def _shifted_rows(w0, w1, shift):
    return jnp.concatenate([w0[shift:], w1[:shift]], axis=0)


def _proj_kernel(*refs, shift):
    if shift:
        x_ref, w0_ref, w1_ref, o_ref, wb_ref = refs
    else:
        x_ref, w0_ref, o_ref, wb_ref = refs

    @pl.when(pl.program_id(1) == 0)
    def _():
        w = _shifted_rows(w0_ref[...], w1_ref[...], shift) if shift else w0_ref[...]
        wb_ref[...] = w.T.astype(BF16)

    o_ref[...] = jnp.dot(x_ref[...], wb_ref[...], preferred_element_type=F32).astype(o_ref.dtype)


def _proj(x, w_t, layer, col0, n_cols, out_dtype, tm, name):
    m, k = x.shape
    tn = W_TILE
    cb0, shift = col0 // tn, col0 % tn
    assert shift % SUBLANES == 0
    w_specs = [pl.BlockSpec((None, tn, k), lambda j, i: (layer, cb0 + j, 0))]
    if shift:
        w_specs.append(pl.BlockSpec((None, tn, k), lambda j, i: (layer, cb0 + j + 1, 0)))
    return pl.pallas_call(
        functools.partial(_proj_kernel, shift=shift),
        out_shape=jax.ShapeDtypeStruct((m, n_cols), out_dtype),
        grid=(n_cols // tn, m // tm),
        in_specs=[pl.BlockSpec((tm, k), lambda j, i: (i, 0))] + w_specs,
        out_specs=pl.BlockSpec((tm, tn), lambda j, i: (i, j)),
        scratch_shapes=[pltpu.VMEM((k, tn), BF16)],
        compiler_params=_cparams("parallel", "arbitrary"),
        name=name,
    )(x, *([w_t] * len(w_specs)))


def _cast_kernel(w_ref, o_ref):
    o_ref[...] = w_ref[...].astype(BF16)


def _cast_bf16(w, layer):
    _, r, n = w.shape
    tr = _pick_tile(r, (512, 256, 128, 8))
    return pl.pallas_call(
        _cast_kernel,
        out_shape=jax.ShapeDtypeStruct((r, n), BF16),
        grid=(r // tr,),
        in_specs=[pl.BlockSpec((None, tr, n), lambda i: (layer, i, 0))],
        out_specs=pl.BlockSpec((tr, n), lambda i: (i, 0)),
        compiler_params=_cparams("parallel"),
        name="cast_bf16",
    )(w)


def _rope_tables(seq, ctx_len):
    n_rows = seq // GRID_W
    row = jnp.repeat(jnp.arange(n_rows, dtype=F32), GRID_W)
    col = jnp.tile(jnp.arange(GRID_W, dtype=F32), n_rows)
    half = DA_HEAD_DIM // 2
    inv_freq = ROPE_THETA ** (-jnp.arange(0, half, 2, dtype=F32) / half)
    ar = row[:, None] * inv_freq[None, :]
    ac = col[:, None] * inv_freq[None, :]
    ang = jnp.concatenate([ar, ar, ac, ac], axis=-1)
    ang = jnp.concatenate([ang, jnp.zeros((ctx_len, DA_HEAD_DIM), F32)], axis=0)
    cos = jnp.tile(jnp.cos(ang), (1, 2))
    sin = jnp.tile(jnp.sin(ang), (1, 2))
    first = (jnp.arange(LANES) % half) < (half // 2)
    sin_a = jnp.where(first[None, :], -sin, 0.0)
    sin_b = jnp.where(first[None, :], 0.0, sin)
    return cos, sin_a, sin_b


ATTN_STAGE_CHUNKS = 4
ATTN_ROW_BLOCK = 64


def _rope_rows(x, cos, sin_a, sin_b):
    quarter = DA_HEAD_DIM // 4
    return x * cos + pltpu.roll(x, LANES - quarter, 1) * sin_a + pltpu.roll(x, quarter, 1) * sin_b


def _attn_kernel(*refs, tq, tk, n_lat_k, lam_init):
    if n_lat_k:
        lam_ref, subln_ref, q_ref, kc_ref, vc_ref, kl_ref, vl_ref, cos_ref, sa_ref, sb_ref, o_ref = refs[:11]
        kr_ref = refs[-8]
    else:
        lam_ref, subln_ref, q_ref, kc_ref, vc_ref, o_ref = refs[:6]
    qs_ref, s_ref, p_ref, m_ref, a_ref, acc_ref, vx_ref = refs[-7:]
    rb = ATTN_ROW_BLOCK
    n_ctx = kc_ref.shape[0]
    n_lat = n_lat_k * tk
    scale = DA_HEAD_DIM ** -0.5

    @pl.when(pl.program_id(2) == 0)
    def _():
        if n_lat_k:
            vx_ref[0:n_lat, 0:LANES] = vl_ref[...]
            for c in range(n_lat_k):
                rows = slice(c * tk, (c + 1) * tk)
                kr_ref[rows, :] = _rope_rows(kl_ref[rows, :].astype(F32), cos_ref[rows, :], sa_ref[rows, :],
                                             sb_ref[rows, :]).astype(BF16)
        vx_ref[n_lat:, 0:LANES] = vc_ref[...]
        vx_ref[:, LANES:2 * LANES] = jnp.ones((vx_ref.shape[0], LANES), BF16)

    qf = q_ref[...].astype(F32)
    if n_lat_k:
        qrows = pl.ds(pl.multiple_of(pl.program_id(2) * tq, tq), tq)
        qf = _rope_rows(qf, cos_ref[qrows, :], sa_ref[qrows, :], sb_ref[qrows, :])
    q = (qf * scale).astype(BF16)
    lane = lax.broadcasted_iota(jnp.int32, q.shape, 1)
    zero = jnp.zeros_like(q)
    qs_ref[0:tq, :] = jnp.where(lane < DA_HEAD_DIM, q, zero)
    qs_ref[tq:2 * tq, :] = jnp.where(lane >= DA_HEAD_DIM, q, zero)
    m_ref[...] = jnp.full(m_ref.shape, -1e30, F32)
    acc_ref[...] = jnp.zeros(acc_ref.shape, F32)

    def key_rows(t, nk):
        return pl.ds(t * tk if isinstance(t, int) else pl.multiple_of(t * tk, tk), nk)

    def keys(t):
        if isinstance(t, int) and t == n_lat_k:
            return kc_ref[...], n_ctx
        return kr_ref[key_rows(t, tk), :], tk

    def width(t):
        return n_ctx if isinstance(t, int) and t == n_lat_k else tk

    def stage(t_next, t_cur, t_prev, buf):
        other = 1 - buf
        if t_next is not None:
            k_next, nk_next = keys(t_next)
        if t_prev is not None:
            nk_prev = width(t_prev)
            vx = vx_ref[key_rows(t_prev, nk_prev), :]
        rows_c = 2 * tq // ATTN_STAGE_CHUNKS
        for c in range(ATTN_STAGE_CHUNKS):
            rc = slice(c * rows_c, (c + 1) * rows_c)
            if t_next is not None:
                s_ref[other, rc, 0:nk_next] = lax.dot_general(qs_ref[rc, :], k_next, (((1,), (1,)), ((), ())),
                                                              preferred_element_type=F32)
            if t_cur is not None:
                nk = width(t_cur)
                reps = nk // LANES
                for r in range(c * rows_c // rb, (c + 1) * rows_c // rb):
                    rows = slice(r * rb, (r + 1) * rb)
                    m_prev = m_ref[rows, :]
                    m_new = jnp.maximum(m_prev, jnp.max(s_ref[buf, rows, 0:nk], axis=-1, keepdims=True))
                    a_ref[buf, rows, :] = jnp.exp(m_prev - m_new)
                    m_ref[rows, :] = m_new
                    p = jnp.exp(s_ref[buf, rows, 0:nk] - jnp.concatenate([m_new] * reps, axis=1))
                    p_ref[buf, rows, 0:nk] = p.astype(BF16)
            if t_prev is not None:
                alpha = a_ref[other, rc, :]
                pv = jnp.dot(p_ref[other, rc, 0:nk_prev], vx, preferred_element_type=F32)
                acc_ref[rc, :] = jnp.concatenate([alpha, alpha], axis=1) * acc_ref[rc, :] + pv

    last = n_lat_k
    loop_pairs = (n_lat_k - 2) // 2 if (n_lat_k >= 4 and n_lat_k % 2 == 0) else 0
    stage(0, None, None, 1)
    t = 0
    while t <= last:
        if t == 1 and loop_pairs:
            def body(i, carry):
                stage(2 + 2 * i, 1 + 2 * i, 2 * i, 1)
                stage(3 + 2 * i, 2 + 2 * i, 1 + 2 * i, 0)
                return carry
            lax.fori_loop(0, loop_pairs, body, 0)
            t += 2 * loop_pairs
            continue
        stage(t + 1 if t < last else None, t, t - 1 if t >= 1 else None, t % 2)
        t += 1
    stage(None, None, last, (last + 1) % 2)

    lf = lam_ref[...]
    lam = (jnp.exp(jnp.sum(lf[0:1] * lf[1:2], axis=-1, keepdims=True))
           - jnp.exp(jnp.sum(lf[2:3] * lf[3:4], axis=-1, keepdims=True)) + lam_init)
    o = acc_ref[:, 0:LANES] / acc_ref[:, LANES:2 * LANES]
    out = o[:tq] - lam * o[tq:]
    y = _rms(out) * subln_ref[...] * (1.0 - lam_init)
    o_ref[...] = y.astype(BF16)


def _attn_call(p, tables, da_lambda, da_subln, lam_init, dims, *, tq, q_row0, n_q, rows, with_lat):
    batch, seq, ctx_len, nl = dims["batch"], dims["seq"], dims["ctx"], dims["nl"]
    tk = 512
    k_col, v_col = DA_WIDTH // LANES, V_OFF // LANES
    q_map = lambda b, h, i: (q_row0 // tq + b * n_q + i, h)
    o_map = lambda b, h, i: (b * n_q + i, h)
    kv_lat = lambda off: pl.BlockSpec((seq, LANES), lambda b, h, i: (b, off + h))
    kv_ctx = lambda off: pl.BlockSpec((ctx_len, LANES), lambda b, h, i: (nl // ctx_len + b, off + h))
    in_specs = [pl.BlockSpec((4, DA_HEAD_DIM), lambda b, h, i: (0, 0)),
                pl.BlockSpec((1, DA_V_DIM), lambda b, h, i: (0, 0)),
                pl.BlockSpec((tq, LANES), q_map),
                kv_ctx(k_col), kv_ctx(v_col)]
    args = [da_lambda, da_subln.reshape(1, DA_V_DIM), p, p, p]
    n_keys = ctx_len + (seq if with_lat else 0)
    scratch = [pltpu.VMEM((2 * tq, LANES), BF16), pltpu.VMEM((2, 2 * tq, tk), F32), pltpu.VMEM((2, 2 * tq, tk), BF16),
               pltpu.VMEM((2 * tq, LANES), F32), pltpu.VMEM((2, 2 * tq, LANES), F32),
               pltpu.VMEM((2 * tq, 2 * LANES), F32), pltpu.VMEM((n_keys, 2 * LANES), BF16)]
    if with_lat:
        tab = pl.BlockSpec((seq, LANES), lambda b, h, i: (0, 0))
        in_specs += [kv_lat(k_col), kv_lat(v_col), tab, tab, tab]
        args += [p, p, *tables]
        scratch = [pltpu.VMEM((seq, LANES), BF16)] + scratch
    kern = functools.partial(_attn_kernel, tq=tq, tk=tk, n_lat_k=seq // tk if with_lat else 0, lam_init=lam_init)
    return pl.pallas_call(
        kern,
        out_shape=jax.ShapeDtypeStruct((rows, DA_WIDTH), BF16),
        grid=(batch, DA_HEADS, n_q),
        in_specs=in_specs,
        out_specs=pl.BlockSpec((tq, LANES), o_map),
        scratch_shapes=scratch,
        compiler_params=_cparams("parallel", "parallel", "arbitrary"),
        name="diff_attn" if with_lat else "diff_attn_ctx",
    )(*args)


def _attention(p, tables, da_lambda, da_subln, lam_init, with_ctx, dims):
    batch, seq, ctx_len, nl = dims["batch"], dims["seq"], dims["ctx"], dims["nl"]
    tq = _pick_tile(seq, (1024, 512, TOK_TILE))
    y = _attn_call(p, tables, da_lambda, da_subln, lam_init, dims, tq=tq, q_row0=0, n_q=seq // tq, rows=nl,
                   with_lat=True)
    if with_ctx:
        tqc = TOK_TILE
        y_ctx = _attn_call(p, tables, da_lambda, da_subln, lam_init, dims, tq=tqc, q_row0=nl, n_q=ctx_len // tqc,
                           rows=batch * ctx_len, with_lat=False)
        y = jnp.concatenate([y, y_ctx], axis=0)
    return y


def _softplus(x):
    return jnp.maximum(x, 0.0) + jnp.log(1.0 + jnp.exp(-jnp.abs(x)))


def _dt_lanes(h, w, bias, alog):
    raw = lax.dot_general(h, w.astype(BF16), (((1,), (1,)), ((), ())), preferred_element_type=F32)
    n = 2 * SSD_HEADS
    dt = _softplus(raw + bias)
    lane = lax.broadcasted_iota(jnp.int32, dt.shape, 1)
    dta = pltpu.roll(dt, n, 1) * (-jnp.exp(alog))
    return jnp.where(lane < n, dt, jnp.where(lane < 2 * n, dta, 0.0))


def _conv_kernel(prev_ref, x_ref, next_ref, w_ref, b_ref, h_ref, wdt_ref, bdt_ref, alog_ref, o_ref, dd_ref,
                 *, tm, n_lat_tiles, lat_per_seq, ctx_per_seq):
    @pl.when(pl.program_id(1) == 0)
    def _():
        dd_ref[...] = _dt_lanes(h_ref[...], wdt_ref[...], bdt_ref[...], alog_ref[...])

    i = pl.program_id(0)
    is_lat = i < n_lat_tiles
    pos = jnp.where(is_lat, i % lat_per_seq, (i - n_lat_tiles) % ctx_per_seq)
    per = jnp.where(is_lat, lat_per_seq, ctx_per_seq)
    keep_prev = (pos > 0).astype(F32)
    keep_next = (pos < per - 1).astype(F32)
    ext = jnp.concatenate([prev_ref[...].astype(F32) * keep_prev, x_ref[...].astype(F32),
                           next_ref[...].astype(F32) * keep_next], axis=0)
    n = tm + 2 * SUBLANES
    w = w_ref[...]
    acc = jnp.zeros((tm, ext.shape[1]), F32) + b_ref[...]
    for k in range(SSD_CONV):
        shift = (SSD_CONV // 2 - k) % n
        rolled = ext if shift == 0 else pltpu.roll(ext, shift, 0)
        acc = acc + rolled[SUBLANES:SUBLANES + tm] * w[k:k + 1, :]
    o_ref[...] = _silu(acc).astype(BF16)


def _ssd_prep(p, conv_w, conv_b, h, w_in_t, layer, bias2, alog2, dims):
    t = p.shape[0]
    d = h.shape[1]
    tm = TOK_TILE
    cb = 512
    col0 = XBC_OFF // cb
    r8 = tm // SUBLANES
    last8 = t // SUBLANES - 1
    assert REF_DT_OFF % LANES == 0
    kern = functools.partial(_conv_kernel, tm=tm, n_lat_tiles=dims["nl"] // tm, lat_per_seq=dims["seq"] // tm,
                             ctx_per_seq=dims["ctx"] // tm)
    return pl.pallas_call(
        kern,
        out_shape=(jax.ShapeDtypeStruct((t, SSD_XBC), BF16), jax.ShapeDtypeStruct((t, LANES), F32)),
        grid=(t // tm, SSD_XBC // cb),
        in_specs=[pl.BlockSpec((SUBLANES, cb), lambda i, j: (jnp.maximum(i * r8 - 1, 0), col0 + j)),
                  pl.BlockSpec((tm, cb), lambda i, j: (i, col0 + j)),
                  pl.BlockSpec((SUBLANES, cb), lambda i, j: (jnp.minimum((i + 1) * r8, last8), col0 + j)),
                  pl.BlockSpec((SSD_CONV, cb), lambda i, j: (0, j)),
                  pl.BlockSpec((1, cb), lambda i, j: (0, j)),
                  pl.BlockSpec((tm, d), lambda i, j: (i, 0)),
                  pl.BlockSpec((None, LANES, d), lambda i, j: (layer, REF_DT_OFF // LANES, 0)),
                  pl.BlockSpec((1, LANES), lambda i, j: (0, 0)),
                  pl.BlockSpec((1, LANES), lambda i, j: (0, 0))],
        out_specs=(pl.BlockSpec((tm, cb), lambda i, j: (i, j)), pl.BlockSpec((tm, LANES), lambda i, j: (i, 0))),
        compiler_params=_cparams("parallel", "arbitrary"),
        name="ssd_prep",
    )(p, p, p, conv_w, conv_b.reshape(1, SSD_XBC), h, w_in_t, bias2, alog2)


def _split3(x):
    hi = x.astype(BF16)
    rest = x - hi.astype(F32)
    mid = rest.astype(BF16)
    return hi, mid, (rest - mid.astype(F32)).astype(BF16)


def _ssd_scan_kernel(*refs, direction, final, ncc, write_ctx):
    if final:
        xbc_ref, dd_ref, yf_ref, z_ref, dskip_ref, norm_ref, o_ref, h_ref = refs
    else:
        xbc_ref, dd_ref, o_ref, h_ref = refs
    j = pl.program_id(1)
    q = SSD_CHUNK
    gw = SSD_HPG * SSD_HEAD_DIM

    @pl.when(j == 0)
    def _():
        h_ref[...] = jnp.zeros_like(h_ref)

    xbc = xbc_ref[...]
    xs = xbc[:, :SSD_WIDTH].astype(F32)
    bm = xbc[:, SSD_WIDTH:SSD_WIDTH + SSD_GROUPS * SSD_STATE]
    cm = xbc[:, SSD_WIDTH + SSD_GROUPS * SSD_STATE:]
    dd = dd_ref[...]

    ii =lax.broadcasted_iota(jnp.int32, (q, q), 0)
    jj = lax.broadcasted_iota(jnp.int32, (q, q), 1)
    if direction == 0:
        mask = jj <= ii
        last = q - 1
    else:
        mask = jj >= ii
        last = 0
    tri = jnp.where(mask, 1.0, 0.0).astype(BF16)
    ac = sum(jnp.dot(tri, piece, preferred_element_type=F32) for piece in _split3(dd))
    ac_t = ac.T

    er = lax.broadcasted_iota(jnp.int32, (LANES, SSD_WIDTH), 0)
    ec = lax.broadcasted_iota(jnp.int32, (LANES, SSD_WIDTH), 1) // SSD_HEAD_DIM
    dt_col = direction * SSD_HEADS
    ac_col = 2 * SSD_HEADS + direction * SSD_HEADS
    e_dt = jnp.where(er == ec + dt_col, 1.0, 0.0).astype(BF16)
    e_ac = jnp.where(er == ec + ac_col, 1.0, 0.0).astype(BF16)
    dt_exp = sum(jnp.dot(piece, e_dt, preferred_element_type=F32) for piece in _split3(dd))
    ac_exp = sum(jnp.dot(piece, e_ac, preferred_element_type=F32) for piece in _split3(ac))
    ac_last = ac_exp[last:last + 1, :]
    eac = jnp.exp(ac_exp)
    dec_end = jnp.exp(ac_last - ac_exp)
    chunk_dec = jnp.exp(ac_last)
    xdt = xs * dt_exp
    xdt_b = xdt.astype(BF16)
    xde_b = (xdt * dec_end).astype(BF16)

    lane = lax.broadcasted_iota(jnp.int32, (q, LANES), 1)
    left = lane < SSD_HEAD_DIM
    zero_b = jnp.zeros((q, LANES), BF16)
    pieces = []
    for g in range(SSD_GROUPS):
        bg = bm[:, g * SSD_STATE:(g + 1) * SSD_STATE]
        cg = cm[:, g * SSD_STATE:(g + 1) * SSD_STATE]
        cb = lax.dot_general(cg, bg, (((1,), (1,)), ((), ())), preferred_element_type=F32)
        h_t = h_ref[g]
        y_off = jnp.dot(cg, h_t.astype(BF16), preferred_element_type=F32) * eac[:, g * gw:(g + 1) * gw]
        bg_t = bg.astype(F32).T.astype(BF16)
        s_t = jnp.dot(bg_t, xde_b[:, g * gw:(g + 1) * gw], preferred_element_type=F32)
        h_ref[g] = h_t * chunk_dec[:, g * gw:(g + 1) * gw] + s_t
        for kp in range(SSD_HPG // 2):
            k0 = g * SSD_HPG + 2 * kp
            ms = []
            for k in (k0, k0 + 1):
                c = ac_col + k
                seg = ac[:, c:c + 1] - ac_t[c:c + 1, :]
                ms.append((cb * jnp.exp(jnp.where(mask, seg, -1e30))).astype(BF16))
            xp = xdt_b[:, k0 * SSD_HEAD_DIM:k0 * SSD_HEAD_DIM + LANES]
            y_diag = jnp.dot(jnp.concatenate(ms, axis=1),
                             jnp.concatenate([jnp.where(left, xp, zero_b), jnp.where(left, zero_b, xp)], axis=0),
                             preferred_element_type=F32)
            pieces.append(y_diag + y_off[:, kp * LANES:(kp + 1) * LANES])
    y = jnp.concatenate(pieces, axis=1)

    def emit():
        if final:
            yt = y + yf_ref[...] + dskip_ref[...] * xs
            yt = yt * _silu(z_ref[...].astype(F32))
            o_ref[...] = (_rms(yt) * norm_ref[...]).astype(BF16)
        else:
            o_ref[...] = y

    if write_ctx:
        emit()
    else:
        pl.when(j >= ncc)(emit)


def _ssd_scan(xbc, dd, direction, dims, with_ctx, final_args=None):
    batch, seq, ctx_len, nl = dims["batch"], dims["seq"], dims["ctx"], dims["nl"]
    q = SSD_CHUNK
    ncl, ncc = seq // q, ctx_len // q
    final = final_args is not None
    rows = nl + (batch * ctx_len if with_ctx else 0)

    def chunk(b, j):
        jc = j if direction == 0 else ncc - 1 - j
        jl = (j - ncc) if direction == 0 else ncl - 1 - (j - ncc)
        return jnp.where(j < ncc, nl // q + b * ncc + jc, b * ncl + jl)

    def out_chunk(b, j):
        if with_ctx:
            return chunk(b, j)
        return chunk(b, jnp.maximum(j, ncc))

    in_specs = [pl.BlockSpec((q, SSD_XBC), lambda b, j: (chunk(b, j), 0)),
                pl.BlockSpec((q, LANES), lambda b, j: (chunk(b, j), 0))]
    args = [xbc, dd]
    if final:
        yf, p, dskip, norm = final_args
        in_specs += [pl.BlockSpec((q, SSD_WIDTH), lambda b, j: (out_chunk(b, j), 0)),
                     pl.BlockSpec((q, SSD_WIDTH), lambda b, j: (chunk(b, j), Z_OFF // SSD_WIDTH)),
                     pl.BlockSpec((1, SSD_WIDTH), lambda b, j: (0, 0)),
                     pl.BlockSpec((1, SSD_WIDTH), lambda b, j: (0, 0))]
        args += [yf, p, dskip, norm]
    return pl.pallas_call(
        functools.partial(_ssd_scan_kernel, direction=direction, final=final, ncc=ncc, write_ctx=with_ctx),
        out_shape=jax.ShapeDtypeStruct((rows, SSD_WIDTH), BF16 if final else F32),
        grid=(batch, ncc + ncl),
        in_specs=in_specs,
        out_specs=pl.BlockSpec((q, SSD_WIDTH), lambda b, j: (out_chunk(b, j), 0)),
        scratch_shapes=[pltpu.VMEM((SSD_GROUPS, SSD_STATE, SSD_HPG * SSD_HEAD_DIM), F32)],
        compiler_params=_cparams("parallel", "arbitrary"),
        name="ssd_scan_bwd" if direction else "ssd_scan_fwd",
    )(*args)


S5_BLOCK_GROUPS = LANES // S5_GROUP
S5_BLOCKS = S5_GROUPS // S5_BLOCK_GROUPS
S5_BS = S5_BLOCK_GROUPS * S5_STATE
S5_LAG_ROWS = 24


def _expm1(x):
    poly = 1.0 + x / 10.0
    for n in range(9, 1, -1):
        poly = 1.0 + (x / n) * poly
    return jnp.where(jnp.abs(x) < 0.35, x * poly, jnp.exp(x) - 1.0)


def _s5_params(lam_re, lam_im, log_step, b_re, b_im, c_re, c_im, s5_d):
    nb, gl = S5_BLOCKS, S5_BLOCK_GROUPS
    rows = [lam_re[0], lam_im[0], lam_re[1], lam_im[1],
            jnp.repeat(log_step[0], S5_STATE), jnp.repeat(log_step[1], S5_STATE)]
    rows = [r.reshape(nb, S5_BS) for r in rows] + [jnp.zeros((nb, S5_BS), F32)] * 2
    lam_rows = jnp.stack(rows, axis=1)
    eye = jnp.eye(gl, dtype=F32)

    def bd_in(b):
        return jnp.einsum("jgpe,gh->jgehp", b.reshape(nb, gl, S5_STATE, S5_GROUP), eye).reshape(nb, LANES, S5_BS)

    def bd_out(c):
        return jnp.einsum("jgfp,gh->jgfhp", c.reshape(nb, gl, S5_GROUP, S5_STATE), eye).reshape(nb, LANES, S5_BS)

    b_bd = jnp.stack([bd_in(b_re), bd_in(b_im)], axis=1)
    ct_bd = jnp.stack([bd_out(c_re), bd_out(c_im)], axis=1)
    return lam_rows, b_bd, ct_bd, s5_d.reshape(nb, 1, LANES)


def _gelu_tanh(x):
    return 0.5 * x * (1.0 + jnp.tanh(math.sqrt(2.0 / math.pi) * (x + 0.044715 * x * x * x)))


def _s5_kernel(u_ref, lam_ref, b_ref, ct_ref, d_ref, o_ref,
               x_ref, w_ref, m_ref, s_ref, y_ref, taps_ref, pw_ref, bb_ref, *, batch, ncl, ncc):
    tc = S5_CHUNK
    nch = x_ref.shape[0]
    sw = S5_BS
    nt = (((1,), (1,)), ((), ()))

    for s in range(tc):
        x_ref[:, s * LANES:(s + 1) * LANES] = u_ref[pl.ds(s, nch, stride=tc), :].astype(BF16)

    rows = lam_ref[0]
    kk = lax.broadcasted_iota(jnp.int32, (S5_LAG_ROWS, sw), 0).astype(F32)
    b_re, b_im = b_ref[0, 0], b_ref[0, 1]
    for d in range(2):
        l_re, l_im = rows[2 * d:2 * d + 1], rows[2 * d + 1:2 * d + 2]
        delta = jnp.exp(rows[4 + d:5 + d])
        lr, li = l_re * delta, l_im * delta
        mag = jnp.exp(kk * lr)
        pw_ref[2 * d] = mag * jnp.cos(kk * li)
        pw_ref[2 * d + 1] = mag * jnp.sin(kk * li)
        xr = _expm1(lr) * jnp.cos(li) - 2.0 * jnp.sin(0.5 * li) ** 2
        xi = jnp.exp(lr) * jnp.sin(li)
        den = l_re * l_re + l_im * l_im
        co_re = (xr * l_re + xi * l_im) / den
        co_im = (xi * l_re - xr * l_im) / den
        bb_ref[2 * d] = co_re * b_re - co_im * b_im
        bb_ref[2 * d + 1] = co_re * b_im + co_im * b_re

    def power(d, lag):
        return pw_ref[2 * d, lag:lag + 1, :], pw_ref[2 * d + 1, lag:lag + 1, :]

    ct_re, ct_im = ct_ref[0, 0], ct_ref[0, 1]
    ct_re_b, ct_im_b = ct_re.astype(BF16), ct_im.astype(BF16)
    for d in range(2):
        tp = []
        for part in range(2):
            q = 2 * d + part
            for s in range(tc):
                pr, pi = power(d, tc - 1 - s if d == 0 else s)
                if part == 0:
                    tile = pr * bb_ref[2 * d] - pi * bb_ref[2 * d + 1]
                else:
                    tile = pr * bb_ref[2 * d + 1] + pi * bb_ref[2 * d]
                w_ref[q, s * LANES:(s + 1) * LANES, :] = tile.astype(BF16)
            w = w_ref[q]
            s_ref[:, q * sw:(q + 1) * sw] = jnp.dot(x_ref[...], w, preferred_element_type=F32)
            tp.append(lax.dot_general(w, ct_re_b if part == 0 else ct_im_b, nt, preferred_element_type=F32))
        taps_ref[d] = tp[0] - tp[1]

    ri = lax.broadcasted_iota(jnp.int32, (LANES, LANES), 0)
    ci = lax.broadcasted_iota(jnp.int32, (LANES, LANES), 1)
    skip = jnp.where(ri == ci, d_ref[0], 0.0)
    for s in range(tc):
        for t in range(tc):
            if t >= s:
                lag_tile = tc - 1 - (t - s)
                tile = taps_ref[0, lag_tile * LANES:(lag_tile + 1) * LANES, :]
                if t == s:
                    tile = tile + taps_ref[1, 0:LANES, :] + skip
            else:
                tile = taps_ref[1, (s - t) * LANES:(s - t + 1) * LANES, :]
            m_ref[s * LANES:(s + 1) * LANES, t * LANES:(t + 1) * LANES] = tile.astype(BF16)

    af_re, af_im = power(0, tc)
    ab_re, ab_im = power(1, tc)

    def advance(row, h, dir_off, ar, ai):
        h_re, h_im = h
        s_re = s_ref[pl.ds(row, 1), dir_off:dir_off + sw]
        s_im = s_ref[pl.ds(row, 1), dir_off + sw:dir_off + 2 * sw]
        s_ref[pl.ds(row, 1), dir_off:dir_off + sw] = h_re
        s_ref[pl.ds(row, 1), dir_off + sw:dir_off + 2 * sw] = h_im
        return ar * h_re - ai * h_im + s_re, ar * h_im + ai * h_re + s_im

    def sweep(first_chunk, n):
        def body(i, c):
            return tuple((advance(first_chunk(b) + i, c[b][0], 0, af_re, af_im),
                          advance(first_chunk(b) + n - 1 - i, c[b][1], 2 * sw, ab_re, ab_im)) for b in range(batch))
        return body

    zero = jnp.zeros((1, sw), F32)
    c = lax.fori_loop(0, ncc, sweep(lambda b: batch * ncl + b * ncc, ncc), (((zero, zero), (zero, zero)),) * batch)
    lax.fori_loop(0, ncl, sweep(lambda b: b * ncl, ncl), c)

    y = jnp.dot(x_ref[...], m_ref[...], preferred_element_type=F32)
    for q in range(4):
        d, part = divmod(q, 2)
        for t in range(tc):
            pr, pi = power(d, t + 1 if d == 0 else tc - t)
            tile = ct_re * pr - ct_im * pi if part == 0 else -(ct_re * pi + ct_im * pr)
            w_ref[q, t * LANES:(t + 1) * LANES, :] = tile.astype(BF16)
        h = s_ref[:, q * sw:(q + 1) * sw].astype(BF16)
        y = y + lax.dot_general(h, w_ref[q], nt, preferred_element_type=F32)
    y_ref[...] = y
    for t in range(tc):
        o_ref[pl.ds(t, nch, stride=tc), :] = _gelu_tanh(y_ref[:, t * LANES:(t + 1) * LANES])


def _s5(u, params, dims):
    lam_rows, b_bd, ct_bd, d_skip = params
    t = u.shape[0]
    nch = t // S5_CHUNK
    ncl, ncc = dims["seq"] // S5_CHUNK, dims["ctx"] // S5_CHUNK
    width = S5_CHUNK * LANES
    pspec = pl.BlockSpec((1, 2, LANES, S5_BS), lambda i: (i, 0, 0, 0))
    return pl.pallas_call(
        functools.partial(_s5_kernel, batch=dims["batch"], ncl=ncl, ncc=ncc),
        out_shape=jax.ShapeDtypeStruct((t, S5_WIDTH), F32),
        grid=(S5_BLOCKS,),
        in_specs=[pl.BlockSpec((t, LANES), lambda i: (0, i)),
                  pl.BlockSpec((1, SUBLANES, S5_BS), lambda i: (i, 0, 0)), pspec, pspec,
                  pl.BlockSpec((1, 1, LANES), lambda i: (i, 0, 0))],
        out_specs=pl.BlockSpec((t, LANES), lambda i: (0, i)),
        scratch_shapes=[pltpu.VMEM((nch, width), BF16),
                        pltpu.VMEM((4, width, S5_BS), BF16),
                        pltpu.VMEM((width, width), BF16),
                        pltpu.VMEM((nch, 4 * S5_BS), F32),
                        pltpu.VMEM((nch, width), F32),
                        pltpu.VMEM((2, width, LANES), F32),
                        pltpu.VMEM((4, S5_LAG_ROWS, S5_BS), F32),
                        pltpu.VMEM((4, LANES, S5_BS), F32)],
        compiler_params=_cparams("parallel"),
        name="s5_scan",
    )(u, lam_rows, b_bd, ct_bd, d_skip)


def _glu_kernel(x_ref, wa_ref, wb_ref, ba_ref, bb_ref, o_ref, wab_ref, wbb_ref):
    @pl.when(pl.program_id(1) == 0)
    def _():
        wab_ref[...] = wa_ref[...].astype(BF16)
        wbb_ref[...] = wb_ref[...].astype(BF16)

    x = x_ref[...].astype(BF16)
    a = jnp.dot(x, wab_ref[...], preferred_element_type=F32) + ba_ref[...]
    b = jnp.dot(x, wbb_ref[...], preferred_element_type=F32) + bb_ref[...]
    o_ref[...] = (a * _sigmoid(b)).astype(BF16)


def _glu(x, w, bias, layer, rows, tm):
    k = x.shape[1]
    tn = W_TILE
    nb = S5_WIDTH // tn
    return pl.pallas_call(
        _glu_kernel,
        out_shape=jax.ShapeDtypeStruct((rows, S5_WIDTH), BF16),
        grid=(nb, rows // tm),
        in_specs=[pl.BlockSpec((tm, k), lambda j, i: (i, 0)),
                  pl.BlockSpec((None, k, tn), lambda j, i: (layer, 0, j)),
                  pl.BlockSpec((None, k, tn), lambda j, i: (layer, 0, nb + j)),
                  pl.BlockSpec((None, 1, tn), lambda j, i: (layer, 0, j)),
                  pl.BlockSpec((None, 1, tn), lambda j, i: (layer, 0, nb + j))],
        out_specs=pl.BlockSpec((tm, tn), lambda j, i: (i, j)),
        scratch_shapes=[pltpu.VMEM((k, tn), BF16), pltpu.VMEM((k, tn), BF16)],
        compiler_params=_cparams("parallel", "arbitrary"),
        name="s5_glu",
    )(x, w, w, bias, bias)


def _merge_kernel(ya_ref, yb_ref, yc_ref, wa_ref, wb_ref, wc_ref, ga_ref, gb_ref, gc_ref, o_ref, wbf_ref):
    @pl.when(pl.program_id(1) == 0)
    def _():
        for n, w_ref in enumerate((wa_ref, wb_ref, wc_ref)):
            wbf_ref[n] = w_ref[0].astype(BF16)

    acc = None
    for n, (y_ref, g_ref) in enumerate(((ya_ref, ga_ref), (yb_ref, gb_ref), (yc_ref, gc_ref))):
        br = jnp.dot(y_ref[...], wbf_ref[n], preferred_element_type=F32)
        term = _sigmoid(g_ref[...].astype(F32)) * br
        acc = term if acc is None else acc + term
    o_ref[...] = acc.astype(BF16)


def _merge(ya, yb, yc, w_branch, layer, gates, rows, tm):
    k = ya.shape[1]
    d = w_branch.shape[3]
    tn = W_TILE
    gstep = d // tn
    yspec = pl.BlockSpec((tm, k), lambda j, i: (i, 0))
    wspec = lambda n: pl.BlockSpec((None, 1, k, tn), lambda j, i: (layer, n, 0, j))
    gspec = lambda n: pl.BlockSpec((tm, tn), lambda j, i: (i, n * gstep + j))
    return pl.pallas_call(
        _merge_kernel,
        out_shape=jax.ShapeDtypeStruct((rows, d), BF16),
        grid=(d // tn, rows // tm),
        in_specs=[yspec, yspec, yspec, wspec(0), wspec(1), wspec(2), gspec(0), gspec(1), gspec(2)],
        out_specs=pl.BlockSpec((tm, tn), lambda j, i: (i, j)),
        scratch_shapes=[pltpu.VMEM((N_BRANCH, k, tn), BF16)],
        compiler_params=_cparams("parallel", "arbitrary"),
        name="branch_merge",
    )(ya, yb, yc, w_branch, w_branch, w_branch, gates, gates, gates)


def _out_proj_kernel(*refs, n_lat):
    g_ref, w_ref, *x_refs, mod_ref, npost_ref, npre_ref, xo_ref, ho_ref = refs
    o = jnp.dot(g_ref[...], w_ref[...], preferred_element_type=F32)
    gate = mod_ref[0, 2:3, :]
    xn = _stream_tile(x_refs, n_lat) + gate * (_rms(o) * npost_ref[...])
    xo_ref[...] = xn
    shift = mod_ref[0, 3:4, :]
    scale = mod_ref[0, 4:5, :]
    ho_ref[...] = (_rms(xn) * npre_ref[...] * (1.0 + scale) + shift).astype(BF16)


def _out_proj(g, w_out, xs, mod, npost, npre, dims):
    r, d = g.shape
    tm = TOK_TILE * (2 if len(xs) == 1 else 1)
    seg = functools.partial(_seg_of_tile, tm=tm, n_lat=dims["nl"], seq=dims["seq"], batch=dims["batch"])
    row = pl.BlockSpec((tm, d), lambda i: (i, 0))
    vec = pl.BlockSpec((1, d), lambda i: (0, 0))
    return pl.pallas_call(
        functools.partial(_out_proj_kernel, n_lat=dims["nl"] // tm),
        out_shape=(jax.ShapeDtypeStruct((r, d), F32), jax.ShapeDtypeStruct((r, d), BF16)),
        grid=(r // tm,),
        in_specs=[row, pl.BlockSpec((d, d), lambda i: (0, 0))]
        + _stream_specs(xs, tm, d, dims["nl"])
        + [pl.BlockSpec((1, 6, d), lambda i: (seg(i), 0, 0)), vec, vec],
        out_specs=(row, row),
        compiler_params=_cparams("parallel"),
        name="out_proj",
    )(g, w_out, *xs, mod, npost.reshape(1, d), npre.reshape(1, d))


def _ffn_up_kernel(h_ref, wg_ref, wu_ref, o_ref, wgb_ref, wub_ref):
    @pl.when(pl.program_id(1) == 0)
    def _():
        wgb_ref[...] = wg_ref[...].astype(BF16)
        wub_ref[...] = wu_ref[...].astype(BF16)

    h = h_ref[...]
    a = jnp.dot(h, wgb_ref[...], preferred_element_type=F32)
    b = jnp.dot(h, wub_ref[...], preferred_element_type=F32)
    o_ref[...] = (_silu(a) * b).astype(BF16)


def _ffn_up(h, wg, wu, layer, tm):
    r, d = h.shape
    f = wg.shape[2]
    tn = W_TILE
    wspec = pl.BlockSpec((None, d, tn), lambda j, i: (layer, 0, j))
    return pl.pallas_call(
        _ffn_up_kernel,
        out_shape=jax.ShapeDtypeStruct((r, f), BF16),
        grid=(f // tn, r // tm),
        in_specs=[pl.BlockSpec((tm, d), lambda j, i: (i, 0)), wspec, wspec],
        out_specs=pl.BlockSpec((tm, tn), lambda j, i: (i, j)),
        scratch_shapes=[pltpu.VMEM((d, tn), BF16), pltpu.VMEM((d, tn), BF16)],
        compiler_params=_cparams("parallel", "arbitrary"),
        name="ffn_up",
    )(h, wg, wu)


def _ffn_down_kernel(a_ref, w_ref, x_ref, mod_ref, npost_ref, o_ref, acc_ref):
    k = pl.program_id(1)

    @pl.when(k == 0)
    def _():
        acc_ref[...] = jnp.zeros_like(acc_ref)

    acc_ref[...] += jnp.dot(a_ref[...], w_ref[...], preferred_element_type=F32)

    @pl.when(k == pl.num_programs(1) - 1)
    def _():
        gate = mod_ref[0, 5:6, :]
        o_ref[...] = x_ref[...] + gate * (_rms(acc_ref[...]) * npost_ref[...])


def _ffn_down(act, wd, x, mod, npost, dims):
    r, f = act.shape
    d = wd.shape[1]
    big = r == dims["nl"] and dims["seq"] % 1024 == 0 and f % (8 * LANES) == 0
    tm, nk = (1024, 8) if big else (2 * TOK_TILE, 4)
    tk = f // nk
    seg = functools.partial(_seg_of_tile, tm=tm, n_lat=dims["nl"], seq=dims["seq"], batch=dims["batch"])
    row = pl.BlockSpec((tm, d), lambda i, k: (i, 0))
    return pl.pallas_call(
        _ffn_down_kernel,
        out_shape=jax.ShapeDtypeStruct((r, d), F32),
        grid=(r // tm, nk),
        in_specs=[pl.BlockSpec((tm, tk), lambda i, k: (i, k)),
                  pl.BlockSpec((tk, d), lambda i, k: (k, 0)),
                  row,
                  pl.BlockSpec((1, 6, d), lambda i, k: (seg(i), 0, 0)),
                  pl.BlockSpec((1, d), lambda i, k: (0, 0))],
        out_specs=row,
        scratch_shapes=[pltpu.VMEM((tm, d), F32)],
        compiler_params=_cparams("parallel", "arbitrary"),
        name="ffn_down",
    )(act, wd, x, mod, npost.reshape(1, d))


def _pick_tile(rows, cands):
    for c in cands:
        if rows % c == 0:
            return c
    raise ValueError(f"no tile in {cands} divides {rows}")


def kernel(x, c, ctx, c_ctx, ada_w, ada_b, norm_mix_pre, norm_mix_post, norm_ffn_pre, norm_ffn_post, w_in, da_lambda, da_subln, ssd_conv_w, ssd_conv_b, ssd_dt_bias, ssd_a_log, ssd_d, ssd_norm, s5_lam_re, s5_lam_im, s5_log_step, s5_b_re, s5_b_im, s5_c_re, s5_c_im, s5_d, s5_glu_w, s5_glu_b, w_branch, w_out, ffn_w_gate, ffn_w_up, ffn_w_down):
    batch, seq, d = x.shape
    ctx_len = ctx.shape[1]
    depth = ada_w.shape[0]
    nl, nc = batch * seq, batch * ctx_len
    dims = dict(batch=batch, seq=seq, ctx=ctx_len, nl=nl, nc=nc)
    assert batch < MOD_ROWS and seq % TOK_TILE == 0 and ctx_len % TOK_TILE == 0 and seq % ctx_len == 0
    assert nc % (2 * TOK_TILE) == 0 and seq % GRID_W == 0

    xs = (x.reshape(nl, d), ctx.reshape(nc, d))
    cc =jnp.concatenate([c, c_ctx[None], jnp.zeros((MOD_ROWS - batch - 1, d), F32)], axis=0)
    mod_all = _ada(cc, ada_w, ada_b).reshape(depth, MOD_ROWS, 6, d)
    rope_tabs = _rope_tables(seq, ctx_len)
    n_dt = 2 * SSD_HEADS
    pad = lambda v, before: jnp.concatenate(
        [jnp.zeros((1, before), F32), v.reshape(1, n_dt), jnp.zeros((1, LANES - n_dt - before), F32)], axis=1)
    tm_all = _pick_tile(nl + nc, (2176, 1088, 512, 256))
    w_in_t = jnp.swapaxes(w_in, 1, 2)

    for l in range(depth):
        last = l == depth - 1
        with_ctx = not last
        lam_init = 0.8 - 0.6 * math.exp(-0.3 * l)
        mod = mod_all[l]
        rows = nl + nc if with_ctx else nl
        tm = _pick_tile(rows, (1088, 1024, 512))

        h = _norm_mod(xs, norm_mix_pre[l], mod, 0, dims)
        p = _proj(h, w_in_t, l, 0, REF_DT_OFF, BF16, tm_all, "in_proj_main")
        u = _proj(h, w_in_t, l, REF_U_OFF, S5_WIDTH, F32, tm_all, "in_proj_u")
        gates = _proj(h, w_in_t, l, REF_U_OFF + S5_WIDTH, N_BRANCH * d, BF16, tm_all, "in_proj_gates")

        y_attn = _attention(p, rope_tabs, da_lambda[l], da_subln[l], lam_init, with_ctx, dims)

        xbc, dd = _ssd_prep(p, ssd_conv_w[l], ssd_conv_b[l], h, w_in_t, l,
                            pad(ssd_dt_bias[l], 0), pad(ssd_a_log[l], n_dt), dims)
        y_f = _ssd_scan(xbc, dd, 0, dims, with_ctx)
        dskip = jnp.repeat(ssd_d[l], SSD_HEAD_DIM).reshape(1, SSD_WIDTH)
        y_ssd = _ssd_scan(xbc, dd, 1, dims, with_ctx,
                          final_args=(y_f, p, dskip, ssd_norm[l].reshape(1, SSD_WIDTH)))

        s5p = _s5_params(s5_lam_re[l], s5_lam_im[l], s5_log_step[l], s5_b_re[l], s5_b_im[l],
                         s5_c_re[l], s5_c_im[l], s5_d[l])
        yg = _s5(u, s5p, dims)
        y_s5 = _glu(yg, s5_glu_w, s5_glu_b.reshape(depth, 1, 2 * S5_WIDTH), l, rows, tm)

        g = _merge(y_attn, y_ssd, y_s5, w_branch, l, gates, rows, tm)
        xt, h2 = _out_proj(g, _cast_bf16(w_out, l), xs, mod, norm_mix_post[l], norm_ffn_pre[l], dims)
        act = _ffn_up(h2, ffn_w_gate, ffn_w_up, l, tm)
        xt = _ffn_down(act, _cast_bf16(ffn_w_down, l), xt, mod, norm_ffn_post[l], dims)
        xs = (xt,)

    return xt[:nl].reshape(batch, seq, d)
```
